```python
import math
import jax
import jax.numpy as jnp
from jax import lax
import numpy as np

D_MODEL = 1024
BATCH = 4
SEQ = 4096
DEPTH = 2

GRID_W = 64

NA_HEADS = 8
NA_HEAD_DIM = 64
NA_WIN_ROWS = 8
NA_WIN_COLS = 16
NA_QBLOCK = 16
NA_KBLOCK = NA_QBLOCK + NA_WIN_COLS
WIDTH_A = NA_HEADS * NA_HEAD_DIM

GLA_HEADS = 4
GLA_DK = 64
GLA_DV = 128
GLA_RANK = 16
GLA_GATE_NORM = 16.0
WIDTH_B = GLA_HEADS * GLA_DV

HGRN_HEADS = 4
HGRN_DIM = 128
WIDTH_C = HGRN_HEADS * HGRN_DIM

SSD_HEADS = 8
SSD_HEAD_DIM = 64
SSD_GROUPS = 2
SSD_STATE = 128
SSD_CONV = 4
SSD_CHUNK = 64
WIDTH_D = SSD_HEADS * SSD_HEAD_DIM
SSD_CONV_CH = WIDTH_D + 2 * SSD_GROUPS * SSD_STATE
CONV_PAD = (SSD_CONV // 2, (SSD_CONV - 1) // 2)

REC_CHUNK = 16

N_EVEN = (DEPTH + 1) // 2
N_ODD = DEPTH // 2
EVEN_SPLITS = (WIDTH_A, WIDTH_A, WIDTH_A, WIDTH_A, GLA_HEADS * GLA_DK, GLA_HEADS * GLA_DK, WIDTH_B, WIDTH_B, GLA_RANK, GLA_RANK)
ODD_SPLITS = (WIDTH_C, WIDTH_C, WIDTH_C, WIDTH_C, WIDTH_C, WIDTH_D, SSD_CONV_CH, SSD_HEADS, SSD_HEADS)
EVEN_IN = 4 * WIDTH_A + 2 * GLA_HEADS * GLA_DK + 2 * WIDTH_B + 2 * GLA_RANK
ODD_IN = 5 * WIDTH_C + WIDTH_D + SSD_CONV_CH + 2 * SSD_HEADS

DEEPNORM_ALPHA = (2 * DEPTH) ** 0.25
DEEPNORM_BETA = (8 * DEPTH) ** -0.25
LN_EPS = 1e-5
RMS_EPS = 1e-6

kernel_name = 'hybrid_na_gla_hgrn2_ssd_encoder'


def split_cols(t, sizes):
    idx = np.cumsum(np.array(sizes))[:-1].tolist()
    return jnp.split(t, idx, axis=-1)


def layer_norm(x, g, b):
    xf = x.astype(jnp.float32)
    mu = jnp.mean(xf, -1, keepdims=True)
    var = jnp.mean(jnp.square(xf - mu), -1, keepdims=True)
    return ((xf - mu) * lax.rsqrt(var + LN_EPS) * g + b).astype(x.dtype)


def rms_norm(x, g):
    xf = x.astype(jnp.float32)
    return xf * lax.rsqrt(jnp.mean(jnp.square(xf), -1, keepdims=True) + RMS_EPS) * g.astype(jnp.float32)


def neighbourhood_attention(q, k, v, rpb):
    bsz, seq, heads, dh = q.shape
    rows = seq // GRID_W
    kr = min(NA_WIN_ROWS, rows)
    ncb = GRID_W // NA_QBLOCK
    r = jnp.arange(rows)
    row_start = jnp.clip(r - kr // 2, 0, rows - kr)
    key_rows = row_start[:, None] + jnp.arange(kr)[None, :]
    blk_start = jnp.clip(jnp.arange(ncb) * NA_QBLOCK - NA_WIN_COLS // 2, 0, GRID_W - NA_KBLOCK)
    key_cols = blk_start[:, None] + jnp.arange(NA_KBLOCK)[None, :]
    q_cols = (jnp.arange(ncb) * NA_QBLOCK)[:, None] + jnp.arange(NA_QBLOCK)[None, :]
    col_start = jnp.clip(q_cols - NA_WIN_COLS // 2, 0, GRID_W - NA_WIN_COLS)
    kc = key_cols[:, None, :]
    cs = col_start[:, :, None]
    col_ok = (kc >= cs) & (kc < cs + NA_WIN_COLS)
    nk = kr * NA_KBLOCK
    valid = jnp.broadcast_to(col_ok[:, :, None, :], (ncb, NA_QBLOCK, kr, NA_KBLOCK)).reshape(ncb, NA_QBLOCK, nk)
    dr_idx = key_rows - r[:, None] + NA_WIN_ROWS - 1
    dc_idx = jnp.clip(kc - q_cols[:, :, None], 1 - NA_WIN_COLS, NA_WIN_COLS - 1) + NA_WIN_COLS - 1
    bias = rpb[:, dr_idx[:, None, None, :, None], dc_idx[None, :, :, None, :]]
    bias = bias.reshape(heads, rows, ncb, NA_QBLOCK, nk).transpose(1, 2, 0, 3, 4).astype(jnp.float32)
    k_grid = k.reshape(bsz, rows, GRID_W, heads, dh)
    v_grid = v.reshape(bsz, rows, GRID_W, heads, dh)
    g_r = key_rows[:, None, :, None]
    g_c = key_cols[None, :, None, :]
    kg = k_grid[:, g_r, g_c].reshape(bsz, rows, ncb, nk, heads, dh)
    vg = v_grid[:, g_r, g_c].reshape(bsz, rows, ncb, nk, heads, dh)
    qb = q.reshape(bsz, rows, ncb, NA_QBLOCK, heads, dh)
    s = jnp.einsum('brnqhd,brnkhd->brnhqk', qb, kg).astype(jnp.float32) * (dh ** -0.5) + bias[None]
    s = jnp.where(valid[None, None, :, None], s, -jnp.inf)
    p = jax.nn.softmax(s, axis=-1).astype(v.dtype)
    o = jnp.einsum('brnhqk,brnkhd->brnqhd', p, vg)
    return o.reshape(bsz, seq, heads, dh)


def chunked_gated_recurrence(q, k, v, log_f):
    bsz, heads, seq, kd = q.shape
    vd = v.shape[-1]
    n = seq // REC_CHUNK
    q, k, log_f = [t.astype(jnp.float32).reshape(bsz, heads, n, REC_CHUNK, kd) for t in (q, k, log_f)]
    v = v.astype(jnp.float32).reshape(bsz, heads, n, REC_CHUNK, vd)
    b = jnp.cumsum(log_f, axis=3)
    mask = jnp.tril(jnp.ones((REC_CHUNK, REC_CHUNK), dtype=bool))
    diff = b[:, :, :, :, None, :] - b[:, :, :, None, :, :]
    decay_ij = jnp.exp(jnp.where(mask[:, :, None], diff, -jnp.inf))
    attn = jnp.einsum('bhnik,bhnjk,bhnijk->bhnij', q, k, decay_ij)
    o = jnp.einsum('bhnij,bhnjv->bhniv', attn, v)
    b_last = b[:, :, :, -1:, :]
    u = jnp.einsum('bhnjk,bhnjv->bhnkv', k * jnp.exp(b_last - b), v)
    chunk_decay = jnp.exp(b_last[:, :, :, 0, :])

    def step(s, inp):
        d, u_n = inp
        return d[..., None] * s + u_n, s

    s0 = jnp.zeros((bsz, heads, kd, vd), jnp.float32)
    _, s_prev = lax.scan(step, s0, (jnp.moveaxis(chunk_decay, 2, 0), jnp.moveaxis(u, 2, 0)))
    s_prev = jnp.moveaxis(s_prev, 0, 2)
    o = o + jnp.einsum('bhnik,bhnkv->bhniv', q * jnp.exp(b), s_prev)
    return o.reshape(bsz, heads, seq, vd)


def bidir_gated_recurrence(q, v, k_fwd, lf_fwd, k_bwd, lf_bwd):
    rev = lambda t: jnp.flip(t, axis=2)
    fwd = chunked_gated_recurrence(q, k_fwd, v, lf_fwd)
    bwd = rev(chunked_gated_recurrence(rev(q), rev(k_bwd), rev(v), rev(lf_bwd)))
    return fwd + bwd


def ssd_chunked(x, a, bm, cm):
    bsz, seq, heads, hp = x.shape
    groups, ns = bm.shape[2], bm.shape[3]
    rep = heads // groups
    n = seq // SSD_CHUNK
    x = x.astype(jnp.float32).reshape(bsz, n, SSD_CHUNK, groups, rep, hp)
    a = a.astype(jnp.float32).reshape(bsz, n, SSD_CHUNK, groups, rep).transpose(0, 3, 4, 1, 2)
    bm = bm.astype(jnp.float32).reshape(bsz, n, SSD_CHUNK, groups, ns)
    cm = cm.astype(jnp.float32).reshape(bsz, n, SSD_CHUNK, groups, ns)
    a_cum = jnp.cumsum(a, axis=-1)
    mask = jnp.tril(jnp.ones((SSD_CHUNK, SSD_CHUNK), dtype=bool))
    seg = jnp.exp(jnp.where(mask, a_cum[..., :, None] - a_cum[..., None, :], -jnp.inf))
    cb = jnp.einsum('bnigs,bnjgs->bgnij', cm, bm)
    y = jnp.einsum('bgnij,bgrnij,bnjgrp->bnigrp', cb, seg, x)
    decay_states = jnp.exp(a_cum[..., -1:] - a_cum)
    states = jnp.einsum('bnjgs,bgrnj,bnjgrp->bngrps', bm, decay_states, x)
    chunk_decay = jnp.exp(a_cum[..., -1])

    def step(s, inp):
        d, st = inp
        return d[..., None, None] * s + st, s

    s0 = jnp.zeros((bsz, groups, rep, hp, ns), jnp.float32)
    _, s_prev = lax.scan(step, s0, (jnp.moveaxis(chunk_decay, 3, 0), jnp.moveaxis(states, 1, 0)))
    s_prev = jnp.moveaxis(s_prev, 0, 1)
    y = y + jnp.einsum('bnigs,bgrni,bngrps->bnigrp', cm, jnp.exp(a_cum), s_prev)
    return y.reshape(bsz, seq, heads, hp)


def even_mixer(h, w_in, rpb, gla_w_up, gla_b, gla_norm_g, w_out):
    bsz, seq, _ = h.shape
    aq, ak, av, ag, bq, bk, bv, bg, lr_f, lr_b = split_cols(h @ w_in, EVEN_SPLITS)

    def heads(t, nh):
        return t.reshape(bsz, seq, nh, -1)

    def bhld(t, nh):
        return heads(t, nh).transpose(0, 2, 1, 3)

    ya = neighbourhood_attention(heads(aq, NA_HEADS), heads(ak, NA_HEADS), heads(av, NA_HEADS), rpb)
    ya = ya.reshape(bsz, seq, WIDTH_A) * jax.nn.silu(ag)

    def log_gate(lr, d):
        z = (lr @ gla_w_up[d] + gla_b[d]).astype(jnp.float32)
        return bhld(jax.nn.log_sigmoid(z) / GLA_GATE_NORM, GLA_HEADS)

    qb = bhld(bq, GLA_HEADS) * (GLA_DK ** -0.5)
    kb = bhld(bk, GLA_HEADS)
    ob = bidir_gated_recurrence(qb, bhld(bv, GLA_HEADS), kb, log_gate(lr_f, 0), kb, log_gate(lr_b, 1))
    ob = rms_norm(ob, gla_norm_g).transpose(0, 2, 1, 3).reshape(bsz, seq, WIDTH_B)
    yb = ob.astype(h.dtype) * jax.nn.silu(bg)
    return jnp.concatenate([ya, yb], axis=-1) @ w_out


def odd_mixer(h, w_in, lb, hgrn_norm_g, conv_w, conv_b, dt_bias, a_log, d_skip, ssm_norm_g, w_out):
    bsz, seq, _ = h.shape
    cq, cf_f, cf_b, ci, cg, dz, dxbc, dt_f, dt_b = split_cols(h @ w_in, ODD_SPLITS)

    def bhld(t, nh):
        return t.reshape(bsz, seq, nh, -1).transpose(0, 2, 1, 3)

    log_lb = jnp.log(lb)
    log_ub = jnp.log1p(-lb)

    def forget(z):
        z = z.astype(jnp.float32)
        log_f = jnp.logaddexp(log_lb, log_ub + jax.nn.log_sigmoid(z))
        k = (1.0 - lb) * jax.nn.sigmoid(-z)
        return bhld(k, HGRN_HEADS), bhld(log_f, HGRN_HEADS)

    k_f, lf_f = forget(cf_f)
    k_b, lf_b = forget(cf_b)
    qc = bhld(cq, HGRN_HEADS) * (HGRN_DIM ** -0.5)
    oc = bidir_gated_recurrence(qc, bhld(ci, HGRN_HEADS), k_f, lf_f, k_b, lf_b)
    oc = rms_norm(oc, hgrn_norm_g).transpose(0, 2, 1, 3).reshape(bsz, seq, WIDTH_C)
    yc = oc.astype(h.dtype) * jax.nn.silu(cg)

    xbc = lax.conv_general_dilated(dxbc, conv_w[:, None, :], (1,), [CONV_PAD],
                                   dimension_numbers=('NWC', 'WIO', 'NWC'),
                                   feature_group_count=SSD_CONV_CH)
    xbc = jax.nn.silu(xbc + conv_b)
    xs, bm, cm = split_cols(xbc, (WIDTH_D, SSD_GROUPS * SSD_STATE, SSD_GROUPS * SSD_STATE))
    xs = xs.reshape(bsz, seq, SSD_HEADS, SSD_HEAD_DIM).astype(jnp.float32)
    bm = bm.reshape(bsz, seq, SSD_GROUPS, SSD_STATE)
    cm = cm.reshape(bsz, seq, SSD_GROUPS, SSD_STATE)

    def dir_inputs(dt_raw, d):
        dt = jax.nn.softplus(dt_raw.astype(jnp.float32) + dt_bias[d].astype(jnp.float32))
        return xs * dt[..., None], dt * (-jnp.exp(a_log[d].astype(jnp.float32)))

    x_fw, a_fw = dir_inputs(dt_f, 0)
    x_bw, a_bw = dir_inputs(dt_b, 1)
    rev = lambda t: jnp.flip(t, axis=1)
    y = (ssd_chunked(x_fw, a_fw, bm, cm)
         + rev(ssd_chunked(rev(x_bw), rev(a_bw), rev(bm), rev(cm)))
         + d_skip.astype(jnp.float32)[:, None] * xs)
    y = rms_norm(y.reshape(bsz, seq, WIDTH_D) * jax.nn.silu(dz.astype(jnp.float32)), ssm_norm_g)
    yd = y.astype(h.dtype)
    return jnp.concatenate([yc, yd], axis=-1) @ w_out


def setup_inputs(seed: int = 0) -> dict:
    key = jax.random.key(seed)
    ks = jax.random.split(key, 24)

    def nrm(k, shape, s):
        return jax.random.normal(k, shape, jnp.float32) * s

    dt0 = jnp.exp(jax.random.uniform(ks[19], (N_ODD, 2, SSD_HEADS), jnp.float32)
                  * (math.log(0.1) - math.log(0.001)) + math.log(0.001))
    return {
        'x': nrm(ks[0], (BATCH, SEQ, D_MODEL), 1.0),
        'c': nrm(ks[1], (BATCH, D_MODEL), 1.0),
        'ada_w': nrm(ks[2], (DEPTH, D_MODEL, 3 * D_MODEL), D_MODEL ** -0.5),
        'ada_b': nrm(ks[3], (DEPTH, 3 * D_MODEL), 0.01),
        'ln_g': 1.0 + nrm(ks[4], (DEPTH, D_MODEL), 0.01),
        'ln_b': nrm(ks[5], (DEPTH, D_MODEL), 0.01),
        'e_w_in': nrm(ks[6], (N_EVEN, D_MODEL, EVEN_IN), D_MODEL ** -0.5),
        'e_rpb': nrm(ks[7], (N_EVEN, NA_HEADS, 2 * NA_WIN_ROWS - 1, 2 * NA_WIN_COLS - 1), 0.05),
        'e_gla_w_up': nrm(ks[8], (N_EVEN, 2, GLA_RANK, GLA_HEADS * GLA_DK), GLA_RANK ** -0.5),
        'e_gla_b': nrm(ks[9], (N_EVEN, 2, GLA_HEADS * GLA_DK), 0.01),
        'e_gla_norm_g': 1.0 + nrm(ks[10], (N_EVEN, GLA_DV), 0.01),
        'e_w_out': nrm(ks[11], (N_EVEN, WIDTH_A + WIDTH_B, D_MODEL), (WIDTH_A + WIDTH_B) ** -0.5 * DEEPNORM_BETA),
        'o_w_in': nrm(ks[12], (N_ODD, D_MODEL, ODD_IN), D_MODEL ** -0.5),
        'hgrn_lb': nrm(ks[13], (DEPTH, WIDTH_C), 1.0),
        'o_hgrn_norm_g': 1.0 + nrm(ks[14], (N_ODD, HGRN_DIM), 0.01),
        'o_conv_w': nrm(ks[15], (N_ODD, SSD_CONV, SSD_CONV_CH), SSD_CONV ** -0.5),
        'o_conv_b': nrm(ks[16], (N_ODD, SSD_CONV_CH), 0.01),
        'o_dt_bias': dt0 + jnp.log(-jnp.expm1(-dt0)),
        'o_a_log': jnp.log(jax.random.uniform(ks[17], (N_ODD, 2, SSD_HEADS), jnp.float32, 1.0, 16.0)),
        'o_d_skip': 1.0 + nrm(ks[18], (N_ODD, SSD_HEADS), 0.01),
        'o_ssm_norm_g': 1.0 + nrm(ks[20], (N_ODD, WIDTH_D), 0.01),
        'o_w_out': nrm(ks[21], (N_ODD, WIDTH_C + WIDTH_D, D_MODEL), (WIDTH_C + WIDTH_D) ** -0.5 * DEEPNORM_BETA),
    }


def reference(x, c, ada_w, ada_b, ln_g, ln_b, e_w_in, e_rpb, e_gla_w_up, e_gla_b, e_gla_norm_g, e_w_out,
              o_w_in, hgrn_lb, o_hgrn_norm_g, o_conv_w, o_conv_b, o_dt_bias, o_a_log, o_d_skip,
              o_ssm_norm_g, o_w_out):
    lb_cum = jnp.cumsum(jax.nn.softmax(hgrn_lb.astype(jnp.float32), axis=0), axis=0)
    cond = jax.nn.silu(c)
    for l in range(DEPTH):
        mod = cond @ ada_w[l] + ada_b[l]
        shift, scale, gate = jnp.split(mod[:, None, :], 3, axis=-1)
        h = x * (1.0 + scale) + shift
        i = l // 2
        if l % 2 == 0:
            y = even_mixer(h, e_w_in[i], e_rpb[i], e_gla_w_up[i], e_gla_b[i], e_gla_norm_g[i], e_w_out[i])
        else:
            y = odd_mixer(h, o_w_in[i], lb_cum[l] - lb_cum[0], o_hgrn_norm_g[i], o_conv_w[i], o_conv_b[i],
                          o_dt_bias[i], o_a_log[i], o_d_skip[i], o_ssm_norm_g[i], o_w_out[i])
        x = layer_norm(DEEPNORM_ALPHA * x + gate * y, ln_g[l], ln_b[l])
    return x
```

```python
import functools
import math

import numpy as np
import jax
import jax.numpy as jnp
from jax import lax
from jax.experimental import pallas as pl
from jax.experimental.pallas import tpu as pltpu

F32 = jnp.float32
BF16 = jnp.bfloat16

GRID_W = 64
NA_HEADS, NA_HEAD_DIM = 8, 64
NA_WIN_ROWS, NA_WIN_COLS = 8, 16
WIDTH_A = NA_HEADS * NA_HEAD_DIM
GLA_HEADS, GLA_DK, GLA_DV, GLA_RANK = 4, 64, 128, 16
GLA_GATE_NORM = 16.0
WIDTH_B = GLA_HEADS * GLA_DV
HGRN_HEADS, HGRN_DIM = 4, 128
WIDTH_C = HGRN_HEADS * HGRN_DIM
SSD_HEADS, SSD_HEAD_DIM, SSD_GROUPS, SSD_STATE, SSD_CONV = 8, 64, 2, 128, 4
WIDTH_D = SSD_HEADS * SSD_HEAD_DIM
SSD_CONV_CH = WIDTH_D + 2 * SSD_GROUPS * SSD_STATE
DEPTH = 2
DEEPNORM_ALPHA = (2 * DEPTH) ** 0.25
LN_EPS = 1e-5
RMS_EPS = 1e-6

LANES = 128
SUBLANES = 8
VMEM_LIMIT = 56 * 1024 * 1024

ROW_TILE = 512
REC_CHUNK = 128
NA_ROWS_PER_STEP = 4
NEG_INF = float("-inf")


def _cparams(*sem):
    return pltpu.CompilerParams(dimension_semantics=sem, vmem_limit_bytes=VMEM_LIMIT)


def _dot(a, b):
    return jnp.dot(a, b, preferred_element_type=F32)


def _dot_nt(a, b):
    return lax.dot_general(a, b, (((1,), (1,)), ((), ())), preferred_element_type=F32)


def _dot_tn(a, b):
    return lax.dot_general(a, b, (((0,), (0,)), ((), ())), preferred_element_type=F32)


def _split3(x):
    hi = x.astype(BF16)
    r = x - hi.astype(F32)
    mid = r.astype(BF16)
    lo = (r - mid.astype(F32)).astype(BF16)
    return hi, mid, lo


def _sel_dot(sel, x):
    hi, mid, lo = _split3(x)
    return _dot(sel, hi) + _dot(sel, mid) + _dot(sel, lo)


def _dot_sel(x, sel):
    hi, mid, lo = _split3(x)
    return _dot(hi, sel) + _dot(mid, sel) + _dot(lo, sel)


def _sigmoid(x):
    return 1.0 / (1.0 + jnp.exp(-x))


def _silu(x):
    return x * _sigmoid(x)


def _log1pexp_neg_abs(x):
    return jnp.log1p(jnp.exp(-jnp.abs(x)))


def _log_sigmoid(x):
    return jnp.minimum(x, 0.0) - _log1pexp_neg_abs(x)


def _softplus(x):
    return jnp.maximum(x, 0.0) + _log1pexp_neg_abs(x)


def _ada_kernel(c_ref, w_ref, b_ref, o_ref):
    cond = _silu(c_ref[...]).astype(BF16)
    o_ref[...] = _dot(cond, w_ref[...].astype(BF16)) + b_ref[...]


def _ada_mod(c, ada_w, ada_b):
    bsz, d = c.shape
    bp = -(-bsz // SUBLANES) * SUBLANES
    n3 = ada_w.shape[-1]
    tn = 512
    c_pad = jnp.zeros((bp, d), F32).at[:bsz].set(c)
    out = pl.pallas_call(
        _ada_kernel,
        grid=(DEPTH, n3 // tn),
        in_specs=[pl.BlockSpec((bp, d), lambda l, j: (0, 0)),
                  pl.BlockSpec((None, d, tn), lambda l, j: (l, 0, j)),
                  pl.BlockSpec((None, 1, tn), lambda l, j: (l, 0, j))],
        out_specs=pl.BlockSpec((None, bp, tn), lambda l, j: (l, 0, j)),
        out_shape=jax.ShapeDtypeStruct((DEPTH, bp, n3), F32),
        compiler_params=_cparams("arbitrary", "arbitrary"),
        name="ada_mod",
    )(c_pad, ada_w, ada_b.reshape(DEPTH, 1, n3))
    return out[:, :bsz].reshape(DEPTH, bsz, 1, n3)


def _modulated(x_ref, mod_ref):
    d = x_ref.shape[-1]
    mod = mod_ref[...]
    return (x_ref[...] * (1.0 + mod[:, d:2 * d]) + mod[:, :d]).astype(BF16)


def _proj_even_kernel(x_ref, mod_ref, w_ref, wup_ref, gb_ref,
                      aq_ref, ak_ref, av_ref, ag_ref, bq_ref, bk_ref, bv_ref, bg_ref, lff_ref, lfb_ref):
    h = _modulated(x_ref, mod_ref)

    def mm(lo, hi):
        return _dot(h, w_ref[:, lo:hi])

    a, kb = WIDTH_A, GLA_HEADS * GLA_DK
    aq_ref[...] = (mm(0, a) * (NA_HEAD_DIM ** -0.5)).astype(BF16)
    ak_ref[...] = mm(a, 2 * a).astype(BF16)
    av_ref[...] = mm(2 * a, 3 * a).astype(BF16)
    ag_ref[...] = mm(3 * a, 4 * a)
    o = 4 * a
    bq_ref[...] = (mm(o, o + kb) * (GLA_DK ** -0.5)).astype(BF16)
    bk_ref[...] = mm(o + kb, o + 2 * kb).astype(BF16)
    o += 2 * kb
    bv_ref[...] = mm(o, o + WIDTH_B).astype(BF16)
    bg_ref[...] = mm(o + WIDTH_B, o + 2 * WIDTH_B)
    o += 2 * WIDTH_B
    lr = mm(o, o + LANES)
    lr_hi = lr.astype(BF16)
    lr_lo = (lr - lr_hi.astype(F32)).astype(BF16)
    for d, out_ref in enumerate((lff_ref, lfb_ref)):
        wu = wup_ref[d]
        wu_hi = wu.astype(BF16)
        wu_lo = (wu - wu_hi.astype(F32)).astype(BF16)
        z = _dot(lr_hi, wu_hi) + _dot(lr_lo, wu_hi) + _dot(lr_hi, wu_lo) + gb_ref[d]
        out_ref[...] = _log_sigmoid(z) * (1.0 / GLA_GATE_NORM)


def _proj_even(x, mod, w_in, gla_w_up, gla_b):
    bsz, seq, d = x.shape
    kb = GLA_HEADS * GLA_DK
    n_main = 4 * WIDTH_A + 2 * kb + 2 * WIDTH_B
    w = jnp.zeros((d, n_main + LANES), BF16).at[:, :w_in.shape[1]].set(w_in.astype(BF16))
    wup = jnp.zeros((2, LANES, kb), F32)
    wup = wup.at[0, :GLA_RANK].set(gla_w_up[0]).at[1, GLA_RANK:2 * GLA_RANK].set(gla_w_up[1])
    tm = min(ROW_TILE, seq)
    row = lambda n: pl.BlockSpec((None, tm, n), lambda b, i: (b, i, 0))
    shp = lambda n, dt: jax.ShapeDtypeStruct((bsz, seq, n), dt)
    widths = [(WIDTH_A, BF16), (WIDTH_A, BF16), (WIDTH_A, BF16), (WIDTH_A, F32),
              (kb, BF16), (kb, BF16), (WIDTH_B, BF16), (WIDTH_B, F32), (kb, F32), (kb, F32)]
    return pl.pallas_call(
        _proj_even_kernel,
        grid=(bsz, seq // tm),
        in_specs=[row(d),
                  pl.BlockSpec((None, 1, mod.shape[-1]), lambda b, i: (b, 0, 0)),
                  pl.BlockSpec(w.shape, lambda b, i: (0, 0)),
                  pl.BlockSpec(wup.shape, lambda b, i: (0, 0, 0)),
                  pl.BlockSpec((2, 1, kb), lambda b, i: (0, 0, 0))],
        out_specs=[row(n) for n, _ in widths],
        out_shape=[shp(n, dt) for n, dt in widths],
        compiler_params=_cparams("parallel", "arbitrary"),
        name="proj_even",
    )(x, mod, w, wup, gla_b.reshape(2, 1, kb))


def _proj_odd_kernel(x_ref, mod_ref, w_ref,
                     cq_ref, cff_ref, cfb_ref, ci_ref, cg_ref, dz_ref, dxbc_ref, dt_ref):
    h = _modulated(x_ref, mod_ref)

    def mm(lo, hi):
        return _dot(h, w_ref[:, lo:hi])

    c = WIDTH_C
    cq_ref[...] = (mm(0, c) * (HGRN_DIM ** -0.5)).astype(BF16)
    cff_ref[...] = mm(c, 2 * c)
    cfb_ref[...] = mm(2 * c, 3 * c)
    ci_ref[...] = mm(3 * c, 4 * c).astype(BF16)
    cg_ref[...] = mm(4 * c, 5 * c)
    o = 5 * c
    dz_ref[...] = mm(o, o + WIDTH_D)
    o += WIDTH_D
    dxbc_ref[...] = mm(o, o + SSD_CONV_CH)
    o += SSD_CONV_CH
    dt_ref[...] = mm(o, o + LANES)


def _proj_odd(x, mod, w_in):
    bsz, seq, d = x.shape
    n_main = 5 * WIDTH_C + WIDTH_D + SSD_CONV_CH
    w = jnp.zeros((d, n_main + LANES), BF16).at[:, :w_in.shape[1]].set(w_in.astype(BF16))
    tm = min(ROW_TILE, seq)
    row = lambda n: pl.BlockSpec((None, tm, n), lambda b, i: (b, i, 0))
    shp = lambda n, dt: jax.ShapeDtypeStruct((bsz, seq, n), dt)
    widths = [(WIDTH_C, BF16), (WIDTH_C, F32), (WIDTH_C, F32), (WIDTH_C, BF16), (WIDTH_C, F32),
              (WIDTH_D, F32), (SSD_CONV_CH, F32), (LANES, F32)]
    return pl.pallas_call(
        _proj_odd_kernel,
        grid=(bsz, seq // tm),
        in_specs=[row(d),
                  pl.BlockSpec((None, 1, mod.shape[-1]), lambda b, i: (b, 0, 0)),
                  pl.BlockSpec(w.shape, lambda b, i: (0, 0))],
        out_specs=[row(n) for n, _ in widths],
        out_shape=[shp(n, dt) for n, dt in widths],
        compiler_params=_cparams("parallel", "arbitrary"),
        name="proj_odd",
    )(x, mod, w)


def _out_ln_kernel(ya_ref, yb_ref, w_ref, x_ref, mod_ref, g_ref, b_ref, o_ref):
    d = x_ref.shape[-1]
    wa = ya_ref.shape[-1]
    y = _dot(ya_ref[...], w_ref[:wa, :]) + _dot(yb_ref[...], w_ref[wa:, :])
    gate = mod_ref[...][:, 2 * d:]
    t = DEEPNORM_ALPHA * x_ref[...] + gate * y
    mu = jnp.mean(t, axis=-1, keepdims=True)
    tc = t - mu
    var = jnp.mean(tc * tc, axis=-1, keepdims=True)
    o_ref[...] = tc * lax.rsqrt(var + LN_EPS) * g_ref[...] + b_ref[...]


def _out_ln(ya, yb, w_out, x, mod, ln_g, ln_b):
    bsz, seq, d = x.shape
    tm = min(ROW_TILE, seq)
    w = w_out.astype(BF16)
    row = lambda n: pl.BlockSpec((None, tm, n), lambda b, i: (b, i, 0))
    vec = pl.BlockSpec((1, d), lambda b, i: (0, 0))
    return pl.pallas_call(
        _out_ln_kernel,
        grid=(bsz, seq // tm),
        in_specs=[row(ya.shape[-1]), row(yb.shape[-1]),
                  pl.BlockSpec(w.shape, lambda b, i: (0, 0)),
                  row(d),
                  pl.BlockSpec((None, 1, mod.shape[-1]), lambda b, i: (b, 0, 0)),
                  vec, vec],
        out_specs=row(d),
        out_shape=jax.ShapeDtypeStruct((bsz, seq, d), F32),
        compiler_params=_cparams("parallel", "arbitrary"),
        name="out_ln",
    )(ya, yb, w, x, mod, ln_g.reshape(1, d), ln_b.reshape(1, d))


NA_BIAS_TYPES = NA_WIN_ROWS
NA_KEYS = NA_WIN_ROWS * GRID_W


def _na_bias_kernel(rpb_ref, o_ref):
    t = pl.program_id(0)
    shape = (GRID_W, LANES)
    q = lax.broadcasted_iota(jnp.int32, shape, 0)
    lane = lax.broadcasted_iota(jnp.int32, shape, 1)
    kc = lane & (GRID_W - 1)
    dc = kc - q + (NA_WIN_COLS - 1)
    cs = jnp.clip(q - NA_WIN_COLS // 2, 0, GRID_W - NA_WIN_COLS)
    valid = (kc >= cs) & (kc < cs + NA_WIN_COLS)
    upper = lane >= GRID_W
    n_dr, n_dc = 2 * NA_WIN_ROWS - 1, 2 * NA_WIN_COLS - 1
    for h in range(NA_HEADS):
        for c in range(NA_KEYS // LANES):
            base0 = (h * n_dr + (2 * c + NA_WIN_ROWS - 1 - t)) * n_dc
            base1 = base0 + n_dc

            def body(d, acc, base0=base0, base1=base1):
                val = jnp.where(upper, rpb_ref[base1 + d], rpb_ref[base0 + d])
                return jnp.where(dc == d, val, acc)

            acc = lax.fori_loop(0, n_dc, body, jnp.zeros(shape, F32))
            o_ref[h, :, c * LANES:(c + 1) * LANES] = jnp.where(valid, acc, NEG_INF)


def _na_bias(rpb):
    return pl.pallas_call(
        _na_bias_kernel,
        grid=(NA_BIAS_TYPES,),
        in_specs=[pl.BlockSpec(memory_space=pltpu.SMEM)],
        out_specs=pl.BlockSpec((None, NA_HEADS, GRID_W, NA_KEYS), lambda t: (t, 0, 0, 0)),
        out_shape=jax.ShapeDtypeStruct((NA_BIAS_TYPES, NA_HEADS, GRID_W, NA_KEYS), F32),
        compiler_params=_cparams("arbitrary"),
        name="na_bias",
    )(rpb.reshape(-1))


def _na_kernel(q_ref, k_ref, v_ref, g_ref, bias_ref, o_ref, *, n_rows):
    i = pl.program_id(1)
    lane = lax.broadcasted_iota(jnp.int32, (1, LANES), 1)
    head_mask = [(lane < NA_HEAD_DIM).astype(BF16), (lane >= NA_HEAD_DIM).astype(BF16)]
    half = NA_WIN_ROWS // 2
    for rr in range(NA_ROWS_PER_STEP):
        r = i * NA_ROWS_PER_STEP + rr
        row_start = jnp.clip(r - half, 0, n_rows - NA_WIN_ROWS)
        t = jnp.where(r < half, r, jnp.where(r > n_rows - half, r - (n_rows - NA_WIN_ROWS), half))
        koff = pl.multiple_of(row_start * GRID_W, GRID_W)
        rows = slice(rr * GRID_W, (rr + 1) * GRID_W)
        for p in range(WIDTH_A // LANES):
            cols = slice(p * LANES, (p + 1) * LANES)
            qp = q_ref[rows, cols]
            kp = k_ref[pl.ds(koff, NA_KEYS), cols]
            vp = v_ref[pl.ds(koff, NA_KEYS), cols]
            outs = []
            for hh in range(2):
                s = _dot_nt(qp * head_mask[hh], kp) + bias_ref[t, 2 * p + hh]
                m = jnp.max(s, axis=-1, keepdims=True)
                e = jnp.exp(s - m)
                l = jnp.sum(e, axis=-1, keepdims=True)
                outs.append(_dot(e.astype(BF16), vp) / l)
            o = jnp.where(lane < NA_HEAD_DIM, outs[0], outs[1])
            o_ref[rows, cols] = (o * _silu(g_ref[rows, cols])).astype(BF16)


def _neighbourhood_attention(aq, ak, av, ag, bias):
    bsz, seq, w = aq.shape
    n_rows = seq // GRID_W
    tq = NA_ROWS_PER_STEP * GRID_W
    row = pl.BlockSpec((None, tq, w), lambda b, i: (b, i, 0))
    full = pl.BlockSpec((None, seq, w), lambda b, i: (b, 0, 0))
    return pl.pallas_call(
        functools.partial(_na_kernel, n_rows=n_rows),
        grid=(bsz, n_rows // NA_ROWS_PER_STEP),
        in_specs=[row, full, full, row,
                  pl.BlockSpec(bias.shape, lambda b, i: (0, 0, 0, 0))],
        out_specs=row,
        out_shape=jax.ShapeDtypeStruct((bsz, seq, w), BF16),
        compiler_params=_cparams("parallel", "arbitrary"),
        name="na_attn",
    )(aq, ak, av, ag, bias)


def _rec_consts(chunk, reverse):
    nlev = int(math.log2(chunk))
    assert 1 << nlev == chunk
    w = np.zeros(((nlev + 2), chunk, chunk), np.float32)
    lv = np.full((chunk, chunk), -1, np.int32)
    idx = np.arange(chunk)
    for l in range(nlev):
        s = 1 << l
        off = idx % (2 * s)
        mid = idx - off + s - 1
        for p in range(chunk):
            if off[p] >= s:
                w[l, p, mid[p] + 1:p + 1] = 1.0
            else:
                w[l, p, p + 1:mid[p] + 1] = 1.0
        same = (idx[:, None] // (2 * s)) == (idx[None, :] // (2 * s))
        lv[same & (off[:, None] >= s) & (off[None, :] < s)] = l
    lv[idx, idx] = nlev
    w[nlev] = (idx[None, :] <= idx[:, None])
    w[nlev + 1] = (idx[None, :] > idx[:, None])
    if reverse:
        w = w[:, ::-1, ::-1]
        lv = lv[::-1, ::-1]
    return (jnp.asarray(w.reshape(-1, chunk), BF16), jnp.asarray(np.ascontiguousarray(lv)), nlev)


def _rec_kernel(*refs, mode, final, reverse, heads, dk, dv, nlev):
    chunk = REC_CHUNK
    if mode == "gla":
        q_ref, k_ref, lf_ref, v_ref, w_ref, lv_ref = refs[:6]
        rest = refs[6:]
    else:
        q_ref, z_ref, lbraw_ref, v_ref, w_ref, lv_ref = refs[:6]
        rest = refs[6:]
    if final:
        prev_ref, gate_ref, ng_ref, o_ref, st_ref = rest
    else:
        o_ref, st_ref = rest

    @pl.when(pl.program_id(1) == 0)
    def _():
        st_ref[...] = jnp.zeros_like(st_ref)

    lane = lax.broadcasted_iota(jnp.int32, (1, LANES), 1)
    per_slab = LANES // dk
    lv = lv_ref[...]
    level_masks = [lv == l for l in range(nlev + 1)]
    last = 0 if reverse else chunk - 1
    w_sel = w_ref[...]

    for s in range(heads * dk // LANES):
        cols = slice(s * LANES, (s + 1) * LANES)
        q = q_ref[:, cols].astype(F32)
        if mode == "gla":
            k = k_ref[:, cols].astype(F32)
            lf = lf_ref[:, cols]
        else:
            lbr = lbraw_ref[:, cols]
            mx = jnp.maximum(lbr[0:1], lbr[1:2])
            e0, e1 = jnp.exp(lbr[0:1] - mx), jnp.exp(lbr[1:2] - mx)
            lb = e1 / (e0 + e1)
            z = z_ref[:, cols]
            k = (1.0 - lb) * _sigmoid(-z)
            a = jnp.log(lb)
            b = jnp.log1p(-lb) + _log_sigmoid(z)
            lf = jnp.maximum(a, b) + _log1pexp_neg_abs(a - b)
        x = jnp.exp(_sel_dot(w_sel, lf))
        blk = lambda l: x[l * chunk:(l + 1) * chunk]
        lhs = [(blk(l) * q).astype(BF16) for l in range(nlev)] + [q.astype(BF16)]
        rhs = [(blk(l) * k).astype(BF16) for l in range(nlev)] + [k.astype(BF16)]
        q_in = (blk(nlev) * q).astype(BF16)
        k_up = (blk(nlev + 1) * k).astype(BF16)
        decay = blk(nlev)[last:last + 1]
        st = st_ref[s]
        st_bf = st.astype(BF16)
        upd = jnp.zeros_like(st)
        for hh in range(per_slab):
            head = s * per_slab + hh
            if per_slab > 1:
                hm = ((lane >= hh * dk) & (lane < (hh + 1) * dk))
                hm_bf = hm.astype(BF16)
                mask_rhs = lambda t: t * hm_bf
            else:
                hm = None
                mask_rhs = lambda t: t
            a_mat = jnp.zeros((chunk, chunk), F32)
            for l in range(nlev + 1):
                a_mat = jnp.where(level_masks[l], _dot_nt(lhs[l], mask_rhs(rhs[l])), a_mat)
            vh = v_ref[:, head * dv:(head + 1) * dv]
            o = _dot(a_mat.astype(BF16), vh) + _dot_nt(mask_rhs(q_in), st_bf)
            u = _dot_tn(vh, k_up)
            upd = u if hm is None else upd + jnp.where(hm, u, 0.0)
            ocols = slice(head * dv, (head + 1) * dv)
            if final:
                tot = prev_ref[:, ocols] + o
                ms = jnp.mean(tot * tot, axis=-1, keepdims=True)
                y = tot * lax.rsqrt(ms + RMS_EPS) * ng_ref[...]
                o_ref[:, ocols] = (y * _silu(gate_ref[:, ocols])).astype(o_ref.dtype)
            else:
                o_ref[:, ocols] = o
        st_ref[s] = st * decay + upd


def _gated_recurrence(mode, q, kz, lf_or_lb, v, prev, gate, norm_g, *, reverse, heads, dk, dv):
    bsz, seq, _ = q.shape
    chunk = REC_CHUNK
    n = seq // chunk
    final = prev is not None
    w_sel, lv, nlev = _rec_consts(chunk, reverse)
    cidx = (lambda b, i: (b, n - 1 - i, 0)) if reverse else (lambda b, i: (b, i, 0))
    row = lambda width: pl.BlockSpec((None, chunk, width), cidx)
    const2 = lambda a: pl.BlockSpec(a.shape, lambda b, i: (0, 0))
    in_specs = [row(heads * dk), row(heads * dk),
                row(heads * dk) if mode == "gla" else const2(lf_or_lb),
                row(heads * dv), const2(w_sel), const2(lv)]
    args = [q, kz, lf_or_lb, v, w_sel, lv]
    if final:
        in_specs += [row(heads * dv), row(heads * dv), pl.BlockSpec((1, dv), lambda b, i: (0, 0))]
        args += [prev, gate, norm_g.reshape(1, dv)]
    return pl.pallas_call(
        functools.partial(_rec_kernel, mode=mode, final=final, reverse=reverse,
                          heads=heads, dk=dk, dv=dv, nlev=nlev),
        grid=(bsz, n),
        in_specs=in_specs,
        out_specs=row(heads * dv),
        out_shape=jax.ShapeDtypeStruct((bsz, seq, heads * dv), BF16 if final else F32),
        scratch_shapes=[pltpu.VMEM((heads * dk // LANES, dv, LANES), F32)],
        compiler_params=_cparams("parallel", "arbitrary"),
        name=f"{mode}_{'bwd' if reverse else 'fwd'}",
    )(*args)


def _ssd_consts(chunk, reverse):
    idx = np.arange(chunk)
    if reverse:
        tri = idx[None, :] >= idx[:, None]
        rest = idx[None, :] < idx[:, None]
    else:
        tri = idx[None, :] <= idx[:, None]
        rest = idx[None, :] > idx[:, None]
    sel = np.concatenate([tri, rest], axis=0).astype(np.float32)
    expand = np.zeros((LANES, WIDTH_D), np.float32)
    for h in range(SSD_HEADS):
        expand[h + (SSD_HEADS if reverse else 0), h * SSD_HEAD_DIM:(h + 1) * SSD_HEAD_DIM] = 1.0
    return jnp.asarray(sel, BF16), jnp.asarray(expand, BF16)


def _ssd_kernel(*refs, final, reverse, n_chunks):
    chunk = REC_CHUNK
    (x_ref, xprev_ref, xnext_ref, dt_ref, cw_ref, cb_ref, dtb_ref, alog_ref, sel_ref, exp_ref) = refs[:10]
    if final:
        prev_ref, dz_ref, dskip_ref, ng_ref, o_ref, ext_ref, st_ref = refs[10:]
    else:
        o_ref, ext_ref, st_ref = refs[10:]
    step = pl.program_id(1)
    c = (n_chunks - 1 - step) if reverse else step

    @pl.when(step == 0)
    def _():
        st_ref[...] = jnp.zeros_like(st_ref)

    halo = SUBLANES
    ext_ref[0:halo] = jnp.where(c == 0, 0.0, xprev_ref[...])
    ext_ref[halo:halo + chunk] = x_ref[...]
    ext_ref[halo + chunk:2 * halo + chunk] = jnp.where(c == n_chunks - 1, 0.0, xnext_ref[...])
    acc = cb_ref[...]
    for tap in range(SSD_CONV):
        o0 = halo - SSD_CONV // 2 + tap
        acc = acc + ext_ref[o0:o0 + chunk] * cw_ref[tap:tap + 1]
    xbc = _silu(acc)
    xs = xbc[:, :WIDTH_D]
    gs = SSD_GROUPS * SSD_STATE
    bm = xbc[:, WIDTH_D:WIDTH_D + gs].astype(BF16)
    cm = xbc[:, WIDTH_D + gs:].astype(BF16)

    dt = _softplus(dt_ref[...] + dtb_ref[...])
    a = dt * (-jnp.exp(alog_ref[...]))
    sums = _sel_dot(sel_ref[...], a)
    a_cum, a_rem = sums[:chunk], sums[chunk:]
    a_cum_t = a_cum.T
    expand = exp_ref[...]
    dt_x = _dot_sel(dt, expand)
    e_cum = jnp.exp(_dot_sel(a_cum, expand))
    e_rem = jnp.exp(_dot_sel(a_rem, expand))
    x_dt = xs * dt_x
    last = 0 if reverse else chunk - 1

    ii = lax.broadcasted_iota(jnp.int32, (chunk, chunk), 0)
    jj = lax.broadcasted_iota(jnp.int32, (chunk, chunk), 1)
    causal = (ii <= jj) if reverse else (ii >= jj)
    lane = lax.broadcasted_iota(jnp.int32, (1, LANES), 1)
    lo_half = lane < SSD_HEAD_DIM
    col0 = SSD_HEADS if reverse else 0
    rep = SSD_HEADS // SSD_GROUPS

    ys = []
    for p in range(SSD_HEADS // 2):
        g = (2 * p) // rep
        gcols = slice(g * SSD_STATE, (g + 1) * SSD_STATE)
        pcols = slice(p * LANES, (p + 1) * LANES)
        cmg, bmg = cm[:, gcols], bm[:, gcols]
        cb = _dot_nt(cmg, bmg)
        mats = []
        for hh in range(2):
            ch = col0 + 2 * p + hh
            diff = a_cum[:, ch:ch + 1] - a_cum_t[ch:ch + 1, :]
            seg = jnp.exp(jnp.where(causal, diff, NEG_INF))
            mats.append((cb * seg).astype(BF16))
        xp = x_dt[:, pcols]
        x_stack = jnp.concatenate([jnp.where(lo_half, xp, 0.0), jnp.where(lo_half, 0.0, xp)], axis=0).astype(BF16)
        y = _dot(jnp.concatenate(mats, axis=1), x_stack)
        st = st_ref[p]
        y = y + e_cum[:, pcols] * _dot(cmg, st.astype(BF16))
        upd = _dot_tn(bmg, (xp * e_rem[:, pcols]).astype(BF16))
        st_ref[p] = st * e_cum[last:last + 1, pcols] + upd
        if final:
            y = y + prev_ref[:, pcols] + dskip_ref[:, pcols] * xs[:, pcols]
            ys.append(y * _silu(dz_ref[:, pcols]))
        else:
            o_ref[:, pcols] = y
    if final:
        t = jnp.concatenate(ys, axis=1)
        ms = jnp.mean(t * t, axis=-1, keepdims=True)
        o_ref[...] = (t * lax.rsqrt(ms + RMS_EPS) * ng_ref[...]).astype(o_ref.dtype)


def _ssd(dxbc, dt_raw, conv_w, conv_b, dt_bias, a_log, prev, dz, d_skip, norm_g, *, reverse):
    bsz, seq, nch = dxbc.shape
    chunk = REC_CHUNK
    n = seq // chunk
    final = prev is not None
    sel, expand = _ssd_consts(chunk, reverse)
    hb = chunk // SUBLANES
    nb = seq // SUBLANES
    cix = (lambda i: n - 1 - i) if reverse else (lambda i: i)
    row = lambda width: pl.BlockSpec((None, chunk, width), lambda b, i: (b, cix(i), 0))
    const2 = lambda a: pl.BlockSpec(a.shape, lambda b, i: (0, 0))
    dtb = jnp.zeros((1, LANES), F32).at[0, :2 * SSD_HEADS].set(dt_bias.reshape(-1))
    alog = jnp.full((1, LANES), NEG_INF, F32).at[0, :2 * SSD_HEADS].set(a_log.reshape(-1))
    cb2 = conv_b.reshape(1, nch)
    in_specs = [row(nch),
                pl.BlockSpec((None, SUBLANES, nch), lambda b, i: (b, jnp.maximum(cix(i) * hb - 1, 0), 0)),
                pl.BlockSpec((None, SUBLANES, nch), lambda b, i: (b, jnp.minimum((cix(i) + 1) * hb, nb - 1), 0)),
                row(LANES), const2(conv_w), const2(cb2), const2(dtb), const2(alog), const2(sel), const2(expand)]
    args = [dxbc, dxbc, dxbc, dt_raw, conv_w, cb2, dtb, alog, sel, expand]
    if final:
        dsk = jnp.repeat(d_skip, SSD_HEAD_DIM).reshape(1, WIDTH_D)
        ng = norm_g.reshape(1, WIDTH_D)
        in_specs += [row(WIDTH_D), row(WIDTH_D), const2(dsk), const2(ng)]
        args += [prev, dz, dsk, ng]
    return pl.pallas_call(
        functools.partial(_ssd_kernel, final=final, reverse=reverse, n_chunks=n),
        grid=(bsz, n),
        in_specs=in_specs,
        out_specs=row(WIDTH_D),
        out_shape=jax.ShapeDtypeStruct((bsz, seq, WIDTH_D), BF16 if final else F32),
        scratch_shapes=[pltpu.VMEM((chunk + 2 * SUBLANES, nch), F32),
                        pltpu.VMEM((SSD_HEADS // 2, SSD_STATE, LANES), F32)],
        compiler_params=_cparams("parallel", "arbitrary"),
        name=f"ssd_{'bwd' if reverse else 'fwd'}",
    )(*args)


def kernel(x, c, ada_w, ada_b, ln_g, ln_b, e_w_in, e_rpb, e_gla_w_up, e_gla_b, e_gla_norm_g, e_w_out,
           o_w_in, hgrn_lb, o_hgrn_norm_g, o_conv_w, o_conv_b, o_dt_bias, o_a_log, o_d_skip,
           o_ssm_norm_g, o_w_out):
    mod = _ada_mod(c, ada_w, ada_b)

    aq, ak, av, ag, bq, bk, bv, bg, lf_f, lf_b = _proj_even(x, mod[0], e_w_in[0], e_gla_w_up[0], e_gla_b[0])
    ya = _neighbourhood_attention(aq, ak, av, ag, _na_bias(e_rpb[0]))
    gla = functools.partial(_gated_recurrence, "gla", heads=GLA_HEADS, dk=GLA_DK, dv=GLA_DV)
    o_fwd = gla(bq, bk, lf_f, bv, None, None, None, reverse=False)
    yb = gla(bq, bk, lf_b, bv, o_fwd, bg, e_gla_norm_g[0], reverse=True)
    x = _out_ln(ya, yb, e_w_out[0], x, mod[0], ln_g[0], ln_b[0])

    cq, cf_f, cf_b, ci, cg, dz, dxbc, dt_raw = _proj_odd(x, mod[1], o_w_in[0])
    hgrn = functools.partial(_gated_recurrence, "hgrn", heads=HGRN_HEADS, dk=HGRN_DIM, dv=HGRN_DIM)
    o_fwd = hgrn(cq, cf_f, hgrn_lb, ci, None, None, None, reverse=False)
    yc = hgrn(cq, cf_b, hgrn_lb, ci, o_fwd, cg, o_hgrn_norm_g[0], reverse=True)
    ssd = functools.partial(_ssd, dxbc, dt_raw, o_conv_w[0], o_conv_b[0], o_dt_bias[0], o_a_log[0])
    y_fwd = ssd(None, None, None, None, reverse=False)
    yd = ssd(y_fwd, dz, o_d_skip[0], o_ssm_norm_g[0], reverse=True)
    return _out_ln(yc, yd, o_w_out[0], x, mod[1], ln_g[1], ln_b[1])
```

```python
import functools
import math

import numpy as np
import jax
import jax.numpy as jnp
from jax import lax
from jax.experimental import pallas as pl
from jax.experimental.pallas import tpu as pltpu

F32 = jnp.float32
BF16 = jnp.bfloat16

GRID_W = 64
NA_HEADS, NA_HEAD_DIM = 8, 64
NA_WIN_ROWS, NA_WIN_COLS = 8, 16
WIDTH_A = NA_HEADS * NA_HEAD_DIM
GLA_HEADS, GLA_DK, GLA_DV, GLA_RANK = 4, 64, 128, 16
GLA_GATE_NORM = 16.0
WIDTH_B = GLA_HEADS * GLA_DV
HGRN_HEADS, HGRN_DIM = 4, 128
WIDTH_C = HGRN_HEADS * HGRN_DIM
SSD_HEADS, SSD_HEAD_DIM, SSD_GROUPS, SSD_STATE, SSD_CONV = 8, 64, 2, 128, 4
WIDTH_D = SSD_HEADS * SSD_HEAD_DIM
SSD_CONV_CH = WIDTH_D + 2 * SSD_GROUPS * SSD_STATE
DEPTH = 2
DEEPNORM_ALPHA = (2 * DEPTH) ** 0.25
LN_EPS = 1e-5
RMS_EPS = 1e-6

LANES = 128
SUBLANES = 8
VMEM_LIMIT = 56 * 1024 * 1024

ROW_TILE = 512
REC_CHUNK = 128
REC_CHUNKS_PER_STEP = 4
REC_SEL_LEVELS = (1, 2)
NA_ROWS_PER_STEP = 4
NEG_INF = float("-inf")


def _cparams(*sem):
    return pltpu.CompilerParams(dimension_semantics=sem, vmem_limit_bytes=VMEM_LIMIT)


def _dot(a, b):
    return jnp.dot(a, b, preferred_element_type=F32)


def _dot_nt(a, b):
    return lax.dot_general(a, b, (((1,), (1,)), ((), ())), preferred_element_type=F32)


def _dot_tn(a, b):
    return lax.dot_general(a, b, (((0,), (0,)), ((), ())), preferred_element_type=F32)


def _split3(x):
    hi = x.astype(BF16)
    r = x - hi.astype(F32)
    mid = r.astype(BF16)
    lo = (r - mid.astype(F32)).astype(BF16)
    return hi, mid, lo


def _sel_dot(sel, x):
    hi, mid, lo = _split3(x)
    return _dot(sel, hi) + _dot(sel, mid) + _dot(sel, lo)


def _dot_sel(x, sel):
    hi, mid, lo = _split3(x)
    return _dot(hi, sel) + _dot(mid, sel) + _dot(lo, sel)


def _sigmoid(x):
    return 1.0 / (1.0 + jnp.exp(-x))


def _silu(x):
    return x * _sigmoid(x)


def _log1pexp_neg_abs(x):
    return jnp.log1p(jnp.exp(-jnp.abs(x)))


def _log_sigmoid(x):
    return jnp.minimum(x, 0.0) - _log1pexp_neg_abs(x)


def _softplus(x):
    return jnp.maximum(x, 0.0) + _log1pexp_neg_abs(x)


def _ada_kernel(c_ref, w_ref, b_ref, o_ref):
    cond = _silu(c_ref[...]).astype(BF16)
    o_ref[...] = _dot(cond, w_ref[...].astype(BF16)) + b_ref[...]


def _ada_mod(c, ada_w, ada_b):
    bsz, d = c.shape
    bp = -(-bsz // SUBLANES) * SUBLANES
    n3 = ada_w.shape[-1]
    tn = ROW_TILE
    c_pad = jnp.zeros((bp, d), F32).at[:bsz].set(c)
    out = pl.pallas_call(
        _ada_kernel,
        grid=(DEPTH, n3 // tn),
        in_specs=[pl.BlockSpec((bp, d), lambda l, j: (0, 0)),
                  pl.BlockSpec((None, d, tn), lambda l, j: (l, 0, j)),
                  pl.BlockSpec((None, 1, tn), lambda l, j: (l, 0, j))],
        out_specs=pl.BlockSpec((None, bp, tn), lambda l, j: (l, 0, j)),
        out_shape=jax.ShapeDtypeStruct((DEPTH, bp, n3), F32),
        compiler_params=_cparams("arbitrary", "arbitrary"),
        name="ada_mod",
    )(c_pad, ada_w, ada_b.reshape(DEPTH, 1, n3))
    return out[:, :bsz].reshape(DEPTH, bsz, 1, n3)


def _modulated(x_ref, mod_ref):
    d = x_ref.shape[-1]
    mod = mod_ref[...]
    return (x_ref[...] * (1.0 + mod[:, d:2 * d]) + mod[:, :d]).astype(BF16)


def _proj_even_kernel(x_ref, mod_ref, w_ref, wup_ref, gb_ref,
                      aq_ref, ak_ref, av_ref, ag_ref, bq_ref, bk_ref, bv_ref, bg_ref, lff_ref, lfb_ref):
    h = _modulated(x_ref, mod_ref)

    def mm(lo, hi):
        return _dot(h, w_ref[:, lo:hi])

    a, kb = WIDTH_A, GLA_HEADS * GLA_DK
    aq_ref[...] = (mm(0, a) * (NA_HEAD_DIM ** -0.5)).astype(BF16)
    ak_ref[...] = mm(a, 2 * a).astype(BF16)
    av_ref[...] = mm(2 * a, 3 * a).astype(BF16)
    ag_ref[...] = mm(3 * a, 4 * a)
    o = 4 * a
    bq_ref[...] = (mm(o, o + kb) * (GLA_DK ** -0.5)).astype(BF16)
    bk_ref[...] = mm(o + kb, o + 2 * kb).astype(BF16)
    o += 2 * kb
    bv_ref[...] = mm(o, o + WIDTH_B).astype(BF16)
    bg_ref[...] = mm(o + WIDTH_B, o + 2 * WIDTH_B)
    o += 2 * WIDTH_B
    lr = mm(o, o + LANES)
    lr_hi = lr.astype(BF16)
    lr_lo = (lr - lr_hi.astype(F32)).astype(BF16)
    for d, out_ref in enumerate((lff_ref, lfb_ref)):
        wu = wup_ref[d]
        wu_hi = wu.astype(BF16)
        wu_lo = (wu - wu_hi.astype(F32)).astype(BF16)
        z = _dot(lr_hi, wu_hi) + _dot(lr_lo, wu_hi) + _dot(lr_hi, wu_lo) + gb_ref[d]
        out_ref[...] = _log_sigmoid(z) * (1.0 / GLA_GATE_NORM)


def _proj_even(x, mod, w_in, gla_w_up, gla_b):
    bsz, seq, d = x.shape
    kb = GLA_HEADS * GLA_DK
    n_main = 4 * WIDTH_A + 2 * kb + 2 * WIDTH_B
    w = jnp.zeros((d, n_main + LANES), BF16).at[:, :w_in.shape[1]].set(w_in.astype(BF16))
    wup = jnp.zeros((2, LANES, kb), F32)
    wup = wup.at[0, :GLA_RANK].set(gla_w_up[0]).at[1, GLA_RANK:2 * GLA_RANK].set(gla_w_up[1])
    tm = min(ROW_TILE, seq)
    row = lambda n: pl.BlockSpec((None, tm, n), lambda b, i: (b, i, 0))
    shp = lambda n, dt: jax.ShapeDtypeStruct((bsz, seq, n), dt)
    widths = [(WIDTH_A, BF16), (WIDTH_A, BF16), (WIDTH_A, BF16), (WIDTH_A, F32),
              (kb, BF16), (kb, BF16), (WIDTH_B, BF16), (WIDTH_B, F32), (kb, F32), (kb, F32)]
    return pl.pallas_call(
        _proj_even_kernel,
        grid=(bsz, seq // tm),
        in_specs=[row(d),
                  pl.BlockSpec((None, 1, mod.shape[-1]), lambda b, i: (b, 0, 0)),
                  pl.BlockSpec(w.shape, lambda b, i: (0, 0)),
                  pl.BlockSpec(wup.shape, lambda b, i: (0, 0, 0)),
                  pl.BlockSpec((2, 1, kb), lambda b, i: (0, 0, 0))],
        out_specs=[row(n) for n, _ in widths],
        out_shape=[shp(n, dt) for n, dt in widths],
        compiler_params=_cparams("parallel", "arbitrary"),
        name="proj_even",
    )(x, mod, w, wup, gla_b.reshape(2, 1, kb))


def _proj_odd_kernel(x_ref, mod_ref, w_ref,
                     cq_ref, cff_ref, cfb_ref, ci_ref, cg_ref, dz_ref, dxbc_ref, dt_ref):
    h = _modulated(x_ref, mod_ref)

    def mm(lo, hi):
        return _dot(h, w_ref[:, lo:hi])

    c = WIDTH_C
    cq_ref[...] = (mm(0, c) * (HGRN_DIM ** -0.5)).astype(BF16)
    cff_ref[...] = mm(c, 2 * c)
    cfb_ref[...] = mm(2 * c, 3 * c)
    ci_ref[...] = mm(3 * c, 4 * c).astype(BF16)
    cg_ref[...] = mm(4 * c, 5 * c)
    o = 5 * c
    dz_ref[...] = mm(o, o + WIDTH_D)
    o += WIDTH_D
    dxbc_ref[...] = mm(o, o + SSD_CONV_CH)
    o += SSD_CONV_CH
    dt_ref[...] = mm(o, o + LANES)


def _proj_odd(x, mod, w_in):
    bsz, seq, d = x.shape
    n_main = 5 * WIDTH_C + WIDTH_D + SSD_CONV_CH
    w = jnp.zeros((d, n_main + LANES), BF16).at[:, :w_in.shape[1]].set(w_in.astype(BF16))
    tm = min(ROW_TILE, seq)
    row = lambda n: pl.BlockSpec((None, tm, n), lambda b, i: (b, i, 0))
    shp = lambda n, dt: jax.ShapeDtypeStruct((bsz, seq, n), dt)
    widths = [(WIDTH_C, BF16), (WIDTH_C, F32), (WIDTH_C, F32), (WIDTH_C, BF16), (WIDTH_C, F32),
              (WIDTH_D, F32), (SSD_CONV_CH, F32), (LANES, F32)]
    return pl.pallas_call(
        _proj_odd_kernel,
        grid=(bsz, seq // tm),
        in_specs=[row(d),
                  pl.BlockSpec((None, 1, mod.shape[-1]), lambda b, i: (b, 0, 0)),
                  pl.BlockSpec(w.shape, lambda b, i: (0, 0))],
        out_specs=[row(n) for n, _ in widths],
        out_shape=[shp(n, dt) for n, dt in widths],
        compiler_params=_cparams("parallel", "arbitrary"),
        name="proj_odd",
    )(x, mod, w)


def _out_ln_kernel(ya_ref, yb_ref, w_ref, x_ref, mod_ref, g_ref, b_ref, o_ref):
    d = x_ref.shape[-1]
    wa = ya_ref.shape[-1]
    y = _dot(ya_ref[...], w_ref[:wa, :]) + _dot(yb_ref[...], w_ref[wa:, :])
    gate = mod_ref[...][:, 2 * d:]
    t = DEEPNORM_ALPHA * x_ref[...] + gate * y
    mu = jnp.mean(t, axis=-1, keepdims=True)
    tc = t - mu
    var = jnp.mean(tc * tc, axis=-1, keepdims=True)
    o_ref[...] = tc * lax.rsqrt(var + LN_EPS) * g_ref[...] + b_ref[...]


def _out_ln(ya, yb, w_out, x, mod, ln_g, ln_b):
    bsz, seq, d = x.shape
    tm = min(ROW_TILE, seq)
    w = w_out.astype(BF16)
    row = lambda n: pl.BlockSpec((None, tm, n), lambda b, i: (b, i, 0))
    vec = pl.BlockSpec((1, d), lambda b, i: (0, 0))
    return pl.pallas_call(
        _out_ln_kernel,
        grid=(bsz, seq // tm),
        in_specs=[row(ya.shape[-1]), row(yb.shape[-1]),
                  pl.BlockSpec(w.shape, lambda b, i: (0, 0)),
                  row(d),
                  pl.BlockSpec((None, 1, mod.shape[-1]), lambda b, i: (b, 0, 0)),
                  vec, vec],
        out_specs=row(d),
        out_shape=jax.ShapeDtypeStruct((bsz, seq, d), F32),
        compiler_params=_cparams("parallel", "arbitrary"),
        name="out_ln",
    )(ya, yb, w, x, mod, ln_g.reshape(1, d), ln_b.reshape(1, d))


NA_BIAS_TYPES = NA_WIN_ROWS
NA_KEYS = NA_WIN_ROWS * GRID_W


def _na_bias_kernel(rpb_ref, o_ref):
    t = pl.program_id(0)
    shape = (GRID_W, LANES)
    q = lax.broadcasted_iota(jnp.int32, shape, 0)
    lane = lax.broadcasted_iota(jnp.int32, shape, 1)
    kc = lane & (GRID_W - 1)
    dc = kc - q + (NA_WIN_COLS - 1)
    cs = jnp.clip(q - NA_WIN_COLS // 2, 0, GRID_W - NA_WIN_COLS)
    valid = (kc >= cs) & (kc < cs + NA_WIN_COLS)
    upper = lane >= GRID_W
    n_dr, n_dc = 2 * NA_WIN_ROWS - 1, 2 * NA_WIN_COLS - 1
    for h in range(NA_HEADS):
        for c in range(NA_KEYS // LANES):
            base0 = (h * n_dr + (2 * c + NA_WIN_ROWS - 1 - t)) * n_dc
            base1 = base0 + n_dc

            def body(d, acc, base0=base0, base1=base1):
                val = jnp.where(upper, rpb_ref[base1 + d], rpb_ref[base0 + d])
                return jnp.where(dc == d, val, acc)

            acc = lax.fori_loop(0, n_dc, body, jnp.zeros(shape, F32))
            o_ref[h, :, c * LANES:(c + 1) * LANES] = jnp.where(valid, acc, NEG_INF)


def _na_bias(rpb):
    return pl.pallas_call(
        _na_bias_kernel,
        grid=(NA_BIAS_TYPES,),
        in_specs=[pl.BlockSpec(memory_space=pltpu.SMEM)],
        out_specs=pl.BlockSpec((None, NA_HEADS, GRID_W, NA_KEYS), lambda t: (t, 0, 0, 0)),
        out_shape=jax.ShapeDtypeStruct((NA_BIAS_TYPES, NA_HEADS, GRID_W, NA_KEYS), F32),
        compiler_params=_cparams("arbitrary"),
        name="na_bias",
    )(rpb.reshape(-1))


def _na_kernel(q_ref, k_ref, v_ref, g_ref, bias_ref, o_ref, *, n_rows):
    i = pl.program_id(1)
    lane = lax.broadcasted_iota(jnp.int32, (1, LANES), 1)
    head_mask = [(lane < NA_HEAD_DIM).astype(BF16), (lane >= NA_HEAD_DIM).astype(BF16)]
    half = NA_WIN_ROWS // 2
    for rr in range(NA_ROWS_PER_STEP):
        r = i * NA_ROWS_PER_STEP + rr
        row_start = jnp.clip(r - half, 0, n_rows - NA_WIN_ROWS)
        t = jnp.where(r < half, r, jnp.where(r > n_rows - half, r - (n_rows - NA_WIN_ROWS), half))
        koff = pl.multiple_of(row_start * GRID_W, GRID_W)
        rows = slice(rr * GRID_W, (rr + 1) * GRID_W)
        n_pairs = WIDTH_A // LANES
        pair_cols = [slice(p * LANES, (p + 1) * LANES) for p in range(n_pairs)]
        scores = []
        for p, cols in enumerate(pair_cols):
            qp = q_ref[rows, cols]
            qs = jnp.concatenate([qp * head_mask[0], qp * head_mask[1]], axis=0)
            kp = k_ref[pl.ds(koff, NA_KEYS), cols]
            scores.append(_dot_nt(qs, kp) + bias_ref[t, p])
        probs, norms = [], []
        for s in scores:
            e = jnp.exp(s - jnp.max(s, axis=-1, keepdims=True))
            norms.append(jnp.sum(e, axis=-1, keepdims=True))
            probs.append(e.astype(BF16))
        for p, cols in enumerate(pair_cols):
            vp = v_ref[pl.ds(koff, NA_KEYS), cols]
            o2 = _dot(probs[p], vp) / norms[p]
            o = jnp.where(lane < NA_HEAD_DIM, o2[:GRID_W], o2[GRID_W:])
            o_ref[rows, cols] = (o * _silu(g_ref[rows, cols])).astype(BF16)


def _neighbourhood_attention(aq, ak, av, ag, bias):
    bsz, seq, w = aq.shape
    n_rows = seq // GRID_W
    tq = NA_ROWS_PER_STEP * GRID_W
    bias = bias.reshape(NA_BIAS_TYPES, NA_HEADS // 2, 2 * GRID_W, NA_KEYS)
    row = pl.BlockSpec((None, tq, w), lambda b, i: (b, i, 0))
    full = pl.BlockSpec((None, seq, w), lambda b, i: (b, 0, 0))
    return pl.pallas_call(
        functools.partial(_na_kernel, n_rows=n_rows),
        grid=(bsz, n_rows // NA_ROWS_PER_STEP),
        in_specs=[row, full, full, row,
                  pl.BlockSpec(bias.shape, lambda b, i: (0, 0, 0, 0))],
        out_specs=row,
        out_shape=jax.ShapeDtypeStruct((bsz, seq, w), BF16),
        compiler_params=_cparams("parallel", "arbitrary"),
        name="na_attn",
    )(aq, ak, av, ag, bias)


def _rec_consts(chunk, reverse):
    nlev = int(math.log2(chunk))
    assert 1 << nlev == chunk
    w = np.zeros(((nlev + 2), chunk, chunk), np.float32)
    lv = np.full((chunk, chunk), -1, np.int32)
    idx = np.arange(chunk)
    for l in range(nlev):
        s = 1 << l
        off = idx % (2 * s)
        mid = idx - off + s - 1
        for p in range(chunk):
            if off[p] >= s:
                w[l, p, mid[p] + 1:p + 1] = 1.0
            else:
                w[l, p, p + 1:mid[p] + 1] = 1.0
        same = (idx[:, None] // (2 * s)) == (idx[None, :] // (2 * s))
        lv[same & (off[:, None] >= s) & (off[None, :] < s)] = l
    lv[idx, idx] = nlev
    w[nlev] = (idx[None, :] <= idx[:, None])
    if reverse:
        w = w[:, ::-1, ::-1]
        lv = lv[::-1, ::-1]
    w = np.concatenate([w[nlev]] + [w[l] for l in REC_SEL_LEVELS], axis=0)
    return (jnp.asarray(w, BF16), jnp.asarray(np.ascontiguousarray(lv)), nlev)


def _rec_kernel(*refs, mode, final, reverse, heads, dk, dv, nlev):
    chunk = REC_CHUNK
    if mode == "gla":
        q_ref, k_ref, lf_ref, v_ref, w_ref, lv_ref = refs[:6]
        rest = refs[6:]
    else:
        q_ref, z_ref, lbraw_ref, v_ref, w_ref, lv_ref = refs[:6]
        rest = refs[6:]
    if final:
        prev_ref, gate_ref, ng_ref, o_ref, st_ref = rest
    else:
        o_ref, st_ref = rest

    @pl.when(pl.program_id(1) == 0)
    def _():
        st_ref[...] = jnp.zeros_like(st_ref)

    lane = lax.broadcasted_iota(jnp.int32, (1, LANES), 1)
    row = lax.broadcasted_iota(jnp.int32, (chunk, LANES), 0)
    per_slab = LANES // dk
    n_slabs = heads * dk // LANES
    lv = lv_ref[...]
    level_masks = [lv == l for l in range(nlev + 1)]
    last = 0 if reverse else chunk - 1

    def is_query_side(l):
        off = row & (2 * (1 << l) - 1)
        return (off < (1 << l)) if reverse else (off >= (1 << l))

    def boundary_rows(b, l):
        s = 1 << l
        parts = []
        for blk in range(chunk // (2 * s)):
            mid = blk * 2 * s + (s if reverse else s - 1)
            parts.append(jnp.broadcast_to(b[mid:mid + 1, :], (2 * s, LANES)))
        return parts[0] if len(parts) == 1 else jnp.concatenate(parts, axis=0)

    def query_or_key(q, k, l):
        if (1 << l) < SUBLANES:
            return jnp.where(is_query_side(l), q, k)
        s = 1 << l
        parts = []
        for blk in range(chunk // (2 * s)):
            lo, hi = slice(blk * 2 * s, blk * 2 * s + s), slice(blk * 2 * s + s, (blk + 1) * 2 * s)
            parts += [q[lo], k[hi]] if reverse else [k[lo], q[hi]]
        return jnp.concatenate(parts, axis=0)

    if mode == "hgrn":
        lbr = lbraw_ref[...]
        mx = jnp.maximum(lbr[0:1], lbr[1:2])
        e0, e1 = jnp.exp(lbr[0:1] - mx), jnp.exp(lbr[1:2] - mx)
        lb = e1 / (e0 + e1)
        log_lb, log_ub = jnp.log(lb), jnp.log1p(-lb)

    head_masks = [((lane >= hh * dk) & (lane < (hh + 1) * dk)) for hh in range(per_slab)]
    head_masks_bf = [hm.astype(BF16) for hm in head_masks]

    def only_head(t, hh):
        return t if per_slab == 1 else t * head_masks_bf[hh]

    def stage_scale(ci):
        rows = slice(ci * chunk, (ci + 1) * chunk)
        if mode == "gla":
            lf_all = lf_ref[rows, :]
            k_all = None
        else:
            z = z_ref[rows, :]
            k_all = (1.0 - lb) * _sigmoid(-z)
            lc = log_ub + _log_sigmoid(z)
            lf_all = jnp.maximum(log_lb, lc) + _log1pexp_neg_abs(log_lb - lc)
        sums = _sel_dot(w_ref[...], lf_all)
        slabs = []
        for s in range(n_slabs):
            cols = slice(s * LANES, (s + 1) * LANES)
            q = q_ref[rows, cols].astype(F32)
            k = k_ref[rows, cols].astype(F32) if mode == "gla" else k_all[:, cols]
            lf = lf_all[:, cols]
            b = sums[:chunk, cols]
            ys = []
            for l in range(nlev):
                if l == 0:
                    e = jnp.where(is_query_side(0), lf, 0.0)
                elif l in REC_SEL_LEVELS:
                    i0 = (1 + REC_SEL_LEVELS.index(l)) * chunk
                    e = sums[i0:i0 + chunk, cols]
                else:
                    e = -jnp.abs(b - boundary_rows(b, l))
                ys.append((jnp.exp(e) * query_or_key(q, k, l)).astype(BF16))
            eb = jnp.exp(b)
            b_end = jnp.broadcast_to(b[last:last + 1, :], (chunk, LANES))
            slabs.append(dict(
                lhs=ys + [q.astype(BF16)], rhs=ys + [k.astype(BF16)],
                q_in=(eb * q).astype(BF16),
                k_up=(jnp.exp(-jnp.abs(b_end - b)) * k).astype(BF16),
                decay=eb[last:last + 1]))
        return rows, slabs

    def stage_intra(slabs):
        mats = []
        for s in range(n_slabs):
            for hh in range(per_slab):
                a_mat = jnp.zeros((chunk, chunk), F32)
                for l in range(nlev + 1):
                    p_l = _dot_nt(slabs[s]["lhs"][l], only_head(slabs[s]["rhs"][l], hh))
                    a_mat = jnp.where(level_masks[l], p_l, a_mat)
                mats.append(a_mat.astype(BF16))
        return mats

    def stage_out(rows, slabs, mats):
        for s in range(n_slabs):
            st = st_ref[s]
            st_bf = st.astype(BF16)
            upd = None
            for hh in range(per_slab):
                head = s * per_slab + hh
                vh = v_ref[rows, head * dv:(head + 1) * dv]
                o = _dot(mats[head], vh) + _dot_nt(only_head(slabs[s]["q_in"], hh), st_bf)
                u = _dot_tn(vh, slabs[s]["k_up"])
                if per_slab > 1:
                    u = jnp.where(head_masks[hh], u, 0.0)
                upd = u if upd is None else upd + u
                ocols = slice(head * dv, (head + 1) * dv)
                if final:
                    tot = prev_ref[rows, ocols] + o
                    ms = jnp.mean(tot * tot, axis=-1, keepdims=True)
                    y = tot * lax.rsqrt(ms + RMS_EPS) * ng_ref[...]
                    o_ref[rows, ocols] = (y * _silu(gate_ref[rows, ocols])).astype(o_ref.dtype)
                else:
                    o_ref[rows, ocols] = o
            st_ref[s] = st * slabs[s]["decay"] + upd

    n_sub = q_ref.shape[0] // chunk
    order = list(range(n_sub - 1, -1, -1) if reverse else range(n_sub))
    scaled, intra = {}, {}
    for t in range(n_sub + 2):
        if t < n_sub:
            scaled[t] = stage_scale(order[t])
        if 0 <= t - 1 < n_sub:
            intra[t - 1] = stage_intra(scaled[t - 1][1])
        if 0 <= t - 2 < n_sub:
            stage_out(*scaled.pop(t - 2), intra.pop(t - 2))


def _gated_recurrence(mode, q, kz, lf_or_lb, v, prev, gate, norm_g, *, reverse, heads, dk, dv):
    bsz, seq, _ = q.shape
    chunk = REC_CHUNK
    blk_rows = min(REC_CHUNKS_PER_STEP * chunk, seq)
    n = seq // blk_rows
    final = prev is not None
    w_sel, lv, nlev = _rec_consts(chunk, reverse)
    cidx = (lambda b, i: (b, n - 1 - i, 0)) if reverse else (lambda b, i: (b, i, 0))
    row = lambda width: pl.BlockSpec((None, blk_rows, width), cidx)
    const2 = lambda a: pl.BlockSpec(a.shape, lambda b, i: (0, 0))
    in_specs = [row(heads * dk), row(heads * dk),
                row(heads * dk) if mode == "gla" else const2(lf_or_lb),
                row(heads * dv), const2(w_sel), const2(lv)]
    args = [q, kz, lf_or_lb, v, w_sel, lv]
    if final:
        in_specs += [row(heads * dv), row(heads * dv), pl.BlockSpec((1, dv), lambda b, i: (0, 0))]
        args += [prev, gate, norm_g.reshape(1, dv)]
    return pl.pallas_call(
        functools.partial(_rec_kernel, mode=mode, final=final, reverse=reverse,
                          heads=heads, dk=dk, dv=dv, nlev=nlev),
        grid=(bsz, n),
        in_specs=in_specs,
        out_specs=row(heads * dv),
        out_shape=jax.ShapeDtypeStruct((bsz, seq, heads * dv), BF16 if final else F32),
        scratch_shapes=[pltpu.VMEM((heads * dk // LANES, dv, LANES), F32)],
        compiler_params=_cparams("parallel", "arbitrary"),
        name=f"{mode}_{'bwd' if reverse else 'fwd'}",
    )(*args)


def _ssd_consts(chunk, reverse):
    idx = np.arange(chunk)
    if reverse:
        tri = idx[None, :] >= idx[:, None]
        rest = idx[None, :] < idx[:, None]
    else:
        tri = idx[None, :] <= idx[:, None]
        rest = idx[None, :] > idx[:, None]
    sel = np.concatenate([tri, rest], axis=0).astype(np.float32)
    expand = np.zeros((LANES, WIDTH_D), np.float32)
    for h in range(SSD_HEADS):
        expand[h + (SSD_HEADS if reverse else 0), h * SSD_HEAD_DIM:(h + 1) * SSD_HEAD_DIM] = 1.0
    return jnp.asarray(sel, BF16), jnp.asarray(expand, BF16)


def _ssd_kernel(*refs, final, reverse, n_chunks):
    chunk = REC_CHUNK
    (x_ref, xprev_ref, xnext_ref, dt_ref, cw_ref, cb_ref, dtb_ref, alog_ref, sel_ref, exp_ref) = refs[:10]
    if final:
        prev_ref, dz_ref, dskip_ref, ng_ref, o_ref, ext_ref, st_ref = refs[10:]
    else:
        o_ref, ext_ref, st_ref = refs[10:]
    step = pl.program_id(1)
    c = (n_chunks - 1 - step) if reverse else step

    @pl.when(step == 0)
    def _():
        st_ref[...] = jnp.zeros_like(st_ref)

    halo = SUBLANES
    ext_ref[0:halo] = jnp.where(c == 0, 0.0, xprev_ref[...])
    ext_ref[halo:halo + chunk] = x_ref[...]
    ext_ref[halo + chunk:2 * halo + chunk] = jnp.where(c == n_chunks - 1, 0.0, xnext_ref[...])
    acc = cb_ref[...]
    for tap in range(SSD_CONV):
        o0 = halo - SSD_CONV // 2 + tap
        acc = acc + ext_ref[o0:o0 + chunk] * cw_ref[tap:tap + 1]
    xbc = _silu(acc)
    xs = xbc[:, :WIDTH_D]
    gs = SSD_GROUPS * SSD_STATE
    bm = xbc[:, WIDTH_D:WIDTH_D + gs].astype(BF16)
    cm = xbc[:, WIDTH_D + gs:].astype(BF16)

    dt = _softplus(dt_ref[...] + dtb_ref[...])
    a = dt * (-jnp.exp(alog_ref[...]))
    sums = _sel_dot(sel_ref[...], a)
    a_cum, a_rem = sums[:chunk], sums[chunk:]
    a_cum_t = a_cum.T
    expand = exp_ref[...]
    dt_x = _dot_sel(dt, expand)
    e_cum = jnp.exp(_dot_sel(a_cum, expand))
    e_rem = jnp.exp(_dot_sel(a_rem, expand))
    x_dt = xs * dt_x
    last = 0 if reverse else chunk - 1

    ii = lax.broadcasted_iota(jnp.int32, (chunk, chunk), 0)
    jj = lax.broadcasted_iota(jnp.int32, (chunk, chunk), 1)
    causal = (ii <= jj) if reverse else (ii >= jj)
    lane = lax.broadcasted_iota(jnp.int32, (1, LANES), 1)
    lo_half = lane < SSD_HEAD_DIM
    col0 = SSD_HEADS if reverse else 0
    rep = SSD_HEADS // SSD_GROUPS

    ys = []
    for p in range(SSD_HEADS // 2):
        g = (2 * p) // rep
        gcols = slice(g * SSD_STATE, (g + 1) * SSD_STATE)
        pcols = slice(p * LANES, (p + 1) * LANES)
        cmg, bmg = cm[:, gcols], bm[:, gcols]
        cb = _dot_nt(cmg, bmg)
        mats = []
        for hh in range(2):
            ch = col0 + 2 * p + hh
            diff = a_cum[:, ch:ch + 1] - a_cum_t[ch:ch + 1, :]
            seg = jnp.exp(jnp.where(causal, diff, NEG_INF))
            mats.append((cb * seg).astype(BF16))
        xp = x_dt[:, pcols]
        x_stack = jnp.concatenate([jnp.where(lo_half, xp, 0.0), jnp.where(lo_half, 0.0, xp)], axis=0).astype(BF16)
        y = _dot(jnp.concatenate(mats, axis=1), x_stack)
        st = st_ref[p]
        y = y + e_cum[:, pcols] * _dot(cmg, st.astype(BF16))
        upd = _dot_tn(bmg, (xp * e_rem[:, pcols]).astype(BF16))
        st_ref[p] = st * e_cum[last:last + 1, pcols] + upd
        if final:
            y = y + prev_ref[:, pcols] + dskip_ref[:, pcols] * xs[:, pcols]
            ys.append(y * _silu(dz_ref[:, pcols]))
        else:
            o_ref[:, pcols] = y
    if final:
        t = jnp.concatenate(ys, axis=1)
        ms = jnp.mean(t * t, axis=-1, keepdims=True)
        o_ref[...] = (t * lax.rsqrt(ms + RMS_EPS) * ng_ref[...]).astype(o_ref.dtype)


def _ssd(dxbc, dt_raw, conv_w, conv_b, dt_bias, a_log, prev, dz, d_skip, norm_g, *, reverse):
    bsz, seq, nch = dxbc.shape
    chunk = REC_CHUNK
    n = seq // chunk
    final = prev is not None
    sel, expand = _ssd_consts(chunk, reverse)
    hb = chunk // SUBLANES
    nb = seq // SUBLANES
    cix = (lambda i: n - 1 - i) if reverse else (lambda i: i)
    row = lambda width: pl.BlockSpec((None, chunk, width), lambda b, i: (b, cix(i), 0))
    const2 = lambda a: pl.BlockSpec(a.shape, lambda b, i: (0, 0))
    dtb = jnp.zeros((1, LANES), F32).at[0, :2 * SSD_HEADS].set(dt_bias.reshape(-1))
    alog = jnp.full((1, LANES), NEG_INF, F32).at[0, :2 * SSD_HEADS].set(a_log.reshape(-1))
    cb2 = conv_b.reshape(1, nch)
    in_specs = [row(nch),
                pl.BlockSpec((None, SUBLANES, nch), lambda b, i: (b, jnp.maximum(cix(i) * hb - 1, 0), 0)),
                pl.BlockSpec((None, SUBLANES, nch), lambda b, i: (b, jnp.minimum((cix(i) + 1) * hb, nb - 1), 0)),
                row(LANES), const2(conv_w), const2(cb2), const2(dtb), const2(alog), const2(sel), const2(expand)]
    args = [dxbc, dxbc, dxbc, dt_raw, conv_w, cb2, dtb, alog, sel, expand]
    if final:
        dsk = jnp.repeat(d_skip, SSD_HEAD_DIM).reshape(1, WIDTH_D)
        ng = norm_g.reshape(1, WIDTH_D)
        in_specs += [row(WIDTH_D), row(WIDTH_D), const2(dsk), const2(ng)]
        args += [prev, dz, dsk, ng]
    return pl.pallas_call(
        functools.partial(_ssd_kernel, final=final, reverse=reverse, n_chunks=n),
        grid=(bsz, n),
        in_specs=in_specs,
        out_specs=row(WIDTH_D),
        out_shape=jax.ShapeDtypeStruct((bsz, seq, WIDTH_D), BF16 if final else F32),
        scratch_shapes=[pltpu.VMEM((chunk + 2 * SUBLANES, nch), F32),
                        pltpu.VMEM((SSD_HEADS // 2, SSD_STATE, LANES), F32)],
        compiler_params=_cparams("parallel", "arbitrary"),
        name=f"ssd_{'bwd' if reverse else 'fwd'}",
    )(*args)


def kernel(x, c, ada_w, ada_b, ln_g, ln_b, e_w_in, e_rpb, e_gla_w_up, e_gla_b, e_gla_norm_g, e_w_out,
           o_w_in, hgrn_lb, o_hgrn_norm_g, o_conv_w, o_conv_b, o_dt_bias, o_a_log, o_d_skip,
           o_ssm_norm_g, o_w_out):
    mod = _ada_mod(c, ada_w, ada_b)

    aq, ak, av, ag, bq, bk, bv, bg, lf_f, lf_b = _proj_even(x, mod[0], e_w_in[0], e_gla_w_up[0], e_gla_b[0])
    ya = _neighbourhood_attention(aq, ak, av, ag, _na_bias(e_rpb[0]))
    gla = functools.partial(_gated_recurrence, "gla", heads=GLA_HEADS, dk=GLA_DK, dv=GLA_DV)
    o_fwd = gla(bq, bk, lf_f, bv, None, None, None, reverse=False)
    yb = gla(bq, bk, lf_b, bv, o_fwd, bg, e_gla_norm_g[0], reverse=True)
    x = _out_ln(ya, yb, e_w_out[0], x, mod[0], ln_g[0], ln_b[0])

    cq, cf_f, cf_b, ci, cg, dz, dxbc, dt_raw = _proj_odd(x, mod[1], o_w_in[0])
    hgrn = functools.partial(_gated_recurrence, "hgrn", heads=HGRN_HEADS, dk=HGRN_DIM, dv=HGRN_DIM)
    o_fwd = hgrn(cq, cf_f, hgrn_lb, ci, None, None, None, reverse=False)
    yc = hgrn(cq, cf_b, hgrn_lb, ci, o_fwd, cg, o_hgrn_norm_g[0], reverse=True)
    ssd = functools.partial(_ssd, dxbc, dt_raw, o_conv_w[0], o_conv_b[0], o_dt_bias[0], o_a_log[0])
    y_fwd = ssd(None, None, None, None, reverse=False)
    yd = ssd(y_fwd, dz, o_d_skip[0], o_ssm_norm_g[0], reverse=True)
    return _out_ln(yc, yd, o_w_out[0], x, mod[1], ln_g[1], ln_b[1])
```

```python
import functools
import math

import numpy as np
import jax
import jax.numpy as jnp
from jax import lax
from jax.experimental import pallas as pl
from jax.experimental.pallas import tpu as pltpu

F32 = jnp.float32
BF16 = jnp.bfloat16

GRID_W = 64
NA_HEADS, NA_HEAD_DIM = 8, 64
NA_WIN_ROWS, NA_WIN_COLS = 8, 16
WIDTH_A = NA_HEADS * NA_HEAD_DIM
GLA_HEADS, GLA_DK, GLA_DV, GLA_RANK = 4, 64, 128, 16
GLA_GATE_NORM = 16.0
WIDTH_B = GLA_HEADS * GLA_DV
HGRN_HEADS, HGRN_DIM = 4, 128
WIDTH_C = HGRN_HEADS * HGRN_DIM
SSD_HEADS, SSD_HEAD_DIM, SSD_GROUPS, SSD_STATE, SSD_CONV = 8, 64, 2, 128, 4
WIDTH_D = SSD_HEADS * SSD_HEAD_DIM
SSD_CONV_CH = WIDTH_D + 2 * SSD_GROUPS * SSD_STATE
DEPTH = 2
DEEPNORM_ALPHA = (2 * DEPTH) ** 0.25
LN_EPS = 1e-5
RMS_EPS = 1e-6

LANES = 128
SUBLANES = 8
VMEM_LIMIT = 56 * 1024 * 1024

ROW_TILE = 512
REC_CHUNK = 128
REC_CHUNKS_PER_STEP = 4
REC_SEL_LEVELS = (1, 2)
NA_ROWS_PER_STEP = 4
NEG_INF = float("-inf")
LOG2_E = math.log2(math.e)


def _cparams(*sem):
    return pltpu.CompilerParams(dimension_semantics=sem, vmem_limit_bytes=VMEM_LIMIT)


def _dot(a, b):
    return jnp.dot(a, b, preferred_element_type=F32)


def _dot_nt(a, b):
    return lax.dot_general(a, b, (((1,), (1,)), ((), ())), preferred_element_type=F32)


def _dot_tn(a, b):
    return lax.dot_general(a, b, (((0,), (0,)), ((), ())), preferred_element_type=F32)


def _split3(x):
    hi = x.astype(BF16)
    r = x - hi.astype(F32)
    mid = r.astype(BF16)
    lo = (r - mid.astype(F32)).astype(BF16)
    return hi, mid, lo


def _sel_dot(sel, x):
    hi, mid, lo = _split3(x)
    return _dot(sel, hi) + _dot(sel, mid) + _dot(sel, lo)


def _dot_sel(x, sel):
    hi, mid, lo = _split3(x)
    return _dot(hi, sel) + _dot(mid, sel) + _dot(lo, sel)


def _sigmoid(x):
    return 1.0 / (1.0 + jnp.exp(-x))


def _silu(x):
    return x * _sigmoid(x)


def _log1pexp_neg_abs(x):
    return jnp.log1p(jnp.exp(-jnp.abs(x)))


def _log_1p_exp_neg_abs(x):
    return jnp.log(1.0 + jnp.exp(-jnp.abs(x)))


def _log_sigmoid(x):
    return jnp.minimum(x, 0.0) - _log_1p_exp_neg_abs(x)


def _softplus(x):
    return jnp.maximum(x, 0.0) + _log1pexp_neg_abs(x)


def _ada_kernel(c_ref, w_ref, b_ref, o_ref):
    cond = _silu(c_ref[...]).astype(BF16)
    o_ref[...] = _dot(cond, w_ref[...].astype(BF16)) + b_ref[...]


def _ada_mod(c, ada_w, ada_b):
    bsz, d = c.shape
    bp = -(-bsz // SUBLANES) * SUBLANES
    n3 = ada_w.shape[-1]
    tn = ROW_TILE
    c_pad = jnp.zeros((bp, d), F32).at[:bsz].set(c)
    out = pl.pallas_call(
        _ada_kernel,
        grid=(DEPTH, n3 // tn),
        in_specs=[pl.BlockSpec((bp, d), lambda l, j: (0, 0)),
                  pl.BlockSpec((None, d, tn), lambda l, j: (l, 0, j)),
                  pl.BlockSpec((None, 1, tn), lambda l, j: (l, 0, j))],
        out_specs=pl.BlockSpec((None, bp, tn), lambda l, j: (l, 0, j)),
        out_shape=jax.ShapeDtypeStruct((DEPTH, bp, n3), F32),
        compiler_params=_cparams("arbitrary", "arbitrary"),
        name="ada_mod",
    )(c_pad, ada_w, ada_b.reshape(DEPTH, 1, n3))
    return out[:, :bsz].reshape(DEPTH, bsz, 1, n3)


def _modulated(x_ref, mod_ref):
    d = x_ref.shape[-1]
    mod = mod_ref[...]
    return (x_ref[...] * (1.0 + mod[:, d:2 * d]) + mod[:, :d]).astype(BF16)


def _proj_even_kernel(x_ref, mod_ref, w_ref, wup_ref, gb_ref,
                      aq_ref, ak_ref, av_ref, ag_ref, bq_ref, bk_ref, bv_ref, bg_ref, lff_ref, lfb_ref):
    h = _modulated(x_ref, mod_ref)

    def mm(lo, hi):
        return _dot(h, w_ref[:, lo:hi])

    a, kb = WIDTH_A, GLA_HEADS * GLA_DK
    aq_ref[...] = (mm(0, a) * (NA_HEAD_DIM ** -0.5)).astype(BF16)
    ak_ref[...] = mm(a, 2 * a).astype(BF16)
    av_ref[...] = mm(2 * a, 3 * a).astype(BF16)
    ag_ref[...] = mm(3 * a, 4 * a)
    o = 4 * a
    bq_ref[...] = (mm(o, o + kb) * (GLA_DK ** -0.5)).astype(BF16)
    bk_ref[...] = mm(o + kb, o + 2 * kb).astype(BF16)
    o += 2 * kb
    bv_ref[...] = mm(o, o + WIDTH_B).astype(BF16)
    bg_ref[...] = mm(o + WIDTH_B, o + 2 * WIDTH_B)
    o += 2 * WIDTH_B
    lr = mm(o, o + LANES)
    lr_hi = lr.astype(BF16)
    lr_lo = (lr - lr_hi.astype(F32)).astype(BF16)
    for d, out_ref in enumerate((lff_ref, lfb_ref)):
        wu = wup_ref[d]
        wu_hi = wu.astype(BF16)
        wu_lo = (wu - wu_hi.astype(F32)).astype(BF16)
        z = _dot(lr_hi, wu_hi) + _dot(lr_lo, wu_hi) + _dot(lr_hi, wu_lo) + gb_ref[d]
        out_ref[...] = _log_sigmoid(z) * (1.0 / GLA_GATE_NORM)


def _proj_even(x, mod, w_in, gla_w_up, gla_b):
    bsz, seq, d = x.shape
    kb = GLA_HEADS * GLA_DK
    n_main = 4 * WIDTH_A + 2 * kb + 2 * WIDTH_B
    w = jnp.zeros((d, n_main + LANES), BF16).at[:, :w_in.shape[1]].set(w_in.astype(BF16))
    wup = jnp.zeros((2, LANES, kb), F32)
    wup = wup.at[0, :GLA_RANK].set(gla_w_up[0]).at[1, GLA_RANK:2 * GLA_RANK].set(gla_w_up[1])
    tm = min(ROW_TILE, seq)
    row = lambda n: pl.BlockSpec((None, tm, n), lambda b, i: (b, i, 0))
    shp = lambda n, dt: jax.ShapeDtypeStruct((bsz, seq, n), dt)
    widths = [(WIDTH_A, BF16), (WIDTH_A, BF16), (WIDTH_A, BF16), (WIDTH_A, F32),
              (kb, BF16), (kb, BF16), (WIDTH_B, BF16), (WIDTH_B, F32), (kb, F32), (kb, F32)]
    return pl.pallas_call(
        _proj_even_kernel,
        grid=(bsz, seq // tm),
        in_specs=[row(d),
                  pl.BlockSpec((None, 1, mod.shape[-1]), lambda b, i: (b, 0, 0)),
                  pl.BlockSpec(w.shape, lambda b, i: (0, 0)),
                  pl.BlockSpec(wup.shape, lambda b, i: (0, 0, 0)),
                  pl.BlockSpec((2, 1, kb), lambda b, i: (0, 0, 0))],
        out_specs=[row(n) for n, _ in widths],
        out_shape=[shp(n, dt) for n, dt in widths],
        compiler_params=_cparams("parallel", "arbitrary"),
        name="proj_even",
    )(x, mod, w, wup, gla_b.reshape(2, 1, kb))


def _proj_odd_kernel(x_ref, mod_ref, w_ref,
                     cq_ref, cff_ref, cfb_ref, ci_ref, cg_ref, dz_ref, dxbc_ref, dt_ref):
    h = _modulated(x_ref, mod_ref)

    def mm(lo, hi):
        return _dot(h, w_ref[:, lo:hi])

    c = WIDTH_C
    cq_ref[...] = (mm(0, c) * (HGRN_DIM ** -0.5)).astype(BF16)
    cff_ref[...] = mm(c, 2 * c)
    cfb_ref[...] = mm(2 * c, 3 * c)
    ci_ref[...] = mm(3 * c, 4 * c).astype(BF16)
    cg_ref[...] = mm(4 * c, 5 * c)
    o = 5 * c
    dz_ref[...] = mm(o, o + WIDTH_D)
    o += WIDTH_D
    dxbc_ref[...] = mm(o, o + SSD_CONV_CH)
    o += SSD_CONV_CH
    dt_ref[...] = mm(o, o + LANES)


def _proj_odd(x, mod, w_in):
    bsz, seq, d = x.shape
    n_main = 5 * WIDTH_C + WIDTH_D + SSD_CONV_CH
    w = jnp.zeros((d, n_main + LANES), BF16).at[:, :w_in.shape[1]].set(w_in.astype(BF16))
    tm = min(ROW_TILE, seq)
    row = lambda n: pl.BlockSpec((None, tm, n), lambda b, i: (b, i, 0))
    shp = lambda n, dt: jax.ShapeDtypeStruct((bsz, seq, n), dt)
    widths = [(WIDTH_C, BF16), (WIDTH_C, F32), (WIDTH_C, F32), (WIDTH_C, BF16), (WIDTH_C, F32),
              (WIDTH_D, F32), (SSD_CONV_CH, F32), (LANES, F32)]
    return pl.pallas_call(
        _proj_odd_kernel,
        grid=(bsz, seq // tm),
        in_specs=[row(d),
                  pl.BlockSpec((None, 1, mod.shape[-1]), lambda b, i: (b, 0, 0)),
                  pl.BlockSpec(w.shape, lambda b, i: (0, 0))],
        out_specs=[row(n) for n, _ in widths],
        out_shape=[shp(n, dt) for n, dt in widths],
        compiler_params=_cparams("parallel", "arbitrary"),
        name="proj_odd",
    )(x, mod, w)


def _out_ln_kernel(ya_ref, yb_ref, w_ref, x_ref, mod_ref, g_ref, b_ref, o_ref):
    d = x_ref.shape[-1]
    wa = ya_ref.shape[-1]
    y = _dot(ya_ref[...], w_ref[:wa, :]) + _dot(yb_ref[...], w_ref[wa:, :])
    gate = mod_ref[...][:, 2 * d:]
    t = DEEPNORM_ALPHA * x_ref[...] + gate * y
    mu = jnp.mean(t, axis=-1, keepdims=True)
    tc = t - mu
    var = jnp.mean(tc * tc, axis=-1, keepdims=True)
    o_ref[...] = tc * lax.rsqrt(var + LN_EPS) * g_ref[...] + b_ref[...]


def _out_ln(ya, yb, w_out, x, mod, ln_g, ln_b):
    bsz, seq, d = x.shape
    tm = min(ROW_TILE, seq)
    w = w_out.astype(BF16)
    row = lambda n: pl.BlockSpec((None, tm, n), lambda b, i: (b, i, 0))
    vec = pl.BlockSpec((1, d), lambda b, i: (0, 0))
    return pl.pallas_call(
        _out_ln_kernel,
        grid=(bsz, seq // tm),
        in_specs=[row(ya.shape[-1]), row(yb.shape[-1]),
                  pl.BlockSpec(w.shape, lambda b, i: (0, 0)),
                  row(d),
                  pl.BlockSpec((None, 1, mod.shape[-1]), lambda b, i: (b, 0, 0)),
                  vec, vec],
        out_specs=row(d),
        out_shape=jax.ShapeDtypeStruct((bsz, seq, d), F32),
        compiler_params=_cparams("parallel", "arbitrary"),
        name="out_ln",
    )(ya, yb, w, x, mod, ln_g.reshape(1, d), ln_b.reshape(1, d))


NA_BIAS_TYPES = NA_WIN_ROWS
NA_KEYS = NA_WIN_ROWS * GRID_W


def _na_bias_kernel(rpb_ref, o_ref):
    h = pl.program_id(0)
    shape = (GRID_W, LANES)
    q = lax.broadcasted_iota(jnp.int32, shape, 0)
    lane = lax.broadcasted_iota(jnp.int32, shape, 1)
    kc = lane & (GRID_W - 1)
    dc = kc - q + (NA_WIN_COLS - 1)
    cs = jnp.clip(q - NA_WIN_COLS // 2, 0, GRID_W - NA_WIN_COLS)
    valid = (kc >= cs) & (kc < cs + NA_WIN_COLS)
    upper = lane >= GRID_W
    n_dr, n_dc = 2 * NA_WIN_ROWS - 1, 2 * NA_WIN_COLS - 1
    tiles = []
    for dr in range(n_dr):
        acc = jnp.zeros(shape, F32)
        for d in range(n_dc):
            acc = jnp.where(dc == d, rpb_ref[(h * n_dr + dr) * n_dc + d], acc)
        tiles.append(jnp.where(valid, acc, NEG_INF))
    for t in range(NA_BIAS_TYPES):
        for c in range(NA_KEYS // LANES):
            dr0 = 2 * c + NA_WIN_ROWS - 1 - t
            o_ref[t, :, c * LANES:(c + 1) * LANES] = jnp.where(upper, tiles[dr0 + 1], tiles[dr0])


def _na_bias(rpb):
    return pl.pallas_call(
        _na_bias_kernel,
        grid=(NA_HEADS,),
        in_specs=[pl.BlockSpec(memory_space=pltpu.SMEM)],
        out_specs=pl.BlockSpec((NA_BIAS_TYPES, None, GRID_W, NA_KEYS), lambda h: (0, h, 0, 0)),
        out_shape=jax.ShapeDtypeStruct((NA_BIAS_TYPES, NA_HEADS, GRID_W, NA_KEYS), F32),
        compiler_params=_cparams("arbitrary"),
        name="na_bias",
    )(rpb.reshape(-1))


def _na_kernel(q_ref, k_ref, v_ref, g_ref, bias_ref, o_ref, *, n_rows):
    i = pl.program_id(1)
    lane = lax.broadcasted_iota(jnp.int32, (1, LANES), 1)
    head_mask = [(lane < NA_HEAD_DIM).astype(BF16), (lane >= NA_HEAD_DIM).astype(BF16)]
    half = NA_WIN_ROWS // 2
    for rr in range(NA_ROWS_PER_STEP):
        r = i * NA_ROWS_PER_STEP + rr
        row_start = jnp.clip(r - half, 0, n_rows - NA_WIN_ROWS)
        t = jnp.where(r < half, r, jnp.where(r > n_rows - half, r - (n_rows - NA_WIN_ROWS), half))
        koff = pl.multiple_of(row_start * GRID_W, GRID_W)
        rows = slice(rr * GRID_W, (rr + 1) * GRID_W)
        n_pairs = WIDTH_A // LANES
        pair_cols = [slice(p * LANES, (p + 1) * LANES) for p in range(n_pairs)]
        scores = []
        for p, cols in enumerate(pair_cols):
            qp = q_ref[rows, cols]
            qs = jnp.concatenate([qp * head_mask[0], qp * head_mask[1]], axis=0)
            kp = k_ref[pl.ds(koff, NA_KEYS), cols]
            scores.append(_dot_nt(qs, kp) + bias_ref[t, p])
        probs, norms = [], []
        for s in scores:
            e = jnp.exp(s - jnp.max(s, axis=-1, keepdims=True))
            norms.append(jnp.sum(e, axis=-1, keepdims=True))
            probs.append(e.astype(BF16))
        for p, cols in enumerate(pair_cols):
            vp = v_ref[pl.ds(koff, NA_KEYS), cols]
            o2 = _dot(probs[p], vp) / norms[p]
            o = jnp.where(lane < NA_HEAD_DIM, o2[:GRID_W], o2[GRID_W:])
            o_ref[rows, cols] = (o * _silu(g_ref[rows, cols])).astype(BF16)


def _neighbourhood_attention(aq, ak, av, ag, bias):
    bsz, seq, w = aq.shape
    n_rows = seq // GRID_W
    tq = NA_ROWS_PER_STEP * GRID_W
    bias = bias.reshape(NA_BIAS_TYPES, NA_HEADS // 2, 2 * GRID_W, NA_KEYS)
    row = pl.BlockSpec((None, tq, w), lambda b, i: (b, i, 0))
    full = pl.BlockSpec((None, seq, w), lambda b, i: (b, 0, 0))
    return pl.pallas_call(
        functools.partial(_na_kernel, n_rows=n_rows),
        grid=(bsz, n_rows // NA_ROWS_PER_STEP),
        in_specs=[row, full, full, row,
                  pl.BlockSpec(bias.shape, lambda b, i: (0, 0, 0, 0))],
        out_specs=row,
        out_shape=jax.ShapeDtypeStruct((bsz, seq, w), BF16),
        compiler_params=_cparams("parallel", "arbitrary"),
        name="na_attn",
    )(aq, ak, av, ag, bias)


def _rec_consts(chunk, reverse):
    nlev = int(math.log2(chunk))
    assert 1 << nlev == chunk
    w = np.zeros(((nlev + 2), chunk, chunk), np.float32)
    lv = np.full((chunk, chunk), -1, np.int32)
    idx = np.arange(chunk)
    for l in range(nlev):
        s = 1 << l
        off = idx % (2 * s)
        mid = idx - off + s - 1
        for p in range(chunk):
            if off[p] >= s:
                w[l, p, mid[p] + 1:p + 1] = 1.0
            else:
                w[l, p, p + 1:mid[p] + 1] = 1.0
        same = (idx[:, None] // (2 * s)) == (idx[None, :] // (2 * s))
        lv[same & (off[:, None] >= s) & (off[None, :] < s)] = l
    lv[idx, idx] = nlev
    w[nlev] = (idx[None, :] <= idx[:, None])
    if reverse:
        w = w[:, ::-1, ::-1]
        lv = lv[::-1, ::-1]
    w = np.concatenate([w[nlev]] + [w[l] for l in REC_SEL_LEVELS], axis=0)
    return (jnp.asarray(w, BF16), jnp.asarray(np.ascontiguousarray(lv)), nlev)


def _rec_kernel(*refs, mode, final, reverse, heads, dk, dv, nlev):
    chunk = REC_CHUNK
    if mode == "gla":
        q_ref, k_ref, lf_ref, v_ref, w_ref, lv_ref = refs[:6]
        rest = refs[6:]
    else:
        q_ref, z_ref, lbraw_ref, v_ref, w_ref, lv_ref = refs[:6]
        rest = refs[6:]
    if final:
        prev_ref, gate_ref, ng_ref, o_ref, st_ref = rest
    else:
        o_ref, st_ref = rest

    @pl.when(pl.program_id(1) == 0)
    def _():
        st_ref[...] = jnp.zeros_like(st_ref)

    lane = lax.broadcasted_iota(jnp.int32, (1, LANES), 1)
    row = lax.broadcasted_iota(jnp.int32, (chunk, LANES), 0)
    per_slab = LANES // dk
    n_slabs = heads * dk // LANES
    lv = lv_ref[...]
    level_masks = [lv == l for l in range(nlev + 1)]
    last = 0 if reverse else chunk - 1

    def is_query_side(l):
        off = row & (2 * (1 << l) - 1)
        return (off < (1 << l)) if reverse else (off >= (1 << l))

    query_side = [is_query_side(l) for l in range(nlev) if (1 << l) < SUBLANES]

    def boundary_exponent(b, l):
        s = 1 << l
        parts = []
        for blk in range(chunk // (2 * s)):
            mid = blk * 2 * s + (s if reverse else s - 1)
            b_mid = jnp.broadcast_to(b[mid:mid + 1, :], (s, LANES))
            lo, hi = b[blk * 2 * s:blk * 2 * s + s], b[blk * 2 * s + s:(blk + 1) * 2 * s]
            parts += [lo - b_mid, b_mid - hi] if reverse else [b_mid - lo, hi - b_mid]
        return jnp.concatenate(parts, axis=0)

    def query_or_key(q, k, l):
        if (1 << l) < SUBLANES:
            return jnp.where(query_side[l], q, k)
        s = 1 << l
        parts = []
        for blk in range(chunk // (2 * s)):
            lo, hi = slice(blk * 2 * s, blk * 2 * s + s), slice(blk * 2 * s + s, (blk + 1) * 2 * s)
            parts += [q[lo], k[hi]] if reverse else [k[lo], q[hi]]
        return jnp.concatenate(parts, axis=0)

    if mode == "hgrn":
        lbr = lbraw_ref[...]
        mx = jnp.maximum(lbr[0:1], lbr[1:2])
        e0, e1 = jnp.exp(lbr[0:1] - mx), jnp.exp(lbr[1:2] - mx)
        lb = e1 / (e0 + e1)
        log_lb, log_ub = jnp.log(lb), jnp.log1p(-lb)

    head_masks = [((lane >= hh * dk) & (lane < (hh + 1) * dk)) for hh in range(per_slab)]
    head_masks_bf = [hm.astype(BF16) for hm in head_masks]

    def stage_scale(ci):
        rows = slice(ci * chunk, (ci + 1) * chunk)
        if mode == "gla":
            lf_all = lf_ref[rows, :]
            k_all = None
        else:
            z = z_ref[rows, :]
            k_all = (1.0 - lb) * _sigmoid(-z)
            lc = log_ub + _log_sigmoid(z)
            lf_all = jnp.maximum(log_lb, lc) + _log_1p_exp_neg_abs(log_lb - lc)
        lf_all = lf_all * LOG2_E
        sums = _sel_dot(w_ref[...], lf_all)
        slabs = []
        for s in range(n_slabs):
            cols = slice(s * LANES, (s + 1) * LANES)
            q = q_ref[rows, cols].astype(F32)
            k = k_ref[rows, cols].astype(F32) if mode == "gla" else k_all[:, cols]
            lf = lf_all[:, cols]
            b = sums[:chunk, cols]
            ys = []
            for l in range(nlev):
                if l == 0:
                    e = jnp.where(query_side[0], lf, 0.0)
                elif l in REC_SEL_LEVELS:
                    i0 = (1 + REC_SEL_LEVELS.index(l)) * chunk
                    e = sums[i0:i0 + chunk, cols]
                else:
                    e = boundary_exponent(b, l)
                ys.append((jnp.exp2(e) * query_or_key(q, k, l)).astype(BF16))
            eb = jnp.exp2(b)
            b_end = jnp.broadcast_to(b[last:last + 1, :], (chunk, LANES))
            slabs.append(dict(
                lhs=ys + [q.astype(BF16)], rhs=ys + [k.astype(BF16)],
                q_in=(eb * q).astype(BF16),
                k_up=(jnp.exp2(b_end - b) * k).astype(BF16),
                decay=eb[last:last + 1]))
        return rows, slabs

    lane2 = lax.broadcasted_iota(jnp.int32, (1, 2 * dv), 1)
    v_first, v_second = (lane2 < dv).astype(BF16), (lane2 >= dv).astype(BF16)
    pair_masks = [jnp.concatenate([m, m], axis=1) for m in level_masks]

    def block_diag(a, b_):
        za, zb = jnp.zeros_like(a), jnp.zeros_like(b_)
        return jnp.concatenate([jnp.concatenate([a, zb], axis=1), jnp.concatenate([za, b_], axis=1)], axis=0)

    def pair_operands(slabs, pr, key, l=None):
        pick = (lambda d: d[key][l]) if l is not None else (lambda d: d[key])
        if per_slab == 2:
            t = pick(slabs[pr])
            return t, jnp.concatenate([t * head_masks_bf[0], t * head_masks_bf[1]], axis=0)
        a, b_ = pick(slabs[2 * pr]), pick(slabs[2 * pr + 1])
        return jnp.concatenate([a, b_], axis=1), block_diag(a, b_)

    def stage_intra(slabs):
        mats = []
        for pr in range(heads // 2):
            a_mat = jnp.zeros((chunk, 2 * chunk), F32)
            for l in range(nlev + 1):
                lhs = pair_operands(slabs, pr, "lhs", l)[0]
                rhs = pair_operands(slabs, pr, "rhs", l)[1]
                a_mat = jnp.where(pair_masks[l], _dot_nt(lhs, rhs), a_mat)
            mats.append(a_mat.astype(BF16))
        return mats

    def stage_out(rows, slabs, mats):
        for pr in range(heads // 2):
            pcols = slice(pr * 2 * dv, (pr + 1) * 2 * dv)
            vp = v_ref[rows, pcols]
            v_bd = jnp.concatenate([vp * v_first, vp * v_second], axis=0)
            q_in = pair_operands(slabs, pr, "q_in")[0]
            k_up = pair_operands(slabs, pr, "k_up")[0]
            if per_slab == 2:
                st = st_ref[pr]
                st_bf = st.astype(BF16)
                st_bd = jnp.concatenate([st_bf * head_masks_bf[0], st_bf * head_masks_bf[1]], axis=0)
            else:
                st_a, st_b = st_ref[2 * pr], st_ref[2 * pr + 1]
                st_bd = block_diag(st_a.astype(BF16), st_b.astype(BF16))
            o = _dot(mats[pr], v_bd) + _dot_nt(q_in, st_bd)
            u = _dot_tn(vp, k_up)
            if per_slab == 2:
                st_ref[pr] = st * slabs[pr]["decay"] + jnp.where(head_masks[0], u[:dv], u[dv:])
            else:
                st_ref[2 * pr] = st_a * slabs[2 * pr]["decay"] + u[:dv, :LANES]
                st_ref[2 * pr + 1] = st_b * slabs[2 * pr + 1]["decay"] + u[dv:, LANES:]
            if final:
                for hh in range(2):
                    ocols = slice((2 * pr + hh) * dv, (2 * pr + hh + 1) * dv)
                    tot = prev_ref[rows, ocols] + o[:, hh * dv:(hh + 1) * dv]
                    ms = jnp.mean(tot * tot, axis=-1, keepdims=True)
                    y = tot * lax.rsqrt(ms + RMS_EPS) * ng_ref[...]
                    o_ref[rows, ocols] = (y * _silu(gate_ref[rows, ocols])).astype(o_ref.dtype)
            else:
                o_ref[rows, pcols] = o

    n_sub = q_ref.shape[0] // chunk
    order = list(range(n_sub - 1, -1, -1) if reverse else range(n_sub))
    scaled, intra = {}, {}
    for t in range(n_sub + 2):
        if t < n_sub:
            scaled[t] = stage_scale(order[t])
        if 0 <= t - 1 < n_sub:
            intra[t - 1] = stage_intra(scaled[t - 1][1])
        if 0 <= t - 2 < n_sub:
            stage_out(*scaled.pop(t - 2), intra.pop(t - 2))


def _gated_recurrence(mode, q, kz, lf_or_lb, v, prev, gate, norm_g, *, reverse, heads, dk, dv):
    bsz, seq, _ = q.shape
    chunk = REC_CHUNK
    blk_rows = min(REC_CHUNKS_PER_STEP * chunk, seq)
    n = seq // blk_rows
    final = prev is not None
    w_sel, lv, nlev = _rec_consts(chunk, reverse)
    cidx = (lambda b, i: (b, n - 1 - i, 0)) if reverse else (lambda b, i: (b, i, 0))
    row = lambda width: pl.BlockSpec((None, blk_rows, width), cidx)
    const2 = lambda a: pl.BlockSpec(a.shape, lambda b, i: (0, 0))
    in_specs = [row(heads * dk), row(heads * dk),
                row(heads * dk) if mode == "gla" else const2(lf_or_lb),
                row(heads * dv), const2(w_sel), const2(lv)]
    args = [q, kz, lf_or_lb, v, w_sel, lv]
    if final:
        in_specs += [row(heads * dv), row(heads * dv), pl.BlockSpec((1, dv), lambda b, i: (0, 0))]
        args += [prev, gate, norm_g.reshape(1, dv)]
    return pl.pallas_call(
        functools.partial(_rec_kernel, mode=mode, final=final, reverse=reverse,
                          heads=heads, dk=dk, dv=dv, nlev=nlev),
        grid=(bsz, n),
        in_specs=in_specs,
        out_specs=row(heads * dv),
        out_shape=jax.ShapeDtypeStruct((bsz, seq, heads * dv), BF16 if final else F32),
        scratch_shapes=[pltpu.VMEM((heads * dk // LANES, dv, LANES), F32)],
        compiler_params=_cparams("parallel", "arbitrary"),
        name=f"{mode}_{'bwd' if reverse else 'fwd'}",
    )(*args)


def _ssd_consts(chunk, reverse):
    idx = np.arange(chunk)
    if reverse:
        tri = idx[None, :] >= idx[:, None]
        rest = idx[None, :] < idx[:, None]
    else:
        tri = idx[None, :] <= idx[:, None]
        rest = idx[None, :] > idx[:, None]
    sel = np.concatenate([tri, rest], axis=0).astype(np.float32)
    expand = np.zeros((LANES, WIDTH_D), np.float32)
    for h in range(SSD_HEADS):
        expand[h + (SSD_HEADS if reverse else 0), h * SSD_HEAD_DIM:(h + 1) * SSD_HEAD_DIM] = 1.0
    return jnp.asarray(sel, BF16), jnp.asarray(expand, BF16)


def _ssd_kernel(*refs, final, reverse, n_blocks):
    chunk = REC_CHUNK
    if final:
        (xs_ref, bc_ref, dt_ref, dtb_ref, alog_ref, sel_ref, exp_ref,
         prev_ref, dz_ref, dskip_ref, ng_ref, o_ref, st_ref) = refs
    else:
        (x_ref, xprev_ref, xnext_ref, dt_ref, cw_ref, cb_ref, dtb_ref, alog_ref, sel_ref, exp_ref,
         o_ref, xs_ref, bc_ref, ext_ref, st_ref) = refs
    step = pl.program_id(1)
    blk = (n_blocks - 1 - step) if reverse else step
    blk_rows = dt_ref.shape[0]

    @pl.when(step == 0)
    def _():
        st_ref[...] = jnp.zeros_like(st_ref)

    if not final:
        halo = SUBLANES
        ext_ref[0:halo] = jnp.where(blk == 0, 0.0, xprev_ref[...])
        ext_ref[halo:halo + blk_rows] = x_ref[...]
        ext_ref[halo + blk_rows:2 * halo + blk_rows] = jnp.where(blk == n_blocks - 1, 0.0, xnext_ref[...])
        acc = cb_ref[...]
        for tap in range(SSD_CONV):
            o0 = halo - SSD_CONV // 2 + tap
            acc = acc + ext_ref[o0:o0 + blk_rows] * cw_ref[tap:tap + 1]
        xbc = _silu(acc)
        xs_ref[...] = xbc[:, :WIDTH_D]
        bc_ref[...] = xbc[:, WIDTH_D:].astype(BF16)

    gs = SSD_GROUPS * SSD_STATE
    last = 0 if reverse else chunk - 1
    ii = lax.broadcasted_iota(jnp.int32, (chunk, chunk), 0)
    jj = lax.broadcasted_iota(jnp.int32, (chunk, chunk), 1)
    causal = (ii <= jj) if reverse else (ii >= jj)
    lane = lax.broadcasted_iota(jnp.int32, (1, LANES), 1)
    lo_half = lane < SSD_HEAD_DIM
    col0 = SSD_HEADS if reverse else 0
    rep = SSD_HEADS // SSD_GROUPS
    n_pairs = SSD_HEADS // 2
    neg_a = -jnp.exp(alog_ref[...])

    def stage_decay(ci):
        rows = slice(ci * chunk, (ci + 1) * chunk)
        xs = xs_ref[rows, :]
        bm, cm = bc_ref[rows, :gs], bc_ref[rows, gs:]
        dt = _softplus(dt_ref[rows, :] + dtb_ref[...])
        sums = _sel_dot(sel_ref[...], dt * neg_a)
        a_cum, a_rem = sums[:chunk], sums[chunk:]
        a_cum_t = a_cum.T
        expand = exp_ref[...]
        x_dt = xs * _dot_sel(dt, expand)
        e_cum = jnp.exp(_dot_sel(a_cum, expand))
        x_rem = (x_dt * jnp.exp(_dot_sel(a_rem, expand))).astype(BF16)
        cbs = [_dot_nt(cm[:, g * SSD_STATE:(g + 1) * SSD_STATE], bm[:, g * SSD_STATE:(g + 1) * SSD_STATE])
               for g in range(SSD_GROUPS)]
        mats, stacks = [], []
        for p in range(n_pairs):
            pair = []
            for hh in range(2):
                ch = col0 + 2 * p + hh
                diff = a_cum[:, ch:ch + 1] - a_cum_t[ch:ch + 1, :]
                seg = jnp.exp(jnp.where(causal, diff, NEG_INF))
                pair.append((cbs[(2 * p) // rep] * seg).astype(BF16))
            mats.append(jnp.concatenate(pair, axis=1))
            xp = x_dt[:, p * LANES:(p + 1) * LANES]
            stacks.append(jnp.concatenate([jnp.where(lo_half, xp, 0.0), jnp.where(lo_half, 0.0, xp)],
                                          axis=0).astype(BF16))
        return dict(rows=rows, xs=xs, bm=bm, cm=cm, e_cum=e_cum, x_rem=x_rem, mats=mats, stacks=stacks)

    def stage_out(d):
        rows = d["rows"]
        ys = []
        for p in range(n_pairs):
            g = (2 * p) // rep
            gcols = slice(g * SSD_STATE, (g + 1) * SSD_STATE)
            pcols = slice(p * LANES, (p + 1) * LANES)
            st = st_ref[p]
            y = _dot(d["mats"][p], d["stacks"][p]) + d["e_cum"][:, pcols] * _dot(d["cm"][:, gcols], st.astype(BF16))
            upd = _dot_tn(d["bm"][:, gcols], d["x_rem"][:, pcols])
            st_ref[p] = st * d["e_cum"][last:last + 1, pcols] + upd
            if final:
                y = y + prev_ref[rows, pcols] + dskip_ref[:, pcols] * d["xs"][:, pcols]
                ys.append(y * _silu(dz_ref[rows, pcols]))
            else:
                o_ref[rows, pcols] = y
        if final:
            t = jnp.concatenate(ys, axis=1)
            ms = jnp.mean(t * t, axis=-1, keepdims=True)
            o_ref[rows, :] = (t * lax.rsqrt(ms + RMS_EPS) * ng_ref[...]).astype(o_ref.dtype)

    n_sub = blk_rows // chunk
    order = list(range(n_sub - 1, -1, -1) if reverse else range(n_sub))
    pending = None
    for ci in order:
        cur = stage_decay(ci)
        if pending is not None:
            stage_out(pending)
        pending = cur
    stage_out(pending)


def _ssd(dxbc, dt_raw, conv_w, conv_b, dt_bias, a_log, dz, d_skip, norm_g):
    bsz, seq, nch = dxbc.shape
    chunk = REC_CHUNK
    blk_rows = min(REC_CHUNKS_PER_STEP * chunk, seq)
    n = seq // blk_rows
    hb = blk_rows // SUBLANES
    nb = seq // SUBLANES
    const2 = lambda a: pl.BlockSpec(a.shape, lambda b, i: (0, 0))
    dtb = jnp.zeros((1, LANES), F32).at[0, :2 * SSD_HEADS].set(dt_bias.reshape(-1))
    alog = jnp.full((1, LANES), NEG_INF, F32).at[0, :2 * SSD_HEADS].set(a_log.reshape(-1))
    cb2 = conv_b.reshape(1, nch)
    state = pltpu.VMEM((SSD_HEADS // 2, SSD_STATE, LANES), F32)
    shp = lambda width, dt: jax.ShapeDtypeStruct((bsz, seq, width), dt)

    sel, expand = _ssd_consts(chunk, False)
    row = lambda width: pl.BlockSpec((None, blk_rows, width), lambda b, i: (b, i, 0))
    y_fwd, xs, bc = pl.pallas_call(
        functools.partial(_ssd_kernel, final=False, reverse=False, n_blocks=n),
        grid=(bsz, n),
        in_specs=[row(nch),
                  pl.BlockSpec((None, SUBLANES, nch), lambda b, i: (b, jnp.maximum(i * hb - 1, 0), 0)),
                  pl.BlockSpec((None, SUBLANES, nch), lambda b, i: (b, jnp.minimum((i + 1) * hb, nb - 1), 0)),
                  row(LANES), const2(conv_w), const2(cb2), const2(dtb), const2(alog), const2(sel), const2(expand)],
        out_specs=[row(WIDTH_D), row(WIDTH_D), row(nch - WIDTH_D)],
        out_shape=[shp(WIDTH_D, F32), shp(WIDTH_D, F32), shp(nch - WIDTH_D, BF16)],
        scratch_shapes=[pltpu.VMEM((blk_rows + 2 * SUBLANES, nch), F32), state],
        compiler_params=_cparams("parallel", "arbitrary"),
        name="ssd_fwd",
    )(dxbc, dxbc, dxbc, dt_raw, conv_w, cb2, dtb, alog, sel, expand)

    sel, expand = _ssd_consts(chunk, True)
    row = lambda width: pl.BlockSpec((None, blk_rows, width), lambda b, i: (b, n - 1 - i, 0))
    dsk = jnp.repeat(d_skip, SSD_HEAD_DIM).reshape(1, WIDTH_D)
    ng = norm_g.reshape(1, WIDTH_D)
    return pl.pallas_call(
        functools.partial(_ssd_kernel, final=True, reverse=True, n_blocks=n),
        grid=(bsz, n),
        in_specs=[row(WIDTH_D), row(nch - WIDTH_D), row(LANES), const2(dtb), const2(alog), const2(sel), const2(expand),
                  row(WIDTH_D), row(WIDTH_D), const2(dsk), const2(ng)],
        out_specs=row(WIDTH_D),
        out_shape=shp(WIDTH_D, BF16),
        scratch_shapes=[state],
        compiler_params=_cparams("parallel", "arbitrary"),
        name="ssd_bwd",
    )(xs, bc, dt_raw, dtb, alog, sel, expand, y_fwd, dz, dsk, ng)


def kernel(x, c, ada_w, ada_b, ln_g, ln_b, e_w_in, e_rpb, e_gla_w_up, e_gla_b, e_gla_norm_g, e_w_out,
           o_w_in, hgrn_lb, o_hgrn_norm_g, o_conv_w, o_conv_b, o_dt_bias, o_a_log, o_d_skip,
           o_ssm_norm_g, o_w_out):
    mod = _ada_mod(c, ada_w, ada_b)

    aq, ak, av, ag, bq, bk, bv, bg, lf_f, lf_b = _proj_even(x, mod[0], e_w_in[0], e_gla_w_up[0], e_gla_b[0])
    ya = _neighbourhood_attention(aq, ak, av, ag, _na_bias(e_rpb[0]))
    gla = functools.partial(_gated_recurrence, "gla", heads=GLA_HEADS, dk=GLA_DK, dv=GLA_DV)
    o_fwd = gla(bq, bk, lf_f, bv, None, None, None, reverse=False)
    yb = gla(bq, bk, lf_b, bv, o_fwd, bg, e_gla_norm_g[0], reverse=True)
    x = _out_ln(ya, yb, e_w_out[0], x, mod[0], ln_g[0], ln_b[0])

    cq, cf_f, cf_b, ci, cg, dz, dxbc, dt_raw = _proj_odd(x, mod[1], o_w_in[0])
    hgrn = functools.partial(_gated_recurrence, "hgrn", heads=HGRN_HEADS, dk=HGRN_DIM, dv=HGRN_DIM)
    o_fwd = hgrn(cq, cf_f, hgrn_lb, ci, None, None, None, reverse=False)
    yc = hgrn(cq, cf_b, hgrn_lb, ci, o_fwd, cg, o_hgrn_norm_g[0], reverse=True)
    yd = _ssd(dxbc, dt_raw, o_conv_w[0], o_conv_b[0], o_dt_bias[0], o_a_log[0], dz, o_d_skip[0], o_ssm_norm_g[0])
    return _out_ln(yc, yd, o_w_out[0], x, mod[1], ln_g[1], ln_b[1])
```

```python
import functools
import math

import numpy as np
import jax
import jax.numpy as jnp
from jax import lax
from jax.experimental import pallas as pl
from jax.experimental.pallas import tpu as pltpu

F32 = jnp.float32
BF16 = jnp.bfloat16

GRID_W = 64
NA_HEADS, NA_HEAD_DIM = 8, 64
NA_WIN_ROWS, NA_WIN_COLS = 8, 16
WIDTH_A = NA_HEADS * NA_HEAD_DIM
GLA_HEADS, GLA_DK, GLA_DV, GLA_RANK = 4, 64, 128, 16
GLA_GATE_NORM = 16.0
WIDTH_B = GLA_HEADS * GLA_DV
HGRN_HEADS, HGRN_DIM = 4, 128
WIDTH_C = HGRN_HEADS * HGRN_DIM
SSD_HEADS, SSD_HEAD_DIM, SSD_GROUPS, SSD_STATE, SSD_CONV = 8, 64, 2, 128, 4
WIDTH_D = SSD_HEADS * SSD_HEAD_DIM
SSD_CONV_CH = WIDTH_D + 2 * SSD_GROUPS * SSD_STATE
DEPTH = 2
DEEPNORM_ALPHA = (2 * DEPTH) ** 0.25
LN_EPS = 1e-5
RMS_EPS = 1e-6

LANES = 128
SUBLANES = 8
VMEM_LIMIT = 56 * 1024 * 1024

ROW_TILE = 512
REC_CHUNK = 128
REC_CHUNKS_PER_STEP = 4
REC_SEL_LEVELS = (1, 2)
NA_ROWS_PER_STEP = 4
NEG_INF = float("-inf")
LOG2_E = math.log2(math.e)


def _cparams(*sem):
    return pltpu.CompilerParams(dimension_semantics=sem, vmem_limit_bytes=VMEM_LIMIT)


def _dot(a, b):
    return jnp.dot(a, b, preferred_element_type=F32)


def _dot_nt(a, b):
    return lax.dot_general(a, b, (((1,), (1,)), ((), ())), preferred_element_type=F32)


def _dot_tn(a, b):
    return lax.dot_general(a, b, (((0,), (0,)), ((), ())), preferred_element_type=F32)


def _split3(x):
    hi = x.astype(BF16)
    r = x - hi.astype(F32)
    mid = r.astype(BF16)
    lo = (r - mid.astype(F32)).astype(BF16)
    return hi, mid, lo


def _sel_dot(sel, x):
    hi, mid, lo = _split3(x)
    return _dot(sel, hi) + _dot(sel, mid) + _dot(sel, lo)


def _dot_sel(x, sel):
    hi, mid, lo = _split3(x)
    return _dot(hi, sel) + _dot(mid, sel) + _dot(lo, sel)


def _sigmoid(x):
    return 1.0 / (1.0 + jnp.exp(-x))


def _silu(x):
    return x * _sigmoid(x)


def _log1pexp_neg_abs(x):
    return jnp.log1p(jnp.exp(-jnp.abs(x)))


def _softplus(x):
    return jnp.maximum(x, 0.0) + _log1pexp_neg_abs(x)


def _ada_kernel(c_ref, w_ref, b_ref, o_ref):
    cond = _silu(c_ref[...]).astype(BF16)
    o_ref[...] = _dot(cond, w_ref[...].astype(BF16)) + b_ref[...]


def _ada_mod(c, ada_w, ada_b):
    bsz, d = c.shape
    bp = -(-bsz // SUBLANES) * SUBLANES
    n3 = ada_w.shape[-1]
    tn = ROW_TILE
    c_pad = jnp.zeros((bp, d), F32).at[:bsz].set(c)
    out = pl.pallas_call(
        _ada_kernel,
        grid=(DEPTH, n3 // tn),
        in_specs=[pl.BlockSpec((bp, d), lambda l, j: (0, 0)),
                  pl.BlockSpec((None, d, tn), lambda l, j: (l, 0, j)),
                  pl.BlockSpec((None, 1, tn), lambda l, j: (l, 0, j))],
        out_specs=pl.BlockSpec((None, bp, tn), lambda l, j: (l, 0, j)),
        out_shape=jax.ShapeDtypeStruct((DEPTH, bp, n3), F32),
        compiler_params=_cparams("arbitrary", "arbitrary"),
        name="ada_mod",
    )(c_pad, ada_w, ada_b.reshape(DEPTH, 1, n3))
    return out[:, :bsz].reshape(DEPTH, bsz, 1, n3)


def _modulated(x_ref, mod_ref):
    d = x_ref.shape[-1]
    mod = mod_ref[...]
    return (x_ref[...] * (1.0 + mod[:, d:2 * d]) + mod[:, :d]).astype(BF16)


def _proj_even_kernel(x_ref, mod_ref, w_ref, wup_ref, gb_ref,
                      aq_ref, ak_ref, av_ref, ag_ref, bq_ref, bk_ref, bv_ref, bg_ref, lff_ref, lfb_ref, h_ref):
    h_ref[...] = _modulated(x_ref, mod_ref)

    def mm(lo, hi):
        return _dot(h_ref[...], w_ref[:, lo:hi])

    a, kb = WIDTH_A, GLA_HEADS * GLA_DK
    o = 4 * a + 2 * kb + 2 * WIDTH_B
    lr = mm(o, o + LANES)
    lr_hi = lr.astype(BF16)
    lr_lo = (lr - lr_hi.astype(F32)).astype(BF16)
    for d, out_ref in enumerate((lff_ref, lfb_ref)):
        wu = wup_ref[d]
        wu_hi = wu.astype(BF16)
        wu_lo = (wu - wu_hi.astype(F32)).astype(BF16)
        out_ref[...] = _dot(lr_hi, wu_hi) + _dot(lr_lo, wu_hi) + _dot(lr_hi, wu_lo) + gb_ref[d]

    def finish_gate(out_ref):
        z = out_ref[...]
        log_sig = jnp.minimum(z, 0.0) - jnp.log(1.0 + jnp.exp2(jnp.abs(z) * (-LOG2_E)))
        out_ref[...] = log_sig * (LOG2_E / GLA_GATE_NORM)

    aq_ref[...] = (mm(0, a) * (NA_HEAD_DIM ** -0.5)).astype(BF16)
    ak_ref[...] = mm(a, 2 * a).astype(BF16)
    finish_gate(lff_ref)
    av_ref[...] = mm(2 * a, 3 * a).astype(BF16)
    ag_ref[...] = mm(3 * a, 4 * a)
    finish_gate(lfb_ref)
    o = 4 * a
    bq_ref[...] = (mm(o, o + kb) * (GLA_DK ** -0.5)).astype(BF16)
    bk_ref[...] = mm(o + kb, o + 2 * kb).astype(BF16)
    o += 2 * kb
    bv_ref[...] = mm(o, o + WIDTH_B).astype(BF16)
    bg_ref[...] = mm(o + WIDTH_B, o + 2 * WIDTH_B)


def _proj_even(x, mod, w_in, gla_w_up, gla_b):
    bsz, seq, d = x.shape
    kb = GLA_HEADS * GLA_DK
    n_main = 4 * WIDTH_A + 2 * kb + 2 * WIDTH_B
    w = jnp.concatenate([w_in.astype(BF16), jnp.zeros((d, n_main + LANES - w_in.shape[1]), BF16)], axis=1)
    wup = jnp.zeros((2, LANES, kb), F32)
    wup = wup.at[0, :GLA_RANK].set(gla_w_up[0]).at[1, GLA_RANK:2 * GLA_RANK].set(gla_w_up[1])
    tm = min(ROW_TILE, seq)
    row = lambda n: pl.BlockSpec((None, tm, n), lambda b, i: (b, i, 0))
    shp = lambda n, dt: jax.ShapeDtypeStruct((bsz, seq, n), dt)
    widths = [(WIDTH_A, BF16), (WIDTH_A, BF16), (WIDTH_A, BF16), (WIDTH_A, F32),
              (kb, BF16), (kb, BF16), (WIDTH_B, BF16), (WIDTH_B, F32), (kb, F32), (kb, F32)]
    return pl.pallas_call(
        _proj_even_kernel,
        grid=(bsz, seq // tm),
        in_specs=[row(d),
                  pl.BlockSpec((None, 1, mod.shape[-1]), lambda b, i: (b, 0, 0)),
                  pl.BlockSpec(w.shape, lambda b, i: (0, 0)),
                  pl.BlockSpec(wup.shape, lambda b, i: (0, 0, 0)),
                  pl.BlockSpec((2, 1, kb), lambda b, i: (0, 0, 0))],
        out_specs=[row(n) for n, _ in widths],
        out_shape=[shp(n, dt) for n, dt in widths],
        scratch_shapes=[pltpu.VMEM((tm, d), BF16)],
        compiler_params=_cparams("parallel", "arbitrary"),
        name="proj_even",
    )(x, mod, w, wup, gla_b.reshape(2, 1, kb))


def _proj_odd_kernel(x_ref, mod_ref, w_ref, lbraw_ref,
                     cq_ref, kf_ref, lff_ref, kb_ref, lfb_ref, ci_ref, cg_ref, dz_ref, dxbc_ref, dt_ref, h_ref):
    h_ref[...] = _modulated(x_ref, mod_ref)

    def mm(lo, hi):
        return _dot(h_ref[...], w_ref[:, lo:hi])

    c = WIDTH_C
    lbr = lbraw_ref[...]
    mx = jnp.maximum(lbr[0:1], lbr[1:2])
    e0, e1 = jnp.exp(lbr[0:1] - mx), jnp.exp(lbr[1:2] - mx)
    lb = e1 / (e0 + e1)
    log_lb, log_ub = jnp.log(lb), jnp.log1p(-lb)
    gate_refs = ((kf_ref, lff_ref), (kb_ref, lfb_ref))
    for d, (_, lf_ref) in enumerate(gate_refs):
        lf_ref[...] = mm((1 + d) * c, (2 + d) * c)

    def finish_gate(d):
        k_ref, lf_ref = gate_refs[d]
        z = lf_ref[...]
        t = jnp.exp2(jnp.abs(z) * (-LOG2_E))
        u = 1.0 + t
        k_ref[...] = (1.0 - lb) * (jnp.where(z > 0.0, t, 1.0) / u)
        lc = log_ub + (jnp.minimum(z, 0.0) - jnp.log(u))
        w = 1.0 + jnp.exp2(jnp.abs(log_lb - lc) * (-LOG2_E))
        lf_ref[...] = (jnp.maximum(log_lb, lc) + jnp.log(w)) * LOG2_E

    cq_ref[...] = (mm(0, c) * (HGRN_DIM ** -0.5)).astype(BF16)
    ci_ref[...] = mm(3 * c, 4 * c).astype(BF16)
    finish_gate(0)
    cg_ref[...] = mm(4 * c, 5 * c)
    o = 5 * c
    dz_ref[...] = mm(o, o + WIDTH_D)
    finish_gate(1)
    o += WIDTH_D
    dxbc_ref[...] = mm(o, o + SSD_CONV_CH)
    o += SSD_CONV_CH
    dt_ref[...] = mm(o, o + LANES)


def _proj_odd(x, mod, w_in, hgrn_lb):
    bsz, seq, d = x.shape
    n_main = 5 * WIDTH_C + WIDTH_D + SSD_CONV_CH
    w = jnp.concatenate([w_in.astype(BF16), jnp.zeros((d, n_main + LANES - w_in.shape[1]), BF16)], axis=1)
    tm = min(ROW_TILE, seq)
    row = lambda n: pl.BlockSpec((None, tm, n), lambda b, i: (b, i, 0))
    shp = lambda n, dt: jax.ShapeDtypeStruct((bsz, seq, n), dt)
    widths = [(WIDTH_C, BF16), (WIDTH_C, F32), (WIDTH_C, F32), (WIDTH_C, F32), (WIDTH_C, F32), (WIDTH_C, BF16),
              (WIDTH_C, F32), (WIDTH_D, F32), (SSD_CONV_CH, F32), (LANES, F32)]
    return pl.pallas_call(
        _proj_odd_kernel,
        grid=(bsz, seq // tm),
        in_specs=[row(d),
                  pl.BlockSpec((None, 1, mod.shape[-1]), lambda b, i: (b, 0, 0)),
                  pl.BlockSpec(w.shape, lambda b, i: (0, 0)),
                  pl.BlockSpec(hgrn_lb.shape, lambda b, i: (0, 0))],
        out_specs=[row(n) for n, _ in widths],
        out_shape=[shp(n, dt) for n, dt in widths],
        scratch_shapes=[pltpu.VMEM((tm, d), BF16)],
        compiler_params=_cparams("parallel", "arbitrary"),
        name="proj_odd",
    )(x, mod, w, hgrn_lb)


def _out_ln_kernel(ya_ref, yb_ref, w_ref, x_ref, mod_ref, g_ref, b_ref, o_ref):
    d = x_ref.shape[-1]
    wa = ya_ref.shape[-1]
    y = _dot(ya_ref[...], w_ref[:wa, :]) + _dot(yb_ref[...], w_ref[wa:, :])
    gate = mod_ref[...][:, 2 * d:]
    t = DEEPNORM_ALPHA * x_ref[...] + gate * y
    mu = jnp.mean(t, axis=-1, keepdims=True)
    tc = t - mu
    var = jnp.mean(tc * tc, axis=-1, keepdims=True)
    o_ref[...] = tc * lax.rsqrt(var + LN_EPS) * g_ref[...] + b_ref[...]


def _out_ln(ya, yb, w_out, x, mod, ln_g, ln_b):
    bsz, seq, d = x.shape
    tm = min(ROW_TILE, seq)
    w = w_out.astype(BF16)
    row = lambda n: pl.BlockSpec((None, tm, n), lambda b, i: (b, i, 0))
    vec = pl.BlockSpec((1, d), lambda b, i: (0, 0))
    return pl.pallas_call(
        _out_ln_kernel,
        grid=(bsz, seq // tm),
        in_specs=[row(ya.shape[-1]), row(yb.shape[-1]),
                  pl.BlockSpec(w.shape, lambda b, i: (0, 0)),
                  row(d),
                  pl.BlockSpec((None, 1, mod.shape[-1]), lambda b, i: (b, 0, 0)),
                  vec, vec],
        out_specs=row(d),
        out_shape=jax.ShapeDtypeStruct((bsz, seq, d), F32),
        compiler_params=_cparams("parallel", "arbitrary"),
        name="out_ln",
    )(ya, yb, w, x, mod, ln_g.reshape(1, d), ln_b.reshape(1, d))


NA_BIAS_TYPES = NA_WIN_ROWS
NA_KEYS = NA_WIN_ROWS * GRID_W


def _na_bias_kernel(rpb_ref, o_ref):
    h = pl.program_id(0)
    shape = (GRID_W, LANES)
    q = lax.broadcasted_iota(jnp.int32, shape, 0)
    lane = lax.broadcasted_iota(jnp.int32, shape, 1)
    kc = lane & (GRID_W - 1)
    dc = kc - q + (NA_WIN_COLS - 1)
    cs = jnp.clip(q - NA_WIN_COLS // 2, 0, GRID_W - NA_WIN_COLS)
    valid = (kc >= cs) & (kc < cs + NA_WIN_COLS)
    upper = lane >= GRID_W
    n_dr, n_dc = 2 * NA_WIN_ROWS - 1, 2 * NA_WIN_COLS - 1
    tiles = []
    for dr in range(n_dr):
        acc = jnp.zeros(shape, F32)
        for d in range(n_dc):
            acc = jnp.where(dc == d, rpb_ref[(h * n_dr + dr) * n_dc + d], acc)
        tiles.append(jnp.where(valid, acc, NEG_INF))
    for t in range(NA_BIAS_TYPES):
        for c in range(NA_KEYS // LANES):
            dr0 = 2 * c + NA_WIN_ROWS - 1 - t
            o_ref[t, :, c * LANES:(c + 1) * LANES] = jnp.where(upper, tiles[dr0 + 1], tiles[dr0])


def _na_bias(rpb):
    return pl.pallas_call(
        _na_bias_kernel,
        grid=(NA_HEADS,),
        in_specs=[pl.BlockSpec(memory_space=pltpu.SMEM)],
        out_specs=pl.BlockSpec((NA_BIAS_TYPES, None, GRID_W, NA_KEYS), lambda h: (0, h, 0, 0)),
        out_shape=jax.ShapeDtypeStruct((NA_BIAS_TYPES, NA_HEADS, GRID_W, NA_KEYS), F32),
        compiler_params=_cparams("arbitrary"),
        name="na_bias",
    )(rpb.reshape(-1))


def _na_kernel(q_ref, k_ref, v_ref, g_ref, bias_ref, o_ref, *, n_rows):
    i = pl.program_id(1)
    lane = lax.broadcasted_iota(jnp.int32, (1, LANES), 1)
    head_mask = [(lane < NA_HEAD_DIM).astype(BF16), (lane >= NA_HEAD_DIM).astype(BF16)]
    half = NA_WIN_ROWS // 2
    for rr in range(NA_ROWS_PER_STEP):
        r = i * NA_ROWS_PER_STEP + rr
        row_start = jnp.clip(r - half, 0, n_rows - NA_WIN_ROWS)
        t = jnp.where(r < half, r, jnp.where(r > n_rows - half, r - (n_rows - NA_WIN_ROWS), half))
        koff = pl.multiple_of(row_start * GRID_W, GRID_W)
        rows = slice(rr * GRID_W, (rr + 1) * GRID_W)
        n_pairs = WIDTH_A // LANES
        pair_cols = [slice(p * LANES, (p + 1) * LANES) for p in range(n_pairs)]
        scores = []
        for p, cols in enumerate(pair_cols):
            qp = q_ref[rows, cols]
            qs = jnp.concatenate([qp * head_mask[0], qp * head_mask[1]], axis=0)
            kp = k_ref[pl.ds(koff, NA_KEYS), cols]
            scores.append(_dot_nt(qs, kp) + bias_ref[t, p])
        probs, norms = [], []
        for s in scores:
            e = jnp.exp(s - jnp.max(s, axis=-1, keepdims=True))
            norms.append(jnp.sum(e, axis=-1, keepdims=True))
            probs.append(e.astype(BF16))
        for p, cols in enumerate(pair_cols):
            vp = v_ref[pl.ds(koff, NA_KEYS), cols]
            o2 = _dot(probs[p], vp) / norms[p]
            o = jnp.where(lane < NA_HEAD_DIM, o2[:GRID_W], o2[GRID_W:])
            o_ref[rows, cols] = (o * _silu(g_ref[rows, cols])).astype(BF16)


def _neighbourhood_attention(aq, ak, av, ag, bias):
    bsz, seq, w = aq.shape
    n_rows = seq // GRID_W
    tq = NA_ROWS_PER_STEP * GRID_W
    bias = bias.reshape(NA_BIAS_TYPES, NA_HEADS // 2, 2 * GRID_W, NA_KEYS)
    row = pl.BlockSpec((None, tq, w), lambda b, i: (b, i, 0))
    full = pl.BlockSpec((None, seq, w), lambda b, i: (b, 0, 0))
    return pl.pallas_call(
        functools.partial(_na_kernel, n_rows=n_rows),
        grid=(bsz, n_rows // NA_ROWS_PER_STEP),
        in_specs=[row, full, full, row,
                  pl.BlockSpec(bias.shape, lambda b, i: (0, 0, 0, 0))],
        out_specs=row,
        out_shape=jax.ShapeDtypeStruct((bsz, seq, w), BF16),
        compiler_params=_cparams("parallel", "arbitrary"),
        name="na_attn",
    )(aq, ak, av, ag, bias)


def _rec_consts(chunk, reverse):
    nlev = int(math.log2(chunk))
    assert 1 << nlev == chunk
    w = np.zeros(((nlev + 2), chunk, chunk), np.float32)
    lv = np.full((chunk, chunk), -1, np.int32)
    idx = np.arange(chunk)
    for l in range(nlev):
        s = 1 << l
        off = idx % (2 * s)
        mid = idx - off + s - 1
        for p in range(chunk):
            if off[p] >= s:
                w[l, p, mid[p] + 1:p + 1] = 1.0
            else:
                w[l, p, p + 1:mid[p] + 1] = 1.0
        same = (idx[:, None] // (2 * s)) == (idx[None, :] // (2 * s))
        lv[same & (off[:, None] >= s) & (off[None, :] < s)] = l
    lv[idx, idx] = nlev
    w[nlev] = (idx[None, :] <= idx[:, None])
    if reverse:
        w = w[:, ::-1, ::-1]
        lv = lv[::-1, ::-1]
    w = np.concatenate([w[nlev]] + [w[l] for l in REC_SEL_LEVELS], axis=0)
    return (jnp.asarray(w, BF16), jnp.asarray(np.ascontiguousarray(lv)), nlev)


def _rec_kernel(*refs, final, reverse, heads, dk, dv, nlev):
    chunk = REC_CHUNK
    q_ref, k_ref, lf_ref, v_ref, w_ref, lv_ref = refs[:6]
    if final:
        prev_ref, gate_ref, ng_ref, o_ref, st_ref = refs[6:]
    else:
        o_ref, st_ref = refs[6:]

    @pl.when(pl.program_id(1) == 0)
    def _():
        st_ref[...] = jnp.zeros_like(st_ref)

    lane = lax.broadcasted_iota(jnp.int32, (1, LANES), 1)
    row = lax.broadcasted_iota(jnp.int32, (chunk, LANES), 0)
    per_slab = LANES // dk
    n_slabs = heads * dk // LANES
    lv = lv_ref[...]
    level_masks = [lv == l for l in range(nlev + 1)]
    last = 0 if reverse else chunk - 1

    def is_query_side(l):
        off = row & (2 * (1 << l) - 1)
        return (off < (1 << l)) if reverse else (off >= (1 << l))

    query_side = [is_query_side(l) for l in range(nlev) if (1 << l) < SUBLANES]

    def boundary_exponent(b, l):
        s = 1 << l
        parts = []
        for blk in range(chunk // (2 * s)):
            mid = blk * 2 * s + (s if reverse else s - 1)
            b_mid = jnp.broadcast_to(b[mid:mid + 1, :], (s, LANES))
            lo, hi = b[blk * 2 * s:blk * 2 * s + s], b[blk * 2 * s + s:(blk + 1) * 2 * s]
            parts += [lo - b_mid, b_mid - hi] if reverse else [b_mid - lo, hi - b_mid]
        return jnp.concatenate(parts, axis=0)

    def query_or_key(q, k, l):
        if (1 << l) < SUBLANES:
            return jnp.where(query_side[l], q, k)
        s = 1 << l
        parts = []
        for blk in range(chunk // (2 * s)):
            lo, hi = slice(blk * 2 * s, blk * 2 * s + s), slice(blk * 2 * s + s, (blk + 1) * 2 * s)
            parts += [q[lo], k[hi]] if reverse else [k[lo], q[hi]]
        return jnp.concatenate(parts, axis=0)

    head_masks = [((lane >= hh * dk) & (lane < (hh + 1) * dk)) for hh in range(per_slab)]
    head_masks_bf = [hm.astype(BF16) for hm in head_masks]

    def stage_scale(ci):
        rows = slice(ci * chunk, (ci + 1) * chunk)
        lf_all = lf_ref[rows, :]
        sums = _sel_dot(w_ref[...], lf_all)
        slabs = []
        for s in range(n_slabs):
            cols = slice(s * LANES, (s + 1) * LANES)
            q = q_ref[rows, cols].astype(F32)
            k = k_ref[rows, cols].astype(F32)
            lf = lf_all[:, cols]
            b = sums[:chunk, cols]
            ys = []
            for l in range(nlev):
                if l == 0:
                    e = jnp.where(query_side[0], lf, 0.0)
                elif l in REC_SEL_LEVELS:
                    i0 = (1 + REC_SEL_LEVELS.index(l)) * chunk
                    e = sums[i0:i0 + chunk, cols]
                else:
                    e = boundary_exponent(b, l)
                ys.append((jnp.exp2(e) * query_or_key(q, k, l)).astype(BF16))
            eb = jnp.exp2(b)
            b_end = jnp.broadcast_to(b[last:last + 1, :], (chunk, LANES))
            slabs.append(dict(
                lhs=ys + [q.astype(BF16)], rhs=ys + [k.astype(BF16)],
                q_in=(eb * q).astype(BF16),
                k_up=(jnp.exp2(b_end - b) * k).astype(BF16),
                decay=eb[last:last + 1]))
        return rows, slabs

    lane2 = lax.broadcasted_iota(jnp.int32, (1, 2 * dv), 1)
    v_first, v_second = (lane2 < dv).astype(BF16), (lane2 >= dv).astype(BF16)
    pair_masks = [jnp.concatenate([m, m], axis=1) for m in level_masks]

    def block_diag(a, b_):
        za, zb = jnp.zeros_like(a), jnp.zeros_like(b_)
        return jnp.concatenate([jnp.concatenate([a, zb], axis=1), jnp.concatenate([za, b_], axis=1)], axis=0)

    def pair_operands(slabs, pr, key, l=None):
        pick = (lambda d: d[key][l]) if l is not None else (lambda d: d[key])
        if per_slab == 2:
            t = pick(slabs[pr])
            return t, jnp.concatenate([t * head_masks_bf[0], t * head_masks_bf[1]], axis=0)
        a, b_ = pick(slabs[2 * pr]), pick(slabs[2 * pr + 1])
        return jnp.concatenate([a, b_], axis=1), block_diag(a, b_)

    def stage_intra(slabs):
        mats = []
        for pr in range(heads // 2):
            a_mat = jnp.zeros((chunk, 2 * chunk), F32)
            for l in range(nlev + 1):
                lhs = pair_operands(slabs, pr, "lhs", l)[0]
                rhs = pair_operands(slabs, pr, "rhs", l)[1]
                a_mat = jnp.where(pair_masks[l], _dot_nt(lhs, rhs), a_mat)
            mats.append(a_mat.astype(BF16))
        return mats

    def stage_out(rows, slabs, mats):
        for pr in range(heads // 2):
            pcols = slice(pr * 2 * dv, (pr + 1) * 2 * dv)
            vp = v_ref[rows, pcols]
            v_bd = jnp.concatenate([vp * v_first, vp * v_second], axis=0)
            q_in = pair_operands(slabs, pr, "q_in")[0]
            k_up = pair_operands(slabs, pr, "k_up")[0]
            if per_slab == 2:
                st = st_ref[pr]
                st_bf = st.astype(BF16)
                st_bd = jnp.concatenate([st_bf * head_masks_bf[0], st_bf * head_masks_bf[1]], axis=0)
            else:
                st_a, st_b = st_ref[2 * pr], st_ref[2 * pr + 1]
                st_bd = block_diag(st_a.astype(BF16), st_b.astype(BF16))
            o = _dot(mats[pr], v_bd) + _dot_nt(q_in, st_bd)
            u = _dot_tn(vp, k_up)
            if per_slab == 2:
                st_ref[pr] = st * slabs[pr]["decay"] + jnp.where(head_masks[0], u[:dv], u[dv:])
            else:
                st_ref[2 * pr] = st_a * slabs[2 * pr]["decay"] + u[:dv, :LANES]
                st_ref[2 * pr + 1] = st_b * slabs[2 * pr + 1]["decay"] + u[dv:, LANES:]
            if final:
                for hh in range(2):
                    ocols = slice((2 * pr + hh) * dv, (2 * pr + hh + 1) * dv)
                    tot = prev_ref[rows, ocols] + o[:, hh * dv:(hh + 1) * dv]
                    ms = jnp.mean(tot * tot, axis=-1, keepdims=True)
                    y = tot * lax.rsqrt(ms + RMS_EPS) * ng_ref[...]
                    o_ref[rows, ocols] = (y * _silu(gate_ref[rows, ocols])).astype(o_ref.dtype)
            else:
                o_ref[rows, pcols] = o

    n_sub = q_ref.shape[0] // chunk
    order = list(range(n_sub - 1, -1, -1) if reverse else range(n_sub))
    scaled, intra = {}, {}
    for t in range(n_sub + 2):
        if t < n_sub:
            scaled[t] = stage_scale(order[t])
        if 0 <= t - 1 < n_sub:
            intra[t - 1] = stage_intra(scaled[t - 1][1])
        if 0 <= t - 2 < n_sub:
            stage_out(*scaled.pop(t - 2), intra.pop(t - 2))


def _gated_recurrence(name, q, k, lf, v, prev, gate, norm_g, *, reverse, heads, dk, dv):
    bsz, seq, _ = q.shape
    chunk = REC_CHUNK
    blk_rows = min(REC_CHUNKS_PER_STEP * chunk, seq)
    n = seq // blk_rows
    final = prev is not None
    w_sel, lv, nlev = _rec_consts(chunk, reverse)
    cidx = (lambda b, i: (b, n - 1 - i, 0)) if reverse else (lambda b, i: (b, i, 0))
    row = lambda width: pl.BlockSpec((None, blk_rows, width), cidx)
    const2 = lambda a: pl.BlockSpec(a.shape, lambda b, i: (0, 0))
    in_specs = [row(heads * dk), row(heads * dk), row(heads * dk), row(heads * dv), const2(w_sel), const2(lv)]
    args = [q, k, lf, v, w_sel, lv]
    if final:
        in_specs += [row(heads * dv), row(heads * dv), pl.BlockSpec((1, dv), lambda b, i: (0, 0))]
        args += [prev, gate, norm_g.reshape(1, dv)]
    return pl.pallas_call(
        functools.partial(_rec_kernel, final=final, reverse=reverse, heads=heads, dk=dk, dv=dv, nlev=nlev),
        grid=(bsz, n),
        in_specs=in_specs,
        out_specs=row(heads * dv),
        out_shape=jax.ShapeDtypeStruct((bsz, seq, heads * dv), BF16 if final else F32),
        scratch_shapes=[pltpu.VMEM((heads * dk // LANES, dv, LANES), F32)],
        compiler_params=_cparams("parallel", "arbitrary"),
        name=f"{name}_{'bwd' if reverse else 'fwd'}",
    )(*args)


def _ssd_consts(chunk, reverse):
    idx = np.arange(chunk)
    if reverse:
        tri = idx[None, :] >= idx[:, None]
        rest = idx[None, :] < idx[:, None]
    else:
        tri = idx[None, :] <= idx[:, None]
        rest = idx[None, :] > idx[:, None]
    sel = np.concatenate([tri, rest], axis=0).astype(np.float32)
    expand = np.zeros((LANES, WIDTH_D), np.float32)
    for h in range(SSD_HEADS):
        expand[h + (SSD_HEADS if reverse else 0), h * SSD_HEAD_DIM:(h + 1) * SSD_HEAD_DIM] = 1.0
    return jnp.asarray(sel, BF16), jnp.asarray(expand, BF16)


def _ssd_kernel(*refs, final, reverse, n_blocks):
    chunk = REC_CHUNK
    if final:
        (xs_ref, bc_ref, dt_ref, dtb_ref, alog_ref, sel_ref, exp_ref,
         prev_ref, dz_ref, dskip_ref, ng_ref, o_ref, st_ref) = refs
    else:
        (x_ref, xprev_ref, xnext_ref, dt_ref, cw_ref, cb_ref, dtb_ref, alog_ref, sel_ref, exp_ref,
         o_ref, xs_ref, bc_ref, ext_ref, st_ref) = refs
    step = pl.program_id(1)
    blk = (n_blocks - 1 - step) if reverse else step
    blk_rows = dt_ref.shape[0]

    @pl.when(step == 0)
    def _():
        st_ref[...] = jnp.zeros_like(st_ref)

    if not final:
        halo = SUBLANES
        ext_ref[0:halo] = jnp.where(blk == 0, 0.0, xprev_ref[...])
        ext_ref[halo:halo + blk_rows] = x_ref[...]
        ext_ref[halo + blk_rows:2 * halo + blk_rows] = jnp.where(blk == n_blocks - 1, 0.0, xnext_ref[...])
        acc = cb_ref[...]
        for tap in range(SSD_CONV):
            o0 = halo - SSD_CONV // 2 + tap
            acc = acc + ext_ref[o0:o0 + blk_rows] * cw_ref[tap:tap + 1]
        xbc = _silu(acc)
        xs_ref[...] = xbc[:, :WIDTH_D]
        bc_ref[...] = xbc[:, WIDTH_D:].astype(BF16)

    gs = SSD_GROUPS * SSD_STATE
    last = 0 if reverse else chunk - 1
    ii = lax.broadcasted_iota(jnp.int32, (chunk, chunk), 0)
    jj = lax.broadcasted_iota(jnp.int32, (chunk, chunk), 1)
    causal = (ii <= jj) if reverse else (ii >= jj)
    lane = lax.broadcasted_iota(jnp.int32, (1, LANES), 1)
    lo_half = lane < SSD_HEAD_DIM
    col0 = SSD_HEADS if reverse else 0
    rep = SSD_HEADS // SSD_GROUPS
    n_pairs = SSD_HEADS // 2
    neg_a = -jnp.exp(alog_ref[...])

    def stage_decay(ci):
        rows = slice(ci * chunk, (ci + 1) * chunk)
        xs = xs_ref[rows, :]
        bm, cm = bc_ref[rows, :gs], bc_ref[rows, gs:]
        dt = _softplus(dt_ref[rows, :] + dtb_ref[...])
        sums = _sel_dot(sel_ref[...], dt * neg_a)
        a_cum, a_rem = sums[:chunk], sums[chunk:]
        a_cum_t = a_cum.T
        expand = exp_ref[...]
        x_dt = xs * _dot_sel(dt, expand)
        e_cum = jnp.exp(_dot_sel(a_cum, expand))
        x_rem = (x_dt * jnp.exp(_dot_sel(a_rem, expand))).astype(BF16)
        cbs = [_dot_nt(cm[:, g * SSD_STATE:(g + 1) * SSD_STATE], bm[:, g * SSD_STATE:(g + 1) * SSD_STATE])
               for g in range(SSD_GROUPS)]
        mats, stacks = [], []
        for p in range(n_pairs):
            pair = []
            for hh in range(2):
                ch = col0 + 2 * p + hh
                diff = a_cum[:, ch:ch + 1] - a_cum_t[ch:ch + 1, :]
                seg = jnp.exp(jnp.where(causal, diff, NEG_INF))
                pair.append((cbs[(2 * p) // rep] * seg).astype(BF16))
            mats.append(jnp.concatenate(pair, axis=1))
            xp = x_dt[:, p * LANES:(p + 1) * LANES]
            stacks.append(jnp.concatenate([jnp.where(lo_half, xp, 0.0), jnp.where(lo_half, 0.0, xp)],
                                          axis=0).astype(BF16))
        return dict(rows=rows, xs=xs, bm=bm, cm=cm, e_cum=e_cum, x_rem=x_rem, mats=mats, stacks=stacks)

    def stage_out(d):
        rows = d["rows"]
        ys = []
        for p in range(n_pairs):
            g = (2 * p) // rep
            gcols = slice(g * SSD_STATE, (g + 1) * SSD_STATE)
            pcols = slice(p * LANES, (p + 1) * LANES)
            st = st_ref[p]
            y = _dot(d["mats"][p], d["stacks"][p]) + d["e_cum"][:, pcols] * _dot(d["cm"][:, gcols], st.astype(BF16))
            upd = _dot_tn(d["bm"][:, gcols], d["x_rem"][:, pcols])
            st_ref[p] = st * d["e_cum"][last:last + 1, pcols] + upd
            if final:
                y = y + prev_ref[rows, pcols] + dskip_ref[:, pcols] * d["xs"][:, pcols]
                ys.append(y * _silu(dz_ref[rows, pcols]))
            else:
                o_ref[rows, pcols] = y
        if final:
            t = jnp.concatenate(ys, axis=1)
            ms = jnp.mean(t * t, axis=-1, keepdims=True)
            o_ref[rows, :] = (t * lax.rsqrt(ms + RMS_EPS) * ng_ref[...]).astype(o_ref.dtype)

    n_sub = blk_rows // chunk
    order = list(range(n_sub - 1, -1, -1) if reverse else range(n_sub))
    pending = None
    for ci in order:
        cur = stage_decay(ci)
        if pending is not None:
            stage_out(pending)
        pending = cur
    stage_out(pending)


def _ssd(dxbc, dt_raw, conv_w, conv_b, dt_bias, a_log, dz, d_skip, norm_g):
    bsz, seq, nch = dxbc.shape
    chunk = REC_CHUNK
    blk_rows = min(REC_CHUNKS_PER_STEP * chunk, seq)
    n = seq // blk_rows
    hb = blk_rows // SUBLANES
    nb = seq // SUBLANES
    const2 = lambda a: pl.BlockSpec(a.shape, lambda b, i: (0, 0))
    dtb = jnp.zeros((1, LANES), F32).at[0, :2 * SSD_HEADS].set(dt_bias.reshape(-1))
    alog = jnp.full((1, LANES), NEG_INF, F32).at[0, :2 * SSD_HEADS].set(a_log.reshape(-1))
    cb2 = conv_b.reshape(1, nch)
    state = pltpu.VMEM((SSD_HEADS // 2, SSD_STATE, LANES), F32)
    shp = lambda width, dt: jax.ShapeDtypeStruct((bsz, seq, width), dt)

    sel, expand = _ssd_consts(chunk, False)
    row = lambda width: pl.BlockSpec((None, blk_rows, width), lambda b, i: (b, i, 0))
    y_fwd, xs, bc = pl.pallas_call(
        functools.partial(_ssd_kernel, final=False, reverse=False, n_blocks=n),
        grid=(bsz, n),
        in_specs=[row(nch),
                  pl.BlockSpec((None, SUBLANES, nch), lambda b, i: (b, jnp.maximum(i * hb - 1, 0), 0)),
                  pl.BlockSpec((None, SUBLANES, nch), lambda b, i: (b, jnp.minimum((i + 1) * hb, nb - 1), 0)),
                  row(LANES), const2(conv_w), const2(cb2), const2(dtb), const2(alog), const2(sel), const2(expand)],
        out_specs=[row(WIDTH_D), row(WIDTH_D), row(nch - WIDTH_D)],
        out_shape=[shp(WIDTH_D, F32), shp(WIDTH_D, F32), shp(nch - WIDTH_D, BF16)],
        scratch_shapes=[pltpu.VMEM((blk_rows + 2 * SUBLANES, nch), F32), state],
        compiler_params=_cparams("parallel", "arbitrary"),
        name="ssd_fwd",
    )(dxbc, dxbc, dxbc, dt_raw, conv_w, cb2, dtb, alog, sel, expand)

    sel, expand = _ssd_consts(chunk, True)
    row = lambda width: pl.BlockSpec((None, blk_rows, width), lambda b, i: (b, n - 1 - i, 0))
    dsk = jnp.repeat(d_skip, SSD_HEAD_DIM).reshape(1, WIDTH_D)
    ng = norm_g.reshape(1, WIDTH_D)
    return pl.pallas_call(
        functools.partial(_ssd_kernel, final=True, reverse=True, n_blocks=n),
        grid=(bsz, n),
        in_specs=[row(WIDTH_D), row(nch - WIDTH_D), row(LANES), const2(dtb), const2(alog), const2(sel), const2(expand),
                  row(WIDTH_D), row(WIDTH_D), const2(dsk), const2(ng)],
        out_specs=row(WIDTH_D),
        out_shape=shp(WIDTH_D, BF16),
        scratch_shapes=[state],
        compiler_params=_cparams("parallel", "arbitrary"),
        name="ssd_bwd",
    )(xs, bc, dt_raw, dtb, alog, sel, expand, y_fwd, dz, dsk, ng)


def kernel(x, c, ada_w, ada_b, ln_g, ln_b, e_w_in, e_rpb, e_gla_w_up, e_gla_b, e_gla_norm_g, e_w_out,
           o_w_in, hgrn_lb, o_hgrn_norm_g, o_conv_w, o_conv_b, o_dt_bias, o_a_log, o_d_skip,
           o_ssm_norm_g, o_w_out):
    mod = _ada_mod(c, ada_w, ada_b)

    aq, ak, av, ag, bq, bk, bv, bg, lf_f, lf_b = _proj_even(x, mod[0], e_w_in[0], e_gla_w_up[0], e_gla_b[0])
    ya = _neighbourhood_attention(aq, ak, av, ag, _na_bias(e_rpb[0]))
    gla = functools.partial(_gated_recurrence, "gla", heads=GLA_HEADS, dk=GLA_DK, dv=GLA_DV)
    o_fwd = gla(bq, bk, lf_f, bv, None, None, None, reverse=False)
    yb = gla(bq, bk, lf_b, bv, o_fwd, bg, e_gla_norm_g[0], reverse=True)
    x = _out_ln(ya, yb, e_w_out[0], x, mod[0], ln_g[0], ln_b[0])

    cq, ck_f, lf_f, ck_b, lf_b, ci, cg, dz, dxbc, dt_raw = _proj_odd(x, mod[1], o_w_in[0], hgrn_lb)
    hgrn = functools.partial(_gated_recurrence, "hgrn", heads=HGRN_HEADS, dk=HGRN_DIM, dv=HGRN_DIM)
    o_fwd = hgrn(cq, ck_f, lf_f, ci, None, None, None, reverse=False)
    yc = hgrn(cq, ck_b, lf_b, ci, o_fwd, cg, o_hgrn_norm_g[0], reverse=True)
    yd = _ssd(dxbc, dt_raw, o_conv_w[0], o_conv_b[0], o_dt_bias[0], o_a_log[0], dz, o_d_skip[0], o_ssm_norm_g[0])
    return _out_ln(yc, yd, o_w_out[0], x, mod[1], ln_g[1], ln_b[1])
```

```python
import functools
import math

import numpy as np
import jax
import jax.numpy as jnp
from jax import lax
from jax.experimental import pallas as pl
from jax.experimental.pallas import tpu as pltpu

F32 = jnp.float32
BF16 = jnp.bfloat16

GRID_W = 64
NA_HEADS, NA_HEAD_DIM = 8, 64
NA_WIN_ROWS, NA_WIN_COLS = 8, 16
WIDTH_A = NA_HEADS * NA_HEAD_DIM
GLA_HEADS, GLA_DK, GLA_DV, GLA_RANK = 4, 64, 128, 16
GLA_GATE_NORM = 16.0
WIDTH_B = GLA_HEADS * GLA_DV
HGRN_HEADS, HGRN_DIM = 4, 128
WIDTH_C = HGRN_HEADS * HGRN_DIM
SSD_HEADS, SSD_HEAD_DIM, SSD_GROUPS, SSD_STATE, SSD_CONV = 8, 64, 2, 128, 4
WIDTH_D = SSD_HEADS * SSD_HEAD_DIM
SSD_CONV_CH = WIDTH_D + 2 * SSD_GROUPS * SSD_STATE
DEPTH = 2
DEEPNORM_ALPHA = (2 * DEPTH) ** 0.25
LN_EPS = 1e-5
RMS_EPS = 1e-6

LANES = 128
SUBLANES = 8
VMEM_LIMIT = 56 * 1024 * 1024

ROW_TILE = 512
OUT_ROW_TILE = 1024
REC_CHUNK = 128
REC_CHUNKS_PER_STEP = 8
REC_SEL_LEVELS = (1, 2)
NA_ROWS_PER_STEP = 8
NEG_INF = float("-inf")
LOG2_E = math.log2(math.e)


def _cparams(*sem):
    return pltpu.CompilerParams(dimension_semantics=sem, vmem_limit_bytes=VMEM_LIMIT)


def _dot(a, b):
    return jnp.dot(a, b, preferred_element_type=F32)


def _dot_nt(a, b):
    return lax.dot_general(a, b, (((1,), (1,)), ((), ())), preferred_element_type=F32)


def _dot_tn(a, b):
    return lax.dot_general(a, b, (((0,), (0,)), ((), ())), preferred_element_type=F32)


def _split3(x):
    hi = x.astype(BF16)
    r = x - hi.astype(F32)
    mid = r.astype(BF16)
    lo = (r - mid.astype(F32)).astype(BF16)
    return hi, mid, lo


def _sel_dot(sel, x):
    hi, mid, lo = _split3(x)
    return _dot(sel, hi) + _dot(sel, mid) + _dot(sel, lo)


def _sel_dot2(sel, x):
    hi = x.astype(BF16)
    lo = (x - hi.astype(F32)).astype(BF16)
    return _dot(sel, hi) + _dot(sel, lo)


def _dot_sel(x, sel):
    hi, mid, lo = _split3(x)
    return _dot(hi, sel) + _dot(mid, sel) + _dot(lo, sel)


def _sigmoid(x):
    return 1.0 / (1.0 + jnp.exp(-x))


def _silu(x):
    return x * _sigmoid(x)


def _log1pexp_neg_abs(x):
    return jnp.log1p(jnp.exp(-jnp.abs(x)))


def _softplus(x):
    return jnp.maximum(x, 0.0) + _log1pexp_neg_abs(x)


def _ada_kernel(c_ref, w_ref, b_ref, o_ref):
    cond = _silu(c_ref[...]).astype(BF16)
    o_ref[...] = _dot(cond, w_ref[...].astype(BF16)) + b_ref[...]


def _ada_mod(c, ada_w, ada_b):
    bsz, d = c.shape
    bp = -(-bsz // SUBLANES) * SUBLANES
    n3 = ada_w.shape[-1]
    tn = ROW_TILE
    c_pad = jnp.zeros((bp, d), F32).at[:bsz].set(c)
    out = pl.pallas_call(
        _ada_kernel,
        grid=(DEPTH, n3 // tn),
        in_specs=[pl.BlockSpec((bp, d), lambda l, j: (0, 0)),
                  pl.BlockSpec((None, d, tn), lambda l, j: (l, 0, j)),
                  pl.BlockSpec((None, 1, tn), lambda l, j: (l, 0, j))],
        out_specs=pl.BlockSpec((None, bp, tn), lambda l, j: (l, 0, j)),
        out_shape=jax.ShapeDtypeStruct((DEPTH, bp, n3), F32),
        compiler_params=_cparams("arbitrary", "arbitrary"),
        name="ada_mod",
    )(c_pad, ada_w, ada_b.reshape(DEPTH, 1, n3))
    return out[:, :bsz].reshape(DEPTH, bsz, 1, n3)


def _modulated(x_ref, mod_ref):
    d = x_ref.shape[-1]
    mod = mod_ref[...]
    return (x_ref[...] * (1.0 + mod[:, d:2 * d]) + mod[:, :d]).astype(BF16)


def _stage_weights(w_ref, wbf_ref):
    d, n = w_ref.shape
    n_full = n // LANES * LANES
    rows_per_iter = LANES

    @pl.when((pl.program_id(0) == 0) & (pl.program_id(1) == 0))
    def _():
        def body(r, carry):
            rows = pl.ds(pl.multiple_of(r * rows_per_iter, rows_per_iter), rows_per_iter)
            wbf_ref[rows, :n_full] = w_ref[rows, :n_full].astype(BF16)
            if n_full < wbf_ref.shape[1]:
                wbf_ref[rows, n_full:] = jnp.zeros((rows_per_iter, wbf_ref.shape[1] - n_full), BF16)
                wbf_ref[rows, n_full:n] = w_ref[rows, n_full:n].astype(BF16)
            return carry
        lax.fori_loop(0, d // rows_per_iter, body, 0)


def _proj_even_kernel(x_ref, mod_ref, wf_ref, wup_ref, gb_ref,
                      aq_ref, ak_ref, av_ref, ag_ref, bq_ref, bk_ref, bv_ref, bg_ref, lff_ref, lfb_ref, h_ref, w_ref):
    _stage_weights(wf_ref, w_ref)
    h_ref[...] = _modulated(x_ref, mod_ref)

    def mm(lo, hi):
        return _dot(h_ref[...], w_ref[:, lo:hi])

    a, kb = WIDTH_A, GLA_HEADS * GLA_DK
    o = 4 * a + 2 * kb + 2 * WIDTH_B
    lr = mm(o, o + LANES)
    lr_hi = lr.astype(BF16)
    lr_lo = (lr - lr_hi.astype(F32)).astype(BF16)
    for d, out_ref in enumerate((lff_ref, lfb_ref)):
        wu = wup_ref[d]
        wu_hi = wu.astype(BF16)
        wu_lo = (wu - wu_hi.astype(F32)).astype(BF16)
        out_ref[...] = _dot(lr_hi, wu_hi) + _dot(lr_lo, wu_hi) + _dot(lr_hi, wu_lo) + gb_ref[d]

    def finish_gate(out_ref):
        z = out_ref[...]
        log_sig = jnp.minimum(z, 0.0) - jnp.log(1.0 + jnp.exp2(jnp.abs(z) * (-LOG2_E)))
        out_ref[...] = log_sig * (LOG2_E / GLA_GATE_NORM)

    aq_ref[...] = (mm(0, a) * (NA_HEAD_DIM ** -0.5)).astype(BF16)
    ak_ref[...] = mm(a, 2 * a).astype(BF16)
    finish_gate(lff_ref)
    av_ref[...] = mm(2 * a, 3 * a).astype(BF16)
    ag_ref[...] = mm(3 * a, 4 * a)
    finish_gate(lfb_ref)
    o = 4 * a
    bq_ref[...] = (mm(o, o + kb) * (GLA_DK ** -0.5)).astype(BF16)
    bk_ref[...] = mm(o + kb, o + 2 * kb).astype(BF16)
    o += 2 * kb
    bv_ref[...] = mm(o, o + WIDTH_B).astype(BF16)
    bg_ref[...] = mm(o + WIDTH_B, o + 2 * WIDTH_B)


def _proj_even(x, mod, w_in, gla_w_up, gla_b):
    bsz, seq, d = x.shape
    kb = GLA_HEADS * GLA_DK
    n_main = 4 * WIDTH_A + 2 * kb + 2 * WIDTH_B
    wup = jnp.zeros((2, LANES, kb), F32)
    wup = wup.at[0, :GLA_RANK].set(gla_w_up[0]).at[1, GLA_RANK:2 * GLA_RANK].set(gla_w_up[1])
    tm = min(ROW_TILE, seq)
    row = lambda n: pl.BlockSpec((None, tm, n), lambda b, i: (b, i, 0))
    shp = lambda n, dt: jax.ShapeDtypeStruct((bsz, seq, n), dt)
    widths = [(WIDTH_A, BF16), (WIDTH_A, BF16), (WIDTH_A, BF16), (WIDTH_A, F32),
              (kb, BF16), (kb, BF16), (WIDTH_B, BF16), (WIDTH_B, F32), (kb, F32), (kb, F32)]
    return pl.pallas_call(
        _proj_even_kernel,
        grid=(bsz, seq // tm),
        in_specs=[row(d),
                  pl.BlockSpec((None, 1, mod.shape[-1]), lambda b, i: (b, 0, 0)),
                  pl.BlockSpec(w_in.shape, lambda b, i: (0, 0), pipeline_mode=pl.Buffered(1)),
                  pl.BlockSpec(wup.shape, lambda b, i: (0, 0, 0)),
                  pl.BlockSpec((2, 1, kb), lambda b, i: (0, 0, 0))],
        out_specs=[row(n) for n, _ in widths],
        out_shape=[shp(n, dt) for n, dt in widths],
        scratch_shapes=[pltpu.VMEM((tm, d), BF16), pltpu.VMEM((d, n_main + LANES), BF16)],
        compiler_params=_cparams("arbitrary", "arbitrary"),
        name="proj_even",
    )(x, mod, w_in, wup, gla_b.reshape(2, 1, kb))


def _proj_odd_kernel(x_ref, mod_ref, wf_ref, lbraw_ref,
                     cq_ref, kf_ref, lff_ref, kb_ref, lfb_ref, ci_ref, cg_ref, dz_ref, dxbc_ref, dt_ref, h_ref, w_ref):
    _stage_weights(wf_ref, w_ref)
    h_ref[...] = _modulated(x_ref, mod_ref)

    def mm(lo, hi):
        return _dot(h_ref[...], w_ref[:, lo:hi])

    c = WIDTH_C
    lbr = lbraw_ref[...]
    mx = jnp.maximum(lbr[0:1], lbr[1:2])
    e0, e1 = jnp.exp(lbr[0:1] - mx), jnp.exp(lbr[1:2] - mx)
    lb = e1 / (e0 + e1)
    log_lb, log_ub = jnp.log(lb), jnp.log1p(-lb)
    gate_refs = ((kf_ref, lff_ref), (kb_ref, lfb_ref))
    for d, (_, lf_ref) in enumerate(gate_refs):
        lf_ref[...] = mm((1 + d) * c, (2 + d) * c)

    half = x_ref.shape[0] // 2

    def finish_gate(d, part):
        k_ref, lf_ref = gate_refs[d]
        rows = slice(part * half, (part + 1) * half)
        z = lf_ref[rows, :]
        t = jnp.exp2(jnp.abs(z) * (-LOG2_E))
        u = 1.0 + t
        k_ref[rows, :] = (1.0 - lb) * (jnp.where(z > 0.0, t, 1.0) / u)
        lc = log_ub + (jnp.minimum(z, 0.0) - jnp.log(u))
        w = 1.0 + jnp.exp2(jnp.abs(log_lb - lc) * (-LOG2_E))
        lf_ref[rows, :] = (jnp.maximum(log_lb, lc) + jnp.log(w)) * LOG2_E

    cq_ref[...] = (mm(0, c) * (HGRN_DIM ** -0.5)).astype(BF16)
    finish_gate(0, 0)
    ci_ref[...] = mm(3 * c, 4 * c).astype(BF16)
    finish_gate(0, 1)
    cg_ref[...] = mm(4 * c, 5 * c)
    finish_gate(1, 0)
    o = 5 * c
    dz_ref[...] = mm(o, o + WIDTH_D)
    finish_gate(1, 1)
    o += WIDTH_D
    dxbc_ref[...] = mm(o, o + SSD_CONV_CH)
    o += SSD_CONV_CH
    dt_ref[...] = mm(o, o + LANES)


def _proj_odd(x, mod, w_in, hgrn_lb):
    bsz, seq, d = x.shape
    n_main = 5 * WIDTH_C + WIDTH_D + SSD_CONV_CH
    tm = min(ROW_TILE, seq)
    row = lambda n: pl.BlockSpec((None, tm, n), lambda b, i: (b, i, 0))
    shp = lambda n, dt: jax.ShapeDtypeStruct((bsz, seq, n), dt)
    widths = [(WIDTH_C, BF16), (WIDTH_C, F32), (WIDTH_C, F32), (WIDTH_C, F32), (WIDTH_C, F32), (WIDTH_C, BF16),
              (WIDTH_C, F32), (WIDTH_D, F32), (SSD_CONV_CH, F32), (LANES, F32)]
    return pl.pallas_call(
        _proj_odd_kernel,
        grid=(bsz, seq // tm),
        in_specs=[row(d),
                  pl.BlockSpec((None, 1, mod.shape[-1]), lambda b, i: (b, 0, 0)),
                  pl.BlockSpec(w_in.shape, lambda b, i: (0, 0), pipeline_mode=pl.Buffered(1)),
                  pl.BlockSpec(hgrn_lb.shape, lambda b, i: (0, 0))],
        out_specs=[row(n) for n, _ in widths],
        out_shape=[shp(n, dt) for n, dt in widths],
        scratch_shapes=[pltpu.VMEM((tm, d), BF16), pltpu.VMEM((d, n_main + LANES), BF16)],
        compiler_params=_cparams("arbitrary", "arbitrary"),
        name="proj_odd",
    )(x, mod, w_in, hgrn_lb)


def _out_ln_kernel(ya_ref, yb_ref, w_ref, x_ref, mod_ref, g_ref, b_ref, o_ref):
    d = x_ref.shape[-1]
    wa = ya_ref.shape[-1]
    y = _dot(ya_ref[...], w_ref[:wa, :]) + _dot(yb_ref[...], w_ref[wa:, :])
    gate = mod_ref[...][:, 2 * d:]
    t = DEEPNORM_ALPHA * x_ref[...] + gate * y
    mu = jnp.mean(t, axis=-1, keepdims=True)
    tc = t - mu
    var = jnp.mean(tc * tc, axis=-1, keepdims=True)
    o_ref[...] = tc * lax.rsqrt(var + LN_EPS) * g_ref[...] + b_ref[...]


def _out_ln(ya, yb, w_out, x, mod, ln_g, ln_b):
    bsz, seq, d = x.shape
    tm = min(OUT_ROW_TILE, seq)
    w = w_out.astype(BF16)
    row = lambda n: pl.BlockSpec((None, tm, n), lambda b, i: (b, i, 0))
    vec = pl.BlockSpec((1, d), lambda b, i: (0, 0))
    return pl.pallas_call(
        _out_ln_kernel,
        grid=(bsz, seq // tm),
        in_specs=[row(ya.shape[-1]), row(yb.shape[-1]),
                  pl.BlockSpec(w.shape, lambda b, i: (0, 0)),
                  row(d),
                  pl.BlockSpec((None, 1, mod.shape[-1]), lambda b, i: (b, 0, 0)),
                  vec, vec],
        out_specs=row(d),
        out_shape=jax.ShapeDtypeStruct((bsz, seq, d), F32),
        compiler_params=_cparams("parallel", "arbitrary"),
        name="out_ln",
    )(ya, yb, w, x, mod, ln_g.reshape(1, d), ln_b.reshape(1, d))


NA_BIAS_TYPES = NA_WIN_ROWS
NA_KEYS = NA_WIN_ROWS * GRID_W


def _na_bias_kernel(rpb_ref, o_ref):
    h = pl.program_id(0)
    shape = (GRID_W, LANES)
    q = lax.broadcasted_iota(jnp.int32, shape, 0)
    lane = lax.broadcasted_iota(jnp.int32, shape, 1)
    kc = lane & (GRID_W - 1)
    dc = kc - q + (NA_WIN_COLS - 1)
    cs = jnp.clip(q - NA_WIN_COLS // 2, 0, GRID_W - NA_WIN_COLS)
    valid = (kc >= cs) & (kc < cs + NA_WIN_COLS)
    upper = lane >= GRID_W
    n_dr, n_dc = 2 * NA_WIN_ROWS - 1, 2 * NA_WIN_COLS - 1
    tiles = []
    for dr in range(n_dr):
        acc = jnp.zeros(shape, F32)
        for d in range(n_dc):
            acc = jnp.where(dc == d, rpb_ref[(h * n_dr + dr) * n_dc + d], acc)
        tiles.append(jnp.where(valid, acc, NEG_INF))
    for t in range(NA_BIAS_TYPES):
        for c in range(NA_KEYS // LANES):
            dr0 = 2 * c + NA_WIN_ROWS - 1 - t
            o_ref[t, :, c * LANES:(c + 1) * LANES] = jnp.where(upper, tiles[dr0 + 1], tiles[dr0])


def _na_bias(rpb):
    return pl.pallas_call(
        _na_bias_kernel,
        grid=(NA_HEADS,),
        in_specs=[pl.BlockSpec(memory_space=pltpu.SMEM)],
        out_specs=pl.BlockSpec((NA_BIAS_TYPES, None, GRID_W, NA_KEYS), lambda h: (0, h, 0, 0)),
        out_shape=jax.ShapeDtypeStruct((NA_BIAS_TYPES, NA_HEADS, GRID_W, NA_KEYS), F32),
        compiler_params=_cparams("arbitrary"),
        name="na_bias",
    )(rpb.reshape(-1))


def _na_kernel(q_ref, k_ref, v_ref, g_ref, bias_ref, o_ref, *, n_rows):
    i = pl.program_id(1)
    lane = lax.broadcasted_iota(jnp.int32, (1, LANES), 1)
    head_mask = [(lane < NA_HEAD_DIM).astype(BF16), (lane >= NA_HEAD_DIM).astype(BF16)]
    half = NA_WIN_ROWS // 2
    pair_cols = [slice(p * LANES, (p + 1) * LANES) for p in range(WIDTH_A // LANES)]

    def stage_scores(rr):
        r = i * NA_ROWS_PER_STEP + rr
        row_start = jnp.clip(r - half, 0, n_rows - NA_WIN_ROWS)
        t = jnp.where(r < half, r, jnp.where(r > n_rows - half, r - (n_rows - NA_WIN_ROWS), half))
        koff = pl.multiple_of(row_start * GRID_W, GRID_W)
        rows = slice(rr * GRID_W, (rr + 1) * GRID_W)
        scores = []
        for p, cols in enumerate(pair_cols):
            qp = q_ref[rows, cols]
            qs = jnp.concatenate([qp * head_mask[0], qp * head_mask[1]], axis=0)
            kp = k_ref[pl.ds(koff, NA_KEYS), cols]
            scores.append(_dot_nt(qs, kp) + bias_ref[t, p])
        return rows, koff, scores

    def stage_out(rows, koff, scores):
        probs, norms = [], []
        for s in scores:
            e = jnp.exp(s - jnp.max(s, axis=-1, keepdims=True))
            norms.append(jnp.sum(e, axis=-1, keepdims=True))
            probs.append(e.astype(BF16))
        for p, cols in enumerate(pair_cols):
            vp = v_ref[pl.ds(koff, NA_KEYS), cols]
            o2 = _dot(probs[p], vp) / norms[p]
            o = jnp.where(lane < NA_HEAD_DIM, o2[:GRID_W], o2[GRID_W:])
            o_ref[rows, cols] = (o * _silu(g_ref[rows, cols])).astype(BF16)

    pending = None
    for rr in range(NA_ROWS_PER_STEP):
        cur = stage_scores(rr)
        if pending is not None:
            stage_out(*pending)
        pending = cur
    stage_out(*pending)


def _neighbourhood_attention(aq, ak, av, ag, bias):
    bsz, seq, w = aq.shape
    n_rows = seq // GRID_W
    tq = NA_ROWS_PER_STEP * GRID_W
    bias = bias.reshape(NA_BIAS_TYPES, NA_HEADS // 2, 2 * GRID_W, NA_KEYS)
    row = pl.BlockSpec((None, tq, w), lambda b, i: (b, i, 0))
    full = pl.BlockSpec((None, seq, w), lambda b, i: (b, 0, 0))
    return pl.pallas_call(
        functools.partial(_na_kernel, n_rows=n_rows),
        grid=(bsz, n_rows // NA_ROWS_PER_STEP),
        in_specs=[row, full, full, row,
                  pl.BlockSpec(bias.shape, lambda b, i: (0, 0, 0, 0))],
        out_specs=row,
        out_shape=jax.ShapeDtypeStruct((bsz, seq, w), BF16),
        compiler_params=_cparams("parallel", "arbitrary"),
        name="na_attn",
    )(aq, ak, av, ag, bias)


def _rec_consts(chunk, reverse):
    nlev = int(math.log2(chunk))
    assert 1 << nlev == chunk
    w = np.zeros(((nlev + 2), chunk, chunk), np.float32)
    lv = np.full((chunk, chunk), -1, np.int32)
    idx = np.arange(chunk)
    for l in range(nlev):
        s = 1 << l
        off = idx % (2 * s)
        mid = idx - off + s - 1
        for p in range(chunk):
            if off[p] >= s:
                w[l, p, mid[p] + 1:p + 1] = 1.0
            else:
                w[l, p, p + 1:mid[p] + 1] = 1.0
        same = (idx[:, None] // (2 * s)) == (idx[None, :] // (2 * s))
        lv[same & (off[:, None] >= s) & (off[None, :] < s)] = l
    lv[idx, idx] = nlev
    w[nlev] = (idx[None, :] <= idx[:, None])
    if reverse:
        w = w[:, ::-1, ::-1]
        lv = lv[::-1, ::-1]
    w = np.concatenate([w[nlev]] + [w[l] for l in REC_SEL_LEVELS], axis=0)
    return (jnp.asarray(w, BF16), jnp.asarray(np.ascontiguousarray(lv)), nlev)


def _rec_kernel(*refs, final, reverse, heads, dk, dv, nlev):
    chunk = REC_CHUNK
    q_ref, k_ref, lf_ref, v_ref, w_ref, lv_ref = refs[:6]
    if final:
        prev_ref, gate_ref, ng_ref, o_ref, st_ref = refs[6:]
    else:
        o_ref, st_ref = refs[6:]

    @pl.when(pl.program_id(1) == 0)
    def _():
        st_ref[...] = jnp.zeros_like(st_ref)

    lane = lax.broadcasted_iota(jnp.int32, (1, LANES), 1)
    row = lax.broadcasted_iota(jnp.int32, (chunk, LANES), 0)
    per_slab = LANES // dk
    n_slabs = heads * dk // LANES
    lv = lv_ref[...]
    level_masks = [lv == l for l in range(nlev + 1)]
    last = 0 if reverse else chunk - 1

    def is_query_side(l):
        off = row & (2 * (1 << l) - 1)
        return (off < (1 << l)) if reverse else (off >= (1 << l))

    query_side = [is_query_side(l) for l in range(nlev) if (1 << l) < SUBLANES]

    def boundary_exponent(b, l):
        s = 1 << l
        parts = []
        for blk in range(chunk // (2 * s)):
            mid = blk * 2 * s + (s if reverse else s - 1)
            b_mid = jnp.broadcast_to(b[mid:mid + 1, :], (s, LANES))
            lo, hi = b[blk * 2 * s:blk * 2 * s + s], b[blk * 2 * s + s:(blk + 1) * 2 * s]
            parts += [lo - b_mid, b_mid - hi] if reverse else [b_mid - lo, hi - b_mid]
        return jnp.concatenate(parts, axis=0)

    def query_or_key(q, k, l):
        if (1 << l) < SUBLANES:
            return jnp.where(query_side[l], q, k)
        s = 1 << l
        parts = []
        for blk in range(chunk // (2 * s)):
            lo, hi = slice(blk * 2 * s, blk * 2 * s + s), slice(blk * 2 * s + s, (blk + 1) * 2 * s)
            parts += [q[lo], k[hi]] if reverse else [k[lo], q[hi]]
        return jnp.concatenate(parts, axis=0)

    head_masks = [((lane >= hh * dk) & (lane < (hh + 1) * dk)) for hh in range(per_slab)]
    head_masks_bf = [hm.astype(BF16) for hm in head_masks]

    def stage_scale(ci):
        rows = slice(ci * chunk, (ci + 1) * chunk)
        lf_all = lf_ref[rows, :]
        sums = _sel_dot2(w_ref[...], lf_all)
        slabs = []
        for s in range(n_slabs):
            cols = slice(s * LANES, (s + 1) * LANES)
            q = q_ref[rows, cols].astype(F32)
            k = k_ref[rows, cols].astype(F32)
            lf = lf_all[:, cols]
            b = sums[:chunk, cols]
            ys = []
            for l in range(nlev):
                if l == 0:
                    e = jnp.where(query_side[0], lf, 0.0)
                elif l in REC_SEL_LEVELS:
                    i0 = (1 + REC_SEL_LEVELS.index(l)) * chunk
                    e = sums[i0:i0 + chunk, cols]
                else:
                    e = boundary_exponent(b, l)
                ys.append((jnp.exp2(e) * query_or_key(q, k, l)).astype(BF16))
            eb = jnp.exp2(b)
            b_end = jnp.broadcast_to(b[last:last + 1, :], (chunk, LANES))
            slabs.append(dict(
                lhs=ys + [q.astype(BF16)], rhs=ys + [k.astype(BF16)],
                q_in=(eb * q).astype(BF16),
                k_up=(jnp.exp2(b_end - b) * k).astype(BF16),
                decay=eb[last:last + 1]))
        return rows, slabs

    lane2 = lax.broadcasted_iota(jnp.int32, (1, 2 * dv), 1)
    v_first, v_second = (lane2 < dv).astype(BF16), (lane2 >= dv).astype(BF16)
    pair_masks = [jnp.concatenate([m, m], axis=1) for m in level_masks]

    def block_diag(a, b_):
        za, zb = jnp.zeros_like(a), jnp.zeros_like(b_)
        return jnp.concatenate([jnp.concatenate([a, zb], axis=1), jnp.concatenate([za, b_], axis=1)], axis=0)

    def pair_operands(slabs, pr, key, l=None):
        pick = (lambda d: d[key][l]) if l is not None else (lambda d: d[key])
        if per_slab == 2:
            t = pick(slabs[pr])
            return t, jnp.concatenate([t * head_masks_bf[0], t * head_masks_bf[1]], axis=0)
        a, b_ = pick(slabs[2 * pr]), pick(slabs[2 * pr + 1])
        return jnp.concatenate([a, b_], axis=1), block_diag(a, b_)

    def stage_intra(slabs):
        mats = []
        for pr in range(heads // 2):
            a_mat = jnp.zeros((chunk, 2 * chunk), F32)
            for l in range(nlev + 1):
                lhs = pair_operands(slabs, pr, "lhs", l)[0]
                rhs = pair_operands(slabs, pr, "rhs", l)[1]
                a_mat = jnp.where(pair_masks[l], _dot_nt(lhs, rhs), a_mat)
            mats.append(a_mat.astype(BF16))
        return mats

    def stage_out(rows, slabs, mats):
        for pr in range(heads // 2):
            pcols = slice(pr * 2 * dv, (pr + 1) * 2 * dv)
            vp = v_ref[rows, pcols]
            v_bd = jnp.concatenate([vp * v_first, vp * v_second], axis=0)
            q_in = pair_operands(slabs, pr, "q_in")[0]
            k_up = pair_operands(slabs, pr, "k_up")[0]
            if per_slab == 2:
                st = st_ref[pr]
                st_bf = st.astype(BF16)
                st_bd = jnp.concatenate([st_bf * head_masks_bf[0], st_bf * head_masks_bf[1]], axis=0)
            else:
                st_a, st_b = st_ref[2 * pr], st_ref[2 * pr + 1]
                st_bd = block_diag(st_a.astype(BF16), st_b.astype(BF16))
            o = _dot(mats[pr], v_bd) + _dot_nt(q_in, st_bd)
            u = _dot_tn(vp, k_up)
            if per_slab == 2:
                st_ref[pr] = st * slabs[pr]["decay"] + jnp.where(head_masks[0], u[:dv], u[dv:])
            else:
                st_ref[2 * pr] = st_a * slabs[2 * pr]["decay"] + u[:dv, :LANES]
                st_ref[2 * pr + 1] = st_b * slabs[2 * pr + 1]["decay"] + u[dv:, LANES:]
            if final:
                for hh in range(2):
                    ocols = slice((2 * pr + hh) * dv, (2 * pr + hh + 1) * dv)
                    tot = prev_ref[rows, ocols] + o[:, hh * dv:(hh + 1) * dv]
                    ms = jnp.mean(tot * tot, axis=-1, keepdims=True)
                    y = tot * lax.rsqrt(ms + RMS_EPS) * ng_ref[...]
                    o_ref[rows, ocols] = (y * _silu(gate_ref[rows, ocols])).astype(o_ref.dtype)
            else:
                o_ref[rows, pcols] = o

    n_sub = q_ref.shape[0] // chunk
    order = list(range(n_sub - 1, -1, -1) if reverse else range(n_sub))
    scaled, intra = {}, {}
    for t in range(n_sub + 2):
        if t < n_sub:
            scaled[t] = stage_scale(order[t])
        if 0 <= t - 1 < n_sub:
            intra[t - 1] = stage_intra(scaled[t - 1][1])
        if 0 <= t - 2 < n_sub:
            stage_out(*scaled.pop(t - 2), intra.pop(t - 2))


def _gated_recurrence(name, q, k, lf, v, prev, gate, norm_g, *, reverse, heads, dk, dv):
    bsz, seq, _ = q.shape
    chunk = REC_CHUNK
    blk_rows = min(REC_CHUNKS_PER_STEP * chunk, seq)
    n = seq // blk_rows
    final = prev is not None
    w_sel, lv, nlev = _rec_consts(chunk, reverse)
    cidx = (lambda b, i: (b, n - 1 - i, 0)) if reverse else (lambda b, i: (b, i, 0))
    row = lambda width: pl.BlockSpec((None, blk_rows, width), cidx)
    const2 = lambda a: pl.BlockSpec(a.shape, lambda b, i: (0, 0))
    in_specs = [row(heads * dk), row(heads * dk), row(heads * dk), row(heads * dv), const2(w_sel), const2(lv)]
    args = [q, k, lf, v, w_sel, lv]
    if final:
        in_specs += [row(heads * dv), row(heads * dv), pl.BlockSpec((1, dv), lambda b, i: (0, 0))]
        args += [prev, gate, norm_g.reshape(1, dv)]
    return pl.pallas_call(
        functools.partial(_rec_kernel, final=final, reverse=reverse, heads=heads, dk=dk, dv=dv, nlev=nlev),
        grid=(bsz, n),
        in_specs=in_specs,
        out_specs=row(heads * dv),
        out_shape=jax.ShapeDtypeStruct((bsz, seq, heads * dv), BF16 if final else F32),
        scratch_shapes=[pltpu.VMEM((heads * dk // LANES, dv, LANES), F32)],
        compiler_params=_cparams("parallel", "arbitrary"),
        name=f"{name}_{'bwd' if reverse else 'fwd'}",
    )(*args)


def _ssd_consts(chunk, reverse):
    idx = np.arange(chunk)
    if reverse:
        tri = idx[None, :] >= idx[:, None]
        rest = idx[None, :] < idx[:, None]
    else:
        tri = idx[None, :] <= idx[:, None]
        rest = idx[None, :] > idx[:, None]
    sel = np.concatenate([tri, rest], axis=0).astype(np.float32)
    expand = np.zeros((LANES, WIDTH_D), np.float32)
    for h in range(SSD_HEADS):
        expand[h + (SSD_HEADS if reverse else 0), h * SSD_HEAD_DIM:(h + 1) * SSD_HEAD_DIM] = 1.0
    return jnp.asarray(sel, BF16), jnp.asarray(expand, BF16)


def _ssd_kernel(*refs, final, reverse, n_blocks):
    chunk = REC_CHUNK
    if final:
        (xs_ref, bc_ref, dt_ref, dtb_ref, alog_ref, sel_ref, exp_ref,
         prev_ref, dz_ref, dskip_ref, ng_ref, o_ref, st_ref) = refs
    else:
        (x_ref, xprev_ref, xnext_ref, dt_ref, cw_ref, cb_ref, dtb_ref, alog_ref, sel_ref, exp_ref,
         o_ref, xs_ref, bc_ref, ext_ref, st_ref) = refs
    step = pl.program_id(1)
    blk = (n_blocks - 1 - step) if reverse else step
    blk_rows = dt_ref.shape[0]

    @pl.when(step == 0)
    def _():
        st_ref[...] = jnp.zeros_like(st_ref)

    if not final:
        halo = SUBLANES
        ext_ref[0:halo] = jnp.where(blk == 0, 0.0, xprev_ref[...])
        ext_ref[halo:halo + blk_rows] = x_ref[...]
        ext_ref[halo + blk_rows:2 * halo + blk_rows] = jnp.where(blk == n_blocks - 1, 0.0, xnext_ref[...])
        acc = cb_ref[...]
        for tap in range(SSD_CONV):
            o0 = halo - SSD_CONV // 2 + tap
            acc = acc + ext_ref[o0:o0 + blk_rows] * cw_ref[tap:tap + 1]
        xbc = _silu(acc)
        xs_ref[...] = xbc[:, :WIDTH_D]
        bc_ref[...] = xbc[:, WIDTH_D:].astype(BF16)

    gs = SSD_GROUPS * SSD_STATE
    last = 0 if reverse else chunk - 1
    ii = lax.broadcasted_iota(jnp.int32, (chunk, chunk), 0)
    jj = lax.broadcasted_iota(jnp.int32, (chunk, chunk), 1)
    causal = (ii <= jj) if reverse else (ii >= jj)
    lane = lax.broadcasted_iota(jnp.int32, (1, LANES), 1)
    lo_half = lane < SSD_HEAD_DIM
    col0 = SSD_HEADS if reverse else 0
    rep = SSD_HEADS // SSD_GROUPS
    n_pairs = SSD_HEADS // 2
    neg_a = -jnp.exp(alog_ref[...])

    def stage_decay(ci):
        rows = slice(ci * chunk, (ci + 1) * chunk)
        xs = xs_ref[rows, :]
        bm, cm = bc_ref[rows, :gs], bc_ref[rows, gs:]
        dt = _softplus(dt_ref[rows, :] + dtb_ref[...])
        sums = _sel_dot(sel_ref[...], dt * neg_a)
        a_cum, a_rem = sums[:chunk], sums[chunk:]
        a_cum_t = a_cum.T
        expand = exp_ref[...]
        x_dt = xs * _dot_sel(dt, expand)
        e_cum = jnp.exp(_dot_sel(a_cum, expand))
        x_rem = (x_dt * jnp.exp(_dot_sel(a_rem, expand))).astype(BF16)
        cbs = [_dot_nt(cm[:, g * SSD_STATE:(g + 1) * SSD_STATE], bm[:, g * SSD_STATE:(g + 1) * SSD_STATE])
               for g in range(SSD_GROUPS)]
        mats, stacks = [], []
        for p in range(n_pairs):
            pair = []
            for hh in range(2):
                ch = col0 + 2 * p + hh
                diff = a_cum[:, ch:ch + 1] - a_cum_t[ch:ch + 1, :]
                seg = jnp.exp(jnp.where(causal, diff, NEG_INF))
                pair.append((cbs[(2 * p) // rep] * seg).astype(BF16))
            mats.append(jnp.concatenate(pair, axis=1))
            xp = x_dt[:, p * LANES:(p + 1) * LANES]
            stacks.append(jnp.concatenate([jnp.where(lo_half, xp, 0.0), jnp.where(lo_half, 0.0, xp)],
                                          axis=0).astype(BF16))
        return dict(rows=rows, xs=xs, bm=bm, cm=cm, e_cum=e_cum, x_rem=x_rem, mats=mats, stacks=stacks)

    def stage_out(d):
        rows = d["rows"]
        ys = []
        for p in range(n_pairs):
            g = (2 * p) // rep
            gcols = slice(g * SSD_STATE, (g + 1) * SSD_STATE)
            pcols = slice(p * LANES, (p + 1) * LANES)
            st = st_ref[p]
            y = _dot(d["mats"][p], d["stacks"][p]) + d["e_cum"][:, pcols] * _dot(d["cm"][:, gcols], st.astype(BF16))
            upd = _dot_tn(d["bm"][:, gcols], d["x_rem"][:, pcols])
            st_ref[p] = st * d["e_cum"][last:last + 1, pcols] + upd
            if final:
                y = y + prev_ref[rows, pcols] + dskip_ref[:, pcols] * d["xs"][:, pcols]
                ys.append(y * _silu(dz_ref[rows, pcols]))
            else:
                o_ref[rows, pcols] = y
        if final:
            t = jnp.concatenate(ys, axis=1)
            ms = jnp.mean(t * t, axis=-1, keepdims=True)
            o_ref[rows, :] = (t * lax.rsqrt(ms + RMS_EPS) * ng_ref[...]).astype(o_ref.dtype)

    n_sub = blk_rows // chunk
    order = list(range(n_sub - 1, -1, -1) if reverse else range(n_sub))
    pending = None
    for ci in order:
        cur = stage_decay(ci)
        if pending is not None:
            stage_out(pending)
        pending = cur
    stage_out(pending)


def _ssd(dxbc, dt_raw, conv_w, conv_b, dt_bias, a_log, dz, d_skip, norm_g):
    bsz, seq, nch = dxbc.shape
    chunk = REC_CHUNK
    blk_rows = min(REC_CHUNKS_PER_STEP * chunk, seq)
    n = seq // blk_rows
    hb = blk_rows // SUBLANES
    nb = seq // SUBLANES
    const2 = lambda a: pl.BlockSpec(a.shape, lambda b, i: (0, 0))
    dtb = jnp.zeros((1, LANES), F32).at[0, :2 * SSD_HEADS].set(dt_bias.reshape(-1))
    alog = jnp.full((1, LANES), NEG_INF, F32).at[0, :2 * SSD_HEADS].set(a_log.reshape(-1))
    cb2 = conv_b.reshape(1, nch)
    state = pltpu.VMEM((SSD_HEADS // 2, SSD_STATE, LANES), F32)
    shp = lambda width, dt: jax.ShapeDtypeStruct((bsz, seq, width), dt)

    sel, expand = _ssd_consts(chunk, False)
    row = lambda width: pl.BlockSpec((None, blk_rows, width), lambda b, i: (b, i, 0))
    y_fwd, xs, bc = pl.pallas_call(
        functools.partial(_ssd_kernel, final=False, reverse=False, n_blocks=n),
        grid=(bsz, n),
        in_specs=[row(nch),
                  pl.BlockSpec((None, SUBLANES, nch), lambda b, i: (b, jnp.maximum(i * hb - 1, 0), 0)),
                  pl.BlockSpec((None, SUBLANES, nch), lambda b, i: (b, jnp.minimum((i + 1) * hb, nb - 1), 0)),
                  row(LANES), const2(conv_w), const2(cb2), const2(dtb), const2(alog), const2(sel), const2(expand)],
        out_specs=[row(WIDTH_D), row(WIDTH_D), row(nch - WIDTH_D)],
        out_shape=[shp(WIDTH_D, F32), shp(WIDTH_D, F32), shp(nch - WIDTH_D, BF16)],
        scratch_shapes=[pltpu.VMEM((blk_rows + 2 * SUBLANES, nch), F32), state],
        compiler_params=_cparams("parallel", "arbitrary"),
        name="ssd_fwd",
    )(dxbc, dxbc, dxbc, dt_raw, conv_w, cb2, dtb, alog, sel, expand)

    sel, expand = _ssd_consts(chunk, True)
    row = lambda width: pl.BlockSpec((None, blk_rows, width), lambda b, i: (b, n - 1 - i, 0))
    dsk = jnp.repeat(d_skip, SSD_HEAD_DIM).reshape(1, WIDTH_D)
    ng = norm_g.reshape(1, WIDTH_D)
    return pl.pallas_call(
        functools.partial(_ssd_kernel, final=True, reverse=True, n_blocks=n),
        grid=(bsz, n),
        in_specs=[row(WIDTH_D), row(nch - WIDTH_D), row(LANES), const2(dtb), const2(alog), const2(sel), const2(expand),
                  row(WIDTH_D), row(WIDTH_D), const2(dsk), const2(ng)],
        out_specs=row(WIDTH_D),
        out_shape=shp(WIDTH_D, BF16),
        scratch_shapes=[state],
        compiler_params=_cparams("parallel", "arbitrary"),
        name="ssd_bwd",
    )(xs, bc, dt_raw, dtb, alog, sel, expand, y_fwd, dz, dsk, ng)


def kernel(x, c, ada_w, ada_b, ln_g, ln_b, e_w_in, e_rpb, e_gla_w_up, e_gla_b, e_gla_norm_g, e_w_out,
           o_w_in, hgrn_lb, o_hgrn_norm_g, o_conv_w, o_conv_b, o_dt_bias, o_a_log, o_d_skip,
           o_ssm_norm_g, o_w_out):
    mod = _ada_mod(c, ada_w, ada_b)

    aq, ak, av, ag, bq, bk, bv, bg, lf_f, lf_b = _proj_even(x, mod[0], e_w_in[0], e_gla_w_up[0], e_gla_b[0])
    ya = _neighbourhood_attention(aq, ak, av, ag, _na_bias(e_rpb[0]))
    gla = functools.partial(_gated_recurrence, "gla", heads=GLA_HEADS, dk=GLA_DK, dv=GLA_DV)
    o_fwd = gla(bq, bk, lf_f, bv, None, None, None, reverse=False)
    yb = gla(bq, bk, lf_b, bv, o_fwd, bg, e_gla_norm_g[0], reverse=True)
    x = _out_ln(ya, yb, e_w_out[0], x, mod[0], ln_g[0], ln_b[0])

    cq, ck_f, lf_f, ck_b, lf_b, ci, cg, dz, dxbc, dt_raw = _proj_odd(x, mod[1], o_w_in[0], hgrn_lb)
    hgrn = functools.partial(_gated_recurrence, "hgrn", heads=HGRN_HEADS, dk=HGRN_DIM, dv=HGRN_DIM)
    o_fwd = hgrn(cq, ck_f, lf_f, ci, None, None, None, reverse=False)
    yc = hgrn(cq, ck_b, lf_b, ci, o_fwd, cg, o_hgrn_norm_g[0], reverse=True)
    yd = _ssd(dxbc, dt_raw, o_conv_w[0], o_conv_b[0], o_dt_bias[0], o_a_log[0], dz, o_d_skip[0], o_ssm_norm_g[0])
    return _out_ln(yc, yd, o_w_out[0], x, mod[1], ln_g[1], ln_b[1])
```

```python
import functools
import math

import numpy as np
import jax
import jax.numpy as jnp
from jax import lax
from jax.experimental import pallas as pl
from jax.experimental.pallas import tpu as pltpu

F32 = jnp.float32
BF16 = jnp.bfloat16

GRID_W = 64
NA_HEADS, NA_HEAD_DIM = 8, 64
NA_WIN_ROWS, NA_WIN_COLS = 8, 16
WIDTH_A = NA_HEADS * NA_HEAD_DIM
GLA_HEADS, GLA_DK, GLA_DV, GLA_RANK = 4, 64, 128, 16
GLA_GATE_NORM = 16.0
WIDTH_B = GLA_HEADS * GLA_DV
HGRN_HEADS, HGRN_DIM = 4, 128
WIDTH_C = HGRN_HEADS * HGRN_DIM
SSD_HEADS, SSD_HEAD_DIM, SSD_GROUPS, SSD_STATE, SSD_CONV = 8, 64, 2, 128, 4
WIDTH_D = SSD_HEADS * SSD_HEAD_DIM
SSD_CONV_CH = WIDTH_D + 2 * SSD_GROUPS * SSD_STATE
DEPTH = 2
DEEPNORM_ALPHA = (2 * DEPTH) ** 0.25
LN_EPS = 1e-5
RMS_EPS = 1e-6

LANES = 128
SUBLANES = 8
VMEM_LIMIT = 56 * 1024 * 1024

ROW_TILE = 512
OUT_ROW_TILE = 1024
REC_CHUNK = 128
REC_CHUNKS_PER_STEP = 8
REC_SEL_LEVELS = (1, 2)
NA_ROWS_PER_STEP = 8
NEG_INF = float("-inf")
LOG2_E = math.log2(math.e)


def _cparams(*sem):
    return pltpu.CompilerParams(dimension_semantics=sem, vmem_limit_bytes=VMEM_LIMIT)


def _dot(a, b):
    return jnp.dot(a, b, preferred_element_type=F32)


def _dot_nt(a, b):
    return lax.dot_general(a, b, (((1,), (1,)), ((), ())), preferred_element_type=F32)


def _dot_tn(a, b):
    return lax.dot_general(a, b, (((0,), (0,)), ((), ())), preferred_element_type=F32)


def _split3(x):
    hi = x.astype(BF16)
    r = x - hi.astype(F32)
    mid = r.astype(BF16)
    lo = (r - mid.astype(F32)).astype(BF16)
    return hi, mid, lo


def _sel_dot(sel, x):
    hi, mid, lo = _split3(x)
    return _dot(sel, hi) + _dot(sel, mid) + _dot(sel, lo)


def _sel_dot2(sel, x):
    hi = x.astype(BF16)
    lo = (x - hi.astype(F32)).astype(BF16)
    return _dot(sel, hi) + _dot(sel, lo)


def _dot_sel2(x, sel):
    hi = x.astype(BF16)
    lo = (x - hi.astype(F32)).astype(BF16)
    return _dot(hi, sel) + _dot(lo, sel)


def _sigmoid(x):
    return 1.0 / (1.0 + jnp.exp(-x))


def _silu(x):
    return x * _sigmoid(x)


def _log1pexp_neg_abs(x):
    return jnp.log1p(jnp.exp(-jnp.abs(x)))


def _softplus(x):
    return jnp.maximum(x, 0.0) + _log1pexp_neg_abs(x)


def _ada_kernel(c_ref, w_ref, b_ref, o_ref):
    cond = _silu(c_ref[...]).astype(BF16)
    o_ref[...] = _dot(cond, w_ref[...].astype(BF16)) + b_ref[...]


def _ada_mod(c, ada_w, ada_b):
    bsz, d = c.shape
    bp = -(-bsz // SUBLANES) * SUBLANES
    n3 = ada_w.shape[-1]
    tn = ROW_TILE
    c_pad = jnp.zeros((bp, d), F32).at[:bsz].set(c)
    out = pl.pallas_call(
        _ada_kernel,
        grid=(DEPTH, n3 // tn),
        in_specs=[pl.BlockSpec((bp, d), lambda l, j: (0, 0)),
                  pl.BlockSpec((None, d, tn), lambda l, j: (l, 0, j)),
                  pl.BlockSpec((None, 1, tn), lambda l, j: (l, 0, j))],
        out_specs=pl.BlockSpec((None, bp, tn), lambda l, j: (l, 0, j)),
        out_shape=jax.ShapeDtypeStruct((DEPTH, bp, n3), F32),
        compiler_params=_cparams("arbitrary", "arbitrary"),
        name="ada_mod",
    )(c_pad, ada_w, ada_b.reshape(DEPTH, 1, n3))
    return out[:, :bsz].reshape(DEPTH, bsz, 1, n3)


def _modulated(x_ref, mod_ref):
    d = x_ref.shape[-1]
    mod = mod_ref[...]
    return (x_ref[...] * (1.0 + mod[:, d:2 * d]) + mod[:, :d]).astype(BF16)


def _stage_weights(wt_ref, wbf_ref):
    n = wt_ref.shape[0]
    n_full = n // LANES * LANES

    @pl.when((pl.program_id(0) == 0) & (pl.program_id(1) == 0))
    def _():
        def body(r, carry):
            rows = pl.ds(pl.multiple_of(r * LANES, LANES), LANES)
            wbf_ref[rows, :] = wt_ref[rows, :].astype(BF16)
            return carry
        lax.fori_loop(0, n_full // LANES, body, 0)
        if n_full < wbf_ref.shape[0]:
            wbf_ref[n_full:, :] = jnp.zeros((wbf_ref.shape[0] - n_full, wbf_ref.shape[1]), BF16)
            wbf_ref[n_full:n, :] = wt_ref[n_full:n, :].astype(BF16)


def _proj_even_kernel(x_ref, mod_ref, wf_ref, wup_ref, gb_ref,
                      aq_ref, ak_ref, av_ref, ag_ref, bq_ref, bk_ref, bv_ref, bg_ref, lff_ref, lfb_ref, h_ref, w_ref):
    _stage_weights(wf_ref, w_ref)
    h_ref[...] = _modulated(x_ref, mod_ref)

    def mm(lo, hi):
        return _dot_nt(h_ref[...], w_ref[lo:hi, :])

    a, kb = WIDTH_A, GLA_HEADS * GLA_DK
    o = 4 * a + 2 * kb + 2 * WIDTH_B
    lr = mm(o, o + LANES)
    lr_hi = lr.astype(BF16)
    lr_lo = (lr - lr_hi.astype(F32)).astype(BF16)
    for d, out_ref in enumerate((lff_ref, lfb_ref)):
        wu = wup_ref[d]
        wu_hi = wu.astype(BF16)
        wu_lo = (wu - wu_hi.astype(F32)).astype(BF16)
        out_ref[...] = _dot(lr_hi, wu_hi) + _dot(lr_lo, wu_hi) + _dot(lr_hi, wu_lo) + gb_ref[d]

    def finish_gate(out_ref):
        z = out_ref[...]
        log_sig = jnp.minimum(z, 0.0) - jnp.log(1.0 + jnp.exp2(jnp.abs(z) * (-LOG2_E)))
        out_ref[...] = log_sig * (LOG2_E / GLA_GATE_NORM)

    aq_ref[...] = (mm(0, a) * (NA_HEAD_DIM ** -0.5)).astype(BF16)
    ak_ref[...] = mm(a, 2 * a).astype(BF16)
    finish_gate(lff_ref)
    av_ref[...] = mm(2 * a, 3 * a).astype(BF16)
    ag_ref[...] = mm(3 * a, 4 * a)
    finish_gate(lfb_ref)
    o = 4 * a
    bq_ref[...] = (mm(o, o + kb) * (GLA_DK ** -0.5)).astype(BF16)
    bk_ref[...] = mm(o + kb, o + 2 * kb).astype(BF16)
    o += 2 * kb
    bv_ref[...] = mm(o, o + WIDTH_B).astype(BF16)
    bg_ref[...] = mm(o + WIDTH_B, o + 2 * WIDTH_B)


def _proj_even(x, mod, w_in, gla_w_up, gla_b):
    bsz, seq, d = x.shape
    kb = GLA_HEADS * GLA_DK
    n_main = 4 * WIDTH_A + 2 * kb + 2 * WIDTH_B
    wt = jnp.swapaxes(w_in, 0, 1)
    wup = jnp.zeros((2, LANES, kb), F32)
    wup = wup.at[0, :GLA_RANK].set(gla_w_up[0]).at[1, GLA_RANK:2 * GLA_RANK].set(gla_w_up[1])
    tm = min(ROW_TILE, seq)
    row = lambda n: pl.BlockSpec((None, tm, n), lambda b, i: (b, i, 0))
    shp = lambda n, dt: jax.ShapeDtypeStruct((bsz, seq, n), dt)
    widths = [(WIDTH_A, BF16), (WIDTH_A, BF16), (WIDTH_A, BF16), (WIDTH_A, F32),
              (kb, BF16), (kb, BF16), (WIDTH_B, BF16), (WIDTH_B, F32), (kb, F32), (kb, F32)]
    return pl.pallas_call(
        _proj_even_kernel,
        grid=(bsz, seq // tm),
        in_specs=[row(d),
                  pl.BlockSpec((None, 1, mod.shape[-1]), lambda b, i: (b, 0, 0)),
                  pl.BlockSpec(wt.shape, lambda b, i: (0, 0), pipeline_mode=pl.Buffered(1)),
                  pl.BlockSpec(wup.shape, lambda b, i: (0, 0, 0)),
                  pl.BlockSpec((2, 1, kb), lambda b, i: (0, 0, 0))],
        out_specs=[row(n) for n, _ in widths],
        out_shape=[shp(n, dt) for n, dt in widths],
        scratch_shapes=[pltpu.VMEM((tm, d), BF16), pltpu.VMEM((n_main + LANES, d), BF16)],
        compiler_params=_cparams("arbitrary", "arbitrary"),
        name="proj_even",
    )(x, mod, wt, wup, gla_b.reshape(2, 1, kb))


def _proj_odd_kernel(x_ref, mod_ref, wf_ref, lbraw_ref,
                     cq_ref, kf_ref, lff_ref, kb_ref, lfb_ref, ci_ref, cg_ref, dz_ref, dxbc_ref, dt_ref, h_ref, w_ref):
    _stage_weights(wf_ref, w_ref)
    h_ref[...] = _modulated(x_ref, mod_ref)

    def mm(lo, hi):
        return _dot_nt(h_ref[...], w_ref[lo:hi, :])

    c = WIDTH_C
    lbr = lbraw_ref[...]
    mx = jnp.maximum(lbr[0:1], lbr[1:2])
    e0, e1 = jnp.exp(lbr[0:1] - mx), jnp.exp(lbr[1:2] - mx)
    lb = e1 / (e0 + e1)
    log_lb, log_ub = jnp.log(lb), jnp.log1p(-lb)
    gate_refs = ((kf_ref, lff_ref), (kb_ref, lfb_ref))
    for d, (_, lf_ref) in enumerate(gate_refs):
        lf_ref[...] = mm((1 + d) * c, (2 + d) * c)

    half = x_ref.shape[0] // 2

    def finish_gate(d, part):
        k_ref, lf_ref = gate_refs[d]
        rows = slice(part * half, (part + 1) * half)
        z = lf_ref[rows, :]
        t = jnp.exp2(jnp.abs(z) * (-LOG2_E))
        u = 1.0 + t
        k_ref[rows, :] = (1.0 - lb) * (jnp.where(z > 0.0, t, 1.0) / u)
        lc = log_ub + (jnp.minimum(z, 0.0) - jnp.log(u))
        w = 1.0 + jnp.exp2(jnp.abs(log_lb - lc) * (-LOG2_E))
        lf_ref[rows, :] = (jnp.maximum(log_lb, lc) + jnp.log(w)) * LOG2_E

    cq_ref[...] = (mm(0, c) * (HGRN_DIM ** -0.5)).astype(BF16)
    finish_gate(0, 0)
    ci_ref[...] = mm(3 * c, 4 * c).astype(BF16)
    finish_gate(0, 1)
    cg_ref[...] = mm(4 * c, 5 * c)
    finish_gate(1, 0)
    o = 5 * c
    dz_ref[...] = mm(o, o + WIDTH_D)
    finish_gate(1, 1)
    o += WIDTH_D
    dxbc_ref[...] = mm(o, o + SSD_CONV_CH)
    o += SSD_CONV_CH
    dt_ref[...] = mm(o, o + LANES)


def _proj_odd(x, mod, w_in, hgrn_lb):
    bsz, seq, d = x.shape
    n_main = 5 * WIDTH_C + WIDTH_D + SSD_CONV_CH
    wt = jnp.swapaxes(w_in, 0, 1)
    tm = min(ROW_TILE, seq)
    row = lambda n: pl.BlockSpec((None, tm, n), lambda b, i: (b, i, 0))
    shp = lambda n, dt: jax.ShapeDtypeStruct((bsz, seq, n), dt)
    widths = [(WIDTH_C, BF16), (WIDTH_C, F32), (WIDTH_C, F32), (WIDTH_C, F32), (WIDTH_C, F32), (WIDTH_C, BF16),
              (WIDTH_C, F32), (WIDTH_D, F32), (SSD_CONV_CH, F32), (LANES, F32)]
    return pl.pallas_call(
        _proj_odd_kernel,
        grid=(bsz, seq // tm),
        in_specs=[row(d),
                  pl.BlockSpec((None, 1, mod.shape[-1]), lambda b, i: (b, 0, 0)),
                  pl.BlockSpec(wt.shape, lambda b, i: (0, 0), pipeline_mode=pl.Buffered(1)),
                  pl.BlockSpec(hgrn_lb.shape, lambda b, i: (0, 0))],
        out_specs=[row(n) for n, _ in widths],
        out_shape=[shp(n, dt) for n, dt in widths],
        scratch_shapes=[pltpu.VMEM((tm, d), BF16), pltpu.VMEM((n_main + LANES, d), BF16)],
        compiler_params=_cparams("arbitrary", "arbitrary"),
        name="proj_odd",
    )(x, mod, wt, hgrn_lb)


def _out_ln_kernel(ya_ref, yb_ref, w_ref, x_ref, mod_ref, g_ref, b_ref, o_ref):
    d = x_ref.shape[-1]
    wa = ya_ref.shape[-1]
    y = _dot(ya_ref[...], w_ref[:wa, :]) + _dot(yb_ref[...], w_ref[wa:, :])
    gate = mod_ref[...][:, 2 * d:]
    t = DEEPNORM_ALPHA * x_ref[...] + gate * y
    mu = jnp.mean(t, axis=-1, keepdims=True)
    tc = t - mu
    var = jnp.mean(tc * tc, axis=-1, keepdims=True)
    o_ref[...] = tc * lax.rsqrt(var + LN_EPS) * g_ref[...] + b_ref[...]


def _out_ln(ya, yb, w_out, x, mod, ln_g, ln_b):
    bsz, seq, d = x.shape
    tm = min(OUT_ROW_TILE, seq)
    w = w_out.astype(BF16)
    row = lambda n: pl.BlockSpec((None, tm, n), lambda b, i: (b, i, 0))
    vec = pl.BlockSpec((1, d), lambda b, i: (0, 0))
    return pl.pallas_call(
        _out_ln_kernel,
        grid=(bsz, seq // tm),
        in_specs=[row(ya.shape[-1]), row(yb.shape[-1]),
                  pl.BlockSpec(w.shape, lambda b, i: (0, 0)),
                  row(d),
                  pl.BlockSpec((None, 1, mod.shape[-1]), lambda b, i: (b, 0, 0)),
                  vec, vec],
        out_specs=row(d),
        out_shape=jax.ShapeDtypeStruct((bsz, seq, d), F32),
        compiler_params=_cparams("parallel", "arbitrary"),
        name="out_ln",
    )(ya, yb, w, x, mod, ln_g.reshape(1, d), ln_b.reshape(1, d))


NA_BIAS_TYPES = NA_WIN_ROWS
NA_KEYS = NA_WIN_ROWS * GRID_W


def _na_bias_kernel(rpb_ref, o_ref):
    h = pl.program_id(0)
    shape = (GRID_W, LANES)
    q = lax.broadcasted_iota(jnp.int32, shape, 0)
    lane = lax.broadcasted_iota(jnp.int32, shape, 1)
    kc = lane & (GRID_W - 1)
    dc = kc - q + (NA_WIN_COLS - 1)
    cs = jnp.clip(q - NA_WIN_COLS // 2, 0, GRID_W - NA_WIN_COLS)
    valid = (kc >= cs) & (kc < cs + NA_WIN_COLS)
    upper = lane >= GRID_W
    n_dr, n_dc = 2 * NA_WIN_ROWS - 1, 2 * NA_WIN_COLS - 1
    tiles = []
    for dr in range(n_dr):
        acc = jnp.zeros(shape, F32)
        for d in range(n_dc):
            acc = jnp.where(dc == d, rpb_ref[(h * n_dr + dr) * n_dc + d], acc)
        tiles.append(jnp.where(valid, acc, NEG_INF))
    for t in range(NA_BIAS_TYPES):
        for c in range(NA_KEYS // LANES):
            dr0 = 2 * c + NA_WIN_ROWS - 1 - t
            o_ref[t, :, c * LANES:(c + 1) * LANES] = jnp.where(upper, tiles[dr0 + 1], tiles[dr0])


def _na_bias(rpb):
    return pl.pallas_call(
        _na_bias_kernel,
        grid=(NA_HEADS,),
        in_specs=[pl.BlockSpec(memory_space=pltpu.SMEM)],
        out_specs=pl.BlockSpec((NA_BIAS_TYPES, None, GRID_W, NA_KEYS), lambda h: (0, h, 0, 0)),
        out_shape=jax.ShapeDtypeStruct((NA_BIAS_TYPES, NA_HEADS, GRID_W, NA_KEYS), F32),
        compiler_params=_cparams("arbitrary"),
        name="na_bias",
    )(rpb.reshape(-1))


def _na_kernel(q_ref, k_ref, v_ref, g_ref, bias_ref, o_ref, *, n_rows):
    i = pl.program_id(1)
    lane = lax.broadcasted_iota(jnp.int32, (1, LANES), 1)
    head_mask = [(lane < NA_HEAD_DIM).astype(BF16), (lane >= NA_HEAD_DIM).astype(BF16)]
    half = NA_WIN_ROWS // 2
    pair_cols = [slice(p * LANES, (p + 1) * LANES) for p in range(WIDTH_A // LANES)]

    def stage_scores(rr):
        r = i * NA_ROWS_PER_STEP + rr
        row_start = jnp.clip(r - half, 0, n_rows - NA_WIN_ROWS)
        t = jnp.where(r < half, r, jnp.where(r > n_rows - half, r - (n_rows - NA_WIN_ROWS), half))
        koff = pl.multiple_of(row_start * GRID_W, GRID_W)
        rows = slice(rr * GRID_W, (rr + 1) * GRID_W)
        scores = []
        for p, cols in enumerate(pair_cols):
            qp = q_ref[rows, cols]
            qs = jnp.concatenate([qp * head_mask[0], qp * head_mask[1]], axis=0)
            kp = k_ref[pl.ds(koff, NA_KEYS), cols]
            scores.append(_dot_nt(qs, kp) + bias_ref[t, p])
        return rows, koff, scores

    def stage_out(rows, koff, scores):
        probs, norms = [], []
        for s in scores:
            e = jnp.exp(s - jnp.max(s, axis=-1, keepdims=True))
            norms.append(jnp.sum(e, axis=-1, keepdims=True))
            probs.append(e.astype(BF16))
        for p, cols in enumerate(pair_cols):
            vp = v_ref[pl.ds(koff, NA_KEYS), cols]
            o2 = _dot(probs[p], vp) / norms[p]
            o = jnp.where(lane < NA_HEAD_DIM, o2[:GRID_W], o2[GRID_W:])
            o_ref[rows, cols] = (o * _silu(g_ref[rows, cols])).astype(BF16)

    pending = None
    for rr in range(NA_ROWS_PER_STEP):
        cur = stage_scores(rr)
        if pending is not None:
            stage_out(*pending)
        pending = cur
    stage_out(*pending)


def _neighbourhood_attention(aq, ak, av, ag, bias):
    bsz, seq, w = aq.shape
    n_rows = seq // GRID_W
    tq = NA_ROWS_PER_STEP * GRID_W
    bias = bias.reshape(NA_BIAS_TYPES, NA_HEADS // 2, 2 * GRID_W, NA_KEYS)
    row = pl.BlockSpec((None, tq, w), lambda b, i: (b, i, 0))
    full = pl.BlockSpec((None, seq, w), lambda b, i: (b, 0, 0))
    return pl.pallas_call(
        functools.partial(_na_kernel, n_rows=n_rows),
        grid=(bsz, n_rows // NA_ROWS_PER_STEP),
        in_specs=[row, full, full, row,
                  pl.BlockSpec(bias.shape, lambda b, i: (0, 0, 0, 0))],
        out_specs=row,
        out_shape=jax.ShapeDtypeStruct((bsz, seq, w), BF16),
        compiler_params=_cparams("parallel", "arbitrary"),
        name="na_attn",
    )(aq, ak, av, ag, bias)


def _rec_consts(chunk, reverse):
    nlev = int(math.log2(chunk))
    assert 1 << nlev == chunk
    w = np.zeros(((nlev + 2), chunk, chunk), np.float32)
    lv = np.full((chunk, chunk), -1, np.int32)
    idx = np.arange(chunk)
    for l in range(nlev):
        s = 1 << l
        off = idx % (2 * s)
        mid = idx - off + s - 1
        for p in range(chunk):
            if off[p] >= s:
                w[l, p, mid[p] + 1:p + 1] = 1.0
            else:
                w[l, p, p + 1:mid[p] + 1] = 1.0
        same = (idx[:, None] // (2 * s)) == (idx[None, :] // (2 * s))
        lv[same & (off[:, None] >= s) & (off[None, :] < s)] = l
    lv[idx, idx] = nlev
    w[nlev] = (idx[None, :] <= idx[:, None])
    if reverse:
        w = w[:, ::-1, ::-1]
        lv = lv[::-1, ::-1]
    w = np.concatenate([w[nlev]] + [w[l] for l in REC_SEL_LEVELS], axis=0)
    return (jnp.asarray(w, BF16), jnp.asarray(np.ascontiguousarray(lv)), nlev)


def _rec_kernel(*refs, final, reverse, heads, dk, dv, nlev):
    chunk = REC_CHUNK
    q_ref, k_ref, lf_ref, v_ref, w_ref, lv_ref = refs[:6]
    if final:
        prev_ref, gate_ref, ng_ref, o_ref, st_ref = refs[6:]
    else:
        o_ref, st_ref = refs[6:]

    @pl.when(pl.program_id(1) == 0)
    def _():
        st_ref[...] = jnp.zeros_like(st_ref)

    lane = lax.broadcasted_iota(jnp.int32, (1, LANES), 1)
    row = lax.broadcasted_iota(jnp.int32, (chunk, LANES), 0)
    per_slab = LANES // dk
    n_slabs = heads * dk // LANES
    lv = lv_ref[...]
    level_masks = [lv == l for l in range(nlev + 1)]
    last = 0 if reverse else chunk - 1

    def is_query_side(l):
        off = row & (2 * (1 << l) - 1)
        return (off < (1 << l)) if reverse else (off >= (1 << l))

    query_side = [is_query_side(l) for l in range(nlev) if (1 << l) < SUBLANES]

    def boundary_exponent(b, l):
        s = 1 << l
        parts = []
        for blk in range(chunk // (2 * s)):
            mid = blk * 2 * s + (s if reverse else s - 1)
            b_mid = jnp.broadcast_to(b[mid:mid + 1, :], (s, LANES))
            lo, hi = b[blk * 2 * s:blk * 2 * s + s], b[blk * 2 * s + s:(blk + 1) * 2 * s]
            parts += [lo - b_mid, b_mid - hi] if reverse else [b_mid - lo, hi - b_mid]
        return jnp.concatenate(parts, axis=0)

    def query_or_key(q, k, l):
        if (1 << l) < SUBLANES:
            return jnp.where(query_side[l], q, k)
        s = 1 << l
        parts = []
        for blk in range(chunk // (2 * s)):
            lo, hi = slice(blk * 2 * s, blk * 2 * s + s), slice(blk * 2 * s + s, (blk + 1) * 2 * s)
            parts += [q[lo], k[hi]] if reverse else [k[lo], q[hi]]
        return jnp.concatenate(parts, axis=0)

    head_masks = [((lane >= hh * dk) & (lane < (hh + 1) * dk)) for hh in range(per_slab)]
    head_masks_bf = [hm.astype(BF16) for hm in head_masks]

    def stage_scale(ci):
        rows = slice(ci * chunk, (ci + 1) * chunk)
        lf_all = lf_ref[rows, :]
        sums = _sel_dot2(w_ref[...], lf_all)
        slabs = []
        for s in range(n_slabs):
            cols = slice(s * LANES, (s + 1) * LANES)
            q = q_ref[rows, cols].astype(F32)
            k = k_ref[rows, cols].astype(F32)
            lf = lf_all[:, cols]
            b = sums[:chunk, cols]
            ys = []
            for l in range(nlev):
                if l == 0:
                    e = jnp.where(query_side[0], lf, 0.0)
                elif l in REC_SEL_LEVELS:
                    i0 = (1 + REC_SEL_LEVELS.index(l)) * chunk
                    e = sums[i0:i0 + chunk, cols]
                else:
                    e = boundary_exponent(b, l)
                ys.append((jnp.exp2(e) * query_or_key(q, k, l)).astype(BF16))
            eb = jnp.exp2(b)
            b_end = jnp.broadcast_to(b[last:last + 1, :], (chunk, LANES))
            slabs.append(dict(
                lhs=ys + [q.astype(BF16)], rhs=ys + [k.astype(BF16)],
                q_in=(eb * q).astype(BF16),
                k_up=(jnp.exp2(b_end - b) * k).astype(BF16),
                decay=eb[last:last + 1]))
        return rows, slabs

    lane2 = lax.broadcasted_iota(jnp.int32, (1, 2 * dv), 1)
    v_first, v_second = (lane2 < dv).astype(BF16), (lane2 >= dv).astype(BF16)
    pair_masks = [jnp.concatenate([m, m], axis=1) for m in level_masks]

    def block_diag(a, b_):
        za, zb = jnp.zeros_like(a), jnp.zeros_like(b_)
        return jnp.concatenate([jnp.concatenate([a, zb], axis=1), jnp.concatenate([za, b_], axis=1)], axis=0)

    def pair_operands(slabs, pr, key, l=None):
        pick = (lambda d: d[key][l]) if l is not None else (lambda d: d[key])
        if per_slab == 2:
            t = pick(slabs[pr])
            return t, jnp.concatenate([t * head_masks_bf[0], t * head_masks_bf[1]], axis=0)
        a, b_ = pick(slabs[2 * pr]), pick(slabs[2 * pr + 1])
        return jnp.concatenate([a, b_], axis=1), block_diag(a, b_)

    def stage_intra(slabs):
        mats = []
        for pr in range(heads // 2):
            a_mat = jnp.zeros((chunk, 2 * chunk), F32)
            for l in range(nlev + 1):
                lhs = pair_operands(slabs, pr, "lhs", l)[0]
                rhs = pair_operands(slabs, pr, "rhs", l)[1]
                a_mat = jnp.where(pair_masks[l], _dot_nt(lhs, rhs), a_mat)
            mats.append(a_mat.astype(BF16))
        return mats

    def stage_out(rows, slabs, mats):
        for pr in range(heads // 2):
            pcols = slice(pr * 2 * dv, (pr + 1) * 2 * dv)
            vp = v_ref[rows, pcols]
            v_bd = jnp.concatenate([vp * v_first, vp * v_second], axis=0)
            q_in = pair_operands(slabs, pr, "q_in")[0]
            k_up = pair_operands(slabs, pr, "k_up")[0]
            if per_slab == 2:
                st = st_ref[pr]
                st_bf = st.astype(BF16)
                st_bd = jnp.concatenate([st_bf * head_masks_bf[0], st_bf * head_masks_bf[1]], axis=0)
            else:
                st_a, st_b = st_ref[2 * pr], st_ref[2 * pr + 1]
                st_bd = block_diag(st_a.astype(BF16), st_b.astype(BF16))
            o = _dot(mats[pr], v_bd) + _dot_nt(q_in, st_bd)
            u = _dot_tn(vp, k_up)
            if per_slab == 2:
                st_ref[pr] = st * slabs[pr]["decay"] + jnp.where(head_masks[0], u[:dv], u[dv:])
            else:
                st_ref[2 * pr] = st_a * slabs[2 * pr]["decay"] + u[:dv, :LANES]
                st_ref[2 * pr + 1] = st_b * slabs[2 * pr + 1]["decay"] + u[dv:, LANES:]
            if final:
                for hh in range(2):
                    ocols = slice((2 * pr + hh) * dv, (2 * pr + hh + 1) * dv)
                    tot = prev_ref[rows, ocols] + o[:, hh * dv:(hh + 1) * dv]
                    ms = jnp.mean(tot * tot, axis=-1, keepdims=True)
                    y = tot * lax.rsqrt(ms + RMS_EPS) * ng_ref[...]
                    o_ref[rows, ocols] = (y * _silu(gate_ref[rows, ocols])).astype(o_ref.dtype)
            else:
                o_ref[rows, pcols] = o

    n_sub = q_ref.shape[0] // chunk
    order = list(range(n_sub - 1, -1, -1) if reverse else range(n_sub))
    scaled, intra = {}, {}
    for t in range(n_sub + 2):
        if t < n_sub:
            scaled[t] = stage_scale(order[t])
        if 0 <= t - 1 < n_sub:
            intra[t - 1] = stage_intra(scaled[t - 1][1])
        if 0 <= t - 2 < n_sub:
            stage_out(*scaled.pop(t - 2), intra.pop(t - 2))


def _gated_recurrence(name, q, k, lf, v, prev, gate, norm_g, *, reverse, heads, dk, dv):
    bsz, seq, _ = q.shape
    chunk = REC_CHUNK
    blk_rows = min(REC_CHUNKS_PER_STEP * chunk, seq)
    n = seq // blk_rows
    final = prev is not None
    w_sel, lv, nlev = _rec_consts(chunk, reverse)
    cidx = (lambda b, i: (b, n - 1 - i, 0)) if reverse else (lambda b, i: (b, i, 0))
    row = lambda width: pl.BlockSpec((None, blk_rows, width), cidx)
    const2 = lambda a: pl.BlockSpec(a.shape, lambda b, i: (0, 0))
    in_specs = [row(heads * dk), row(heads * dk), row(heads * dk), row(heads * dv), const2(w_sel), const2(lv)]
    args = [q, k, lf, v, w_sel, lv]
    if final:
        in_specs += [row(heads * dv), row(heads * dv), pl.BlockSpec((1, dv), lambda b, i: (0, 0))]
        args += [prev, gate, norm_g.reshape(1, dv)]
    return pl.pallas_call(
        functools.partial(_rec_kernel, final=final, reverse=reverse, heads=heads, dk=dk, dv=dv, nlev=nlev),
        grid=(bsz, n),
        in_specs=in_specs,
        out_specs=row(heads * dv),
        out_shape=jax.ShapeDtypeStruct((bsz, seq, heads * dv), BF16 if final else F32),
        scratch_shapes=[pltpu.VMEM((heads * dk // LANES, dv, LANES), F32)],
        compiler_params=_cparams("parallel", "arbitrary"),
        name=f"{name}_{'bwd' if reverse else 'fwd'}",
    )(*args)


def _ssd_consts(chunk, reverse):
    idx = np.arange(chunk)
    if reverse:
        tri = idx[None, :] >= idx[:, None]
        rest = idx[None, :] < idx[:, None]
    else:
        tri = idx[None, :] <= idx[:, None]
        rest = idx[None, :] > idx[:, None]
    sel = np.concatenate([tri, rest], axis=0).astype(np.float32)
    expand = np.zeros((LANES, WIDTH_D), np.float32)
    for h in range(SSD_HEADS):
        expand[h + (SSD_HEADS if reverse else 0), h * SSD_HEAD_DIM:(h + 1) * SSD_HEAD_DIM] = 1.0
    return jnp.asarray(sel, BF16), jnp.asarray(expand, BF16)


def _ssd_kernel(*refs, final, reverse, n_blocks):
    chunk = REC_CHUNK
    if final:
        (xs_ref, bc_ref, dt_ref, dtb_ref, alog_ref, sel_ref, exp_ref,
         prev_ref, dz_ref, dskip_ref, ng_ref, o_ref, st_ref) = refs
    else:
        (x_ref, xprev_ref, xnext_ref, dt_ref, cw_ref, cb_ref, dtb_ref, alog_ref, sel_ref, exp_ref,
         o_ref, xs_ref, bc_ref, ext_ref, st_ref) = refs
    step = pl.program_id(1)
    blk = (n_blocks - 1 - step) if reverse else step
    blk_rows = dt_ref.shape[0]

    @pl.when(step == 0)
    def _():
        st_ref[...] = jnp.zeros_like(st_ref)

    if not final:
        halo = SUBLANES
        ext_ref[0:halo] = jnp.where(blk == 0, 0.0, xprev_ref[...])
        ext_ref[halo:halo + blk_rows] = x_ref[...]
        ext_ref[halo + blk_rows:2 * halo + blk_rows] = jnp.where(blk == n_blocks - 1, 0.0, xnext_ref[...])
        acc = cb_ref[...]
        for tap in range(SSD_CONV):
            o0 = halo - SSD_CONV // 2 + tap
            acc = acc + ext_ref[o0:o0 + blk_rows] * cw_ref[tap:tap + 1]
        xbc = _silu(acc)
        xs_ref[...] = xbc[:, :WIDTH_D]
        bc_ref[...] = xbc[:, WIDTH_D:].astype(BF16)

    gs = SSD_GROUPS * SSD_STATE
    last = 0 if reverse else chunk - 1
    ii = lax.broadcasted_iota(jnp.int32, (chunk, chunk), 0)
    jj = lax.broadcasted_iota(jnp.int32, (chunk, chunk), 1)
    causal = (ii <= jj) if reverse else (ii >= jj)
    lane = lax.broadcasted_iota(jnp.int32, (1, LANES), 1)
    lo_half = lane < SSD_HEAD_DIM
    col0 = SSD_HEADS if reverse else 0
    rep = SSD_HEADS // SSD_GROUPS
    n_pairs = SSD_HEADS // 2
    neg_a = -jnp.exp(alog_ref[...])

    def stage_decay(ci):
        rows = slice(ci * chunk, (ci + 1) * chunk)
        xs = xs_ref[rows, :]
        bm, cm = bc_ref[rows, :gs], bc_ref[rows, gs:]
        dt = _softplus(dt_ref[rows, :] + dtb_ref[...])
        sums = _sel_dot(sel_ref[...], dt * neg_a)
        a_cum, a_rem = sums[:chunk], sums[chunk:]
        a_cum_t = a_cum.T
        expand = exp_ref[...]
        x_dt = xs * _dot_sel2(dt, expand)
        e_cum = jnp.exp(_dot_sel2(a_cum, expand))
        x_rem = (x_dt * jnp.exp(_dot_sel2(a_rem, expand))).astype(BF16)
        cbs = [_dot_nt(cm[:, g * SSD_STATE:(g + 1) * SSD_STATE], bm[:, g * SSD_STATE:(g + 1) * SSD_STATE])
               for g in range(SSD_GROUPS)]
        mats, stacks = [], []
        for p in range(n_pairs):
            pair = []
            for hh in range(2):
                ch = col0 + 2 * p + hh
                diff = a_cum[:, ch:ch + 1] - a_cum_t[ch:ch + 1, :]
                seg = jnp.exp(jnp.where(causal, diff, NEG_INF))
                pair.append((cbs[(2 * p) // rep] * seg).astype(BF16))
            mats.append(jnp.concatenate(pair, axis=1))
            xp = x_dt[:, p * LANES:(p + 1) * LANES]
            stacks.append(jnp.concatenate([jnp.where(lo_half, xp, 0.0), jnp.where(lo_half, 0.0, xp)],
                                          axis=0).astype(BF16))
        return dict(rows=rows, xs=xs, bm=bm, cm=cm, e_cum=e_cum, x_rem=x_rem, mats=mats, stacks=stacks)

    def stage_out(d):
        rows = d["rows"]
        ys = []
        for p in range(n_pairs):
            g = (2 * p) // rep
            gcols = slice(g * SSD_STATE, (g + 1) * SSD_STATE)
            pcols = slice(p * LANES, (p + 1) * LANES)
            st = st_ref[p]
            y = _dot(d["mats"][p], d["stacks"][p]) + d["e_cum"][:, pcols] * _dot(d["cm"][:, gcols], st.astype(BF16))
            upd = _dot_tn(d["bm"][:, gcols], d["x_rem"][:, pcols])
            st_ref[p] = st * d["e_cum"][last:last + 1, pcols] + upd
            if final:
                y = y + prev_ref[rows, pcols] + dskip_ref[:, pcols] * d["xs"][:, pcols]
                ys.append(y * _silu(dz_ref[rows, pcols]))
            else:
                o_ref[rows, pcols] = y
        if final:
            t = jnp.concatenate(ys, axis=1)
            ms = jnp.mean(t * t, axis=-1, keepdims=True)
            o_ref[rows, :] = (t * lax.rsqrt(ms + RMS_EPS) * ng_ref[...]).astype(o_ref.dtype)

    n_sub = blk_rows // chunk
    order = list(range(n_sub - 1, -1, -1) if reverse else range(n_sub))
    pending = None
    for ci in order:
        cur = stage_decay(ci)
        if pending is not None:
            stage_out(pending)
        pending = cur
    stage_out(pending)


def _ssd(dxbc, dt_raw, conv_w, conv_b, dt_bias, a_log, dz, d_skip, norm_g):
    bsz, seq, nch = dxbc.shape
    chunk = REC_CHUNK
    blk_rows = min(REC_CHUNKS_PER_STEP * chunk, seq)
    n = seq // blk_rows
    hb = blk_rows // SUBLANES
    nb = seq // SUBLANES
    const2 = lambda a: pl.BlockSpec(a.shape, lambda b, i: (0, 0))
    dtb = jnp.zeros((1, LANES), F32).at[0, :2 * SSD_HEADS].set(dt_bias.reshape(-1))
    alog = jnp.full((1, LANES), NEG_INF, F32).at[0, :2 * SSD_HEADS].set(a_log.reshape(-1))
    cb2 = conv_b.reshape(1, nch)
    state = pltpu.VMEM((SSD_HEADS // 2, SSD_STATE, LANES), F32)
    shp = lambda width, dt: jax.ShapeDtypeStruct((bsz, seq, width), dt)

    sel, expand = _ssd_consts(chunk, False)
    row = lambda width: pl.BlockSpec((None, blk_rows, width), lambda b, i: (b, i, 0))
    y_fwd, xs, bc = pl.pallas_call(
        functools.partial(_ssd_kernel, final=False, reverse=False, n_blocks=n),
        grid=(bsz, n),
        in_specs=[row(nch),
                  pl.BlockSpec((None, SUBLANES, nch), lambda b, i: (b, jnp.maximum(i * hb - 1, 0), 0)),
                  pl.BlockSpec((None, SUBLANES, nch), lambda b, i: (b, jnp.minimum((i + 1) * hb, nb - 1), 0)),
                  row(LANES), const2(conv_w), const2(cb2), const2(dtb), const2(alog), const2(sel), const2(expand)],
        out_specs=[row(WIDTH_D), row(WIDTH_D), row(nch - WIDTH_D)],
        out_shape=[shp(WIDTH_D, F32), shp(WIDTH_D, F32), shp(nch - WIDTH_D, BF16)],
        scratch_shapes=[pltpu.VMEM((blk_rows + 2 * SUBLANES, nch), F32), state],
        compiler_params=_cparams("parallel", "arbitrary"),
        name="ssd_fwd",
    )(dxbc, dxbc, dxbc, dt_raw, conv_w, cb2, dtb, alog, sel, expand)

    sel, expand = _ssd_consts(chunk, True)
    row = lambda width: pl.BlockSpec((None, blk_rows, width), lambda b, i: (b, n - 1 - i, 0))
    dsk = jnp.repeat(d_skip, SSD_HEAD_DIM).reshape(1, WIDTH_D)
    ng = norm_g.reshape(1, WIDTH_D)
    return pl.pallas_call(
        functools.partial(_ssd_kernel, final=True, reverse=True, n_blocks=n),
        grid=(bsz, n),
        in_specs=[row(WIDTH_D), row(nch - WIDTH_D), row(LANES), const2(dtb), const2(alog), const2(sel), const2(expand),
                  row(WIDTH_D), row(WIDTH_D), const2(dsk), const2(ng)],
        out_specs=row(WIDTH_D),
        out_shape=shp(WIDTH_D, BF16),
        scratch_shapes=[state],
        compiler_params=_cparams("parallel", "arbitrary"),
        name="ssd_bwd",
    )(xs, bc, dt_raw, dtb, alog, sel, expand, y_fwd, dz, dsk, ng)


def kernel(x, c, ada_w, ada_b, ln_g, ln_b, e_w_in, e_rpb, e_gla_w_up, e_gla_b, e_gla_norm_g, e_w_out,
           o_w_in, hgrn_lb, o_hgrn_norm_g, o_conv_w, o_conv_b, o_dt_bias, o_a_log, o_d_skip,
           o_ssm_norm_g, o_w_out):
    mod = _ada_mod(c, ada_w, ada_b)

    aq, ak, av, ag, bq, bk, bv, bg, lf_f, lf_b = _proj_even(x, mod[0], e_w_in[0], e_gla_w_up[0], e_gla_b[0])
    ya = _neighbourhood_attention(aq, ak, av, ag, _na_bias(e_rpb[0]))
    gla = functools.partial(_gated_recurrence, "gla", heads=GLA_HEADS, dk=GLA_DK, dv=GLA_DV)
    o_fwd = gla(bq, bk, lf_f, bv, None, None, None, reverse=False)
    yb = gla(bq, bk, lf_b, bv, o_fwd, bg, e_gla_norm_g[0], reverse=True)
    x = _out_ln(ya, yb, e_w_out[0], x, mod[0], ln_g[0], ln_b[0])

    cq, ck_f, lf_f, ck_b, lf_b, ci, cg, dz, dxbc, dt_raw = _proj_odd(x, mod[1], o_w_in[0], hgrn_lb)
    hgrn = functools.partial(_gated_recurrence, "hgrn", heads=HGRN_HEADS, dk=HGRN_DIM, dv=HGRN_DIM)
    o_fwd = hgrn(cq, ck_f, lf_f, ci, None, None, None, reverse=False)
    yc = hgrn(cq, ck_b, lf_b, ci, o_fwd, cg, o_hgrn_norm_g[0], reverse=True)
    yd = _ssd(dxbc, dt_raw, o_conv_w[0], o_conv_b[0], o_dt_bias[0], o_a_log[0], dz, o_d_skip[0], o_ssm_norm_g[0])
    return _out_ln(yc, yd, o_w_out[0], x, mod[1], ln_g[1], ln_b[1])
```

```python
import functools
import math

import numpy as np
import jax
import jax.numpy as jnp
from jax import lax
from jax.experimental import pallas as pl
from jax.experimental.pallas import tpu as pltpu

F32 = jnp.float32
BF16 = jnp.bfloat16

GRID_W = 64
NA_HEADS, NA_HEAD_DIM = 8, 64
NA_WIN_ROWS, NA_WIN_COLS = 8, 16
WIDTH_A = NA_HEADS * NA_HEAD_DIM
GLA_HEADS, GLA_DK, GLA_DV, GLA_RANK = 4, 64, 128, 16
GLA_GATE_NORM = 16.0
WIDTH_B = GLA_HEADS * GLA_DV
HGRN_HEADS, HGRN_DIM = 4, 128
WIDTH_C = HGRN_HEADS * HGRN_DIM
SSD_HEADS, SSD_HEAD_DIM, SSD_GROUPS, SSD_STATE, SSD_CONV = 8, 64, 2, 128, 4
WIDTH_D = SSD_HEADS * SSD_HEAD_DIM
SSD_CONV_CH = WIDTH_D + 2 * SSD_GROUPS * SSD_STATE
DEPTH = 2
DEEPNORM_ALPHA = (2 * DEPTH) ** 0.25
LN_EPS = 1e-5
RMS_EPS = 1e-6

LANES = 128
SUBLANES = 8
VMEM_LIMIT = 56 * 1024 * 1024

ROW_TILE = 512
OUT_ROW_TILE = 1024
REC_CHUNK = 128
REC_CHUNKS_PER_STEP = 8
REC_SEL_LEVELS = (1, 2)
NA_ROWS_PER_STEP = 8
NEG_INF = float("-inf")
LOG2_E = math.log2(math.e)


def _cparams(*sem):
    return pltpu.CompilerParams(dimension_semantics=sem, vmem_limit_bytes=VMEM_LIMIT)


def _dot(a, b):
    return jnp.dot(a, b, preferred_element_type=F32)


def _dot_nt(a, b):
    return lax.dot_general(a, b, (((1,), (1,)), ((), ())), preferred_element_type=F32)


def _dot_tn(a, b):
    return lax.dot_general(a, b, (((0,), (0,)), ((), ())), preferred_element_type=F32)


def _sel_dot2(sel, x):
    hi = x.astype(BF16)
    lo = (x - hi.astype(F32)).astype(BF16)
    return _dot(sel, hi) + _dot(sel, lo)


def _dot_sel2(x, sel):
    hi = x.astype(BF16)
    lo = (x - hi.astype(F32)).astype(BF16)
    return _dot(hi, sel) + _dot(lo, sel)


def _sigmoid(x):
    return 1.0 / (1.0 + jnp.exp(-x))


def _silu(x):
    return x * _sigmoid(x)


def _log1pexp_neg_abs(x):
    return jnp.log1p(jnp.exp(-jnp.abs(x)))


def _softplus(x):
    return jnp.maximum(x, 0.0) + _log1pexp_neg_abs(x)


def _ada_kernel(c_ref, w_ref, b_ref, o_ref):
    cond = _silu(c_ref[...]).astype(BF16)
    o_ref[...] = _dot(cond, w_ref[...].astype(BF16)) + b_ref[...]


def _ada_mod(c, ada_w, ada_b):
    bsz, d = c.shape
    bp = -(-bsz // SUBLANES) * SUBLANES
    n3 = ada_w.shape[-1]
    tn = OUT_ROW_TILE
    c_pad = jnp.zeros((bp, d), F32).at[:bsz].set(c)
    out = pl.pallas_call(
        _ada_kernel,
        grid=(DEPTH, n3 // tn),
        in_specs=[pl.BlockSpec((bp, d), lambda l, j: (0, 0)),
                  pl.BlockSpec((None, d, tn), lambda l, j: (l, 0, j)),
                  pl.BlockSpec((None, 1, tn), lambda l, j: (l, 0, j))],
        out_specs=pl.BlockSpec((None, bp, tn), lambda l, j: (l, 0, j)),
        out_shape=jax.ShapeDtypeStruct((DEPTH, bp, n3), F32),
        compiler_params=_cparams("arbitrary", "arbitrary"),
        name="ada_mod",
    )(c_pad, ada_w, ada_b.reshape(DEPTH, 1, n3))
    return out[:, :bsz].reshape(DEPTH, bsz, 1, n3)


def _modulated(x_ref, mod_ref):
    d = x_ref.shape[-1]
    mod = mod_ref[...]
    return (x_ref[...] * (1.0 + mod[:, d:2 * d]) + mod[:, :d]).astype(BF16)


def _stage_weights(wt_ref, wbf_ref):
    n = wt_ref.shape[0]
    n_full = n // LANES * LANES

    @pl.when((pl.program_id(0) == 0) & (pl.program_id(1) == 0))
    def _():
        def body(r, carry):
            rows = pl.ds(pl.multiple_of(r * LANES, LANES), LANES)
            wbf_ref[rows, :] = wt_ref[rows, :].astype(BF16)
            return carry
        lax.fori_loop(0, n_full // LANES, body, 0)
        if n_full < wbf_ref.shape[0]:
            wbf_ref[n_full:, :] = jnp.zeros((wbf_ref.shape[0] - n_full, wbf_ref.shape[1]), BF16)
            wbf_ref[n_full:n, :] = wt_ref[n_full:n, :].astype(BF16)


def _proj_even_kernel(x_ref, mod_ref, wf_ref, wup_ref, gb_ref,
                      aq_ref, ak_ref, av_ref, ag_ref, bq_ref, bk_ref, bv_ref, bg_ref, lff_ref, lfb_ref, h_ref, w_ref):
    _stage_weights(wf_ref, w_ref)
    h_ref[...] = _modulated(x_ref, mod_ref)

    def mm(lo, hi):
        return _dot_nt(h_ref[...], w_ref[lo:hi, :])

    a, kb = WIDTH_A, GLA_HEADS * GLA_DK
    aq_ref[...] = (mm(0, a) * (NA_HEAD_DIM ** -0.5)).astype(BF16)
    o = 4 * a + 2 * kb + 2 * WIDTH_B
    lr = mm(o, o + LANES)
    lr_hi = lr.astype(BF16)
    lr_lo = (lr - lr_hi.astype(F32)).astype(BF16)
    ak_ref[...] = mm(a, 2 * a).astype(BF16)

    def gate_logits(d, out_ref):
        wu = wup_ref[d]
        wu_hi = wu.astype(BF16)
        wu_lo = (wu - wu_hi.astype(F32)).astype(BF16)
        out_ref[...] = _dot(lr_hi, wu_hi) + _dot(lr_lo, wu_hi) + _dot(lr_hi, wu_lo) + gb_ref[d]

    def finish_gate(out_ref):
        z = out_ref[...]
        log_sig = jnp.minimum(z, 0.0) - jnp.log(1.0 + jnp.exp2(jnp.abs(z) * (-LOG2_E)))
        out_ref[...] = log_sig * (LOG2_E / GLA_GATE_NORM)

    gate_logits(0, lff_ref)
    av_ref[...] = mm(2 * a, 3 * a).astype(BF16)
    gate_logits(1, lfb_ref)
    ag_ref[...] = mm(3 * a, 4 * a)
    finish_gate(lff_ref)
    o = 4 * a
    bqk = mm(o, o + 2 * kb)
    bq_ref[...] = (bqk[:, :kb] * (GLA_DK ** -0.5)).astype(BF16)
    bk_ref[...] = bqk[:, kb:].astype(BF16)
    finish_gate(lfb_ref)
    o += 2 * kb
    bv_ref[...] = mm(o, o + WIDTH_B).astype(BF16)
    bg_ref[...] = mm(o + WIDTH_B, o + 2 * WIDTH_B)


def _proj_even(x, mod, w_in, gla_w_up, gla_b):
    bsz, seq, d = x.shape
    kb = GLA_HEADS * GLA_DK
    n_main = 4 * WIDTH_A + 2 * kb + 2 * WIDTH_B
    wt = jnp.swapaxes(w_in, 0, 1)
    wup = jnp.zeros((2, LANES, kb), F32)
    wup = wup.at[0, :GLA_RANK].set(gla_w_up[0]).at[1, GLA_RANK:2 * GLA_RANK].set(gla_w_up[1])
    tm = min(ROW_TILE, seq)
    row = lambda n: pl.BlockSpec((None, tm, n), lambda b, i: (b, i, 0))
    shp = lambda n, dt: jax.ShapeDtypeStruct((bsz, seq, n), dt)
    widths = [(WIDTH_A, BF16), (WIDTH_A, BF16), (WIDTH_A, BF16), (WIDTH_A, F32),
              (kb, BF16), (kb, BF16), (WIDTH_B, BF16), (WIDTH_B, F32), (kb, F32), (kb, F32)]
    return pl.pallas_call(
        _proj_even_kernel,
        grid=(bsz, seq // tm),
        in_specs=[row(d),
                  pl.BlockSpec((None, 1, mod.shape[-1]), lambda b, i: (b, 0, 0)),
                  pl.BlockSpec(wt.shape, lambda b, i: (0, 0), pipeline_mode=pl.Buffered(1)),
                  pl.BlockSpec(wup.shape, lambda b, i: (0, 0, 0)),
                  pl.BlockSpec((2, 1, kb), lambda b, i: (0, 0, 0))],
        out_specs=[row(n) for n, _ in widths],
        out_shape=[shp(n, dt) for n, dt in widths],
        scratch_shapes=[pltpu.VMEM((tm, d), BF16), pltpu.VMEM((n_main + LANES, d), BF16)],
        compiler_params=_cparams("arbitrary", "arbitrary"),
        name="proj_even",
    )(x, mod, wt, wup, gla_b.reshape(2, 1, kb))


def _proj_odd_kernel(x_ref, mod_ref, wf_ref, lbraw_ref,
                     cq_ref, kf_ref, lff_ref, kb_ref, lfb_ref, ci_ref, cg_ref, dz_ref, dxbc_ref, dt_ref, h_ref, w_ref):
    _stage_weights(wf_ref, w_ref)
    h_ref[...] = _modulated(x_ref, mod_ref)

    def mm(lo, hi):
        return _dot_nt(h_ref[...], w_ref[lo:hi, :])

    c = WIDTH_C
    lbr = lbraw_ref[...]
    mx = jnp.maximum(lbr[0:1], lbr[1:2])
    e0, e1 = jnp.exp(lbr[0:1] - mx), jnp.exp(lbr[1:2] - mx)
    lb = e1 / (e0 + e1)
    log_lb, log_ub = jnp.log(lb), jnp.log1p(-lb)
    gate_refs = ((kf_ref, lff_ref), (kb_ref, lfb_ref))
    for d, (_, lf_ref) in enumerate(gate_refs):
        lf_ref[...] = mm((1 + d) * c, (2 + d) * c)

    half = x_ref.shape[0] // 2

    def finish_gate(d, part):
        k_ref, lf_ref = gate_refs[d]
        rows = slice(part * half, (part + 1) * half)
        z = lf_ref[rows, :]
        t = jnp.exp2(jnp.abs(z) * (-LOG2_E))
        u = 1.0 + t
        k_ref[rows, :] = (1.0 - lb) * (jnp.where(z > 0.0, t, 1.0) / u)
        lc = log_ub + (jnp.minimum(z, 0.0) - jnp.log(u))
        w = 1.0 + jnp.exp2(jnp.abs(log_lb - lc) * (-LOG2_E))
        lf_ref[rows, :] = (jnp.maximum(log_lb, lc) + jnp.log(w)) * LOG2_E

    cq_ref[...] = (mm(0, c) * (HGRN_DIM ** -0.5)).astype(BF16)
    finish_gate(0, 0)
    ci_ref[...] = mm(3 * c, 4 * c).astype(BF16)
    finish_gate(0, 1)
    cg_ref[...] = mm(4 * c, 5 * c)
    finish_gate(1, 0)
    o = 5 * c
    dz_ref[...] = mm(o, o + WIDTH_D)
    finish_gate(1, 1)
    o += WIDTH_D
    dxbc_ref[...] = mm(o, o + SSD_CONV_CH)
    o += SSD_CONV_CH
    dt_ref[...] = mm(o, o + LANES)


def _proj_odd(x, mod, w_in, hgrn_lb):
    bsz, seq, d = x.shape
    n_main = 5 * WIDTH_C + WIDTH_D + SSD_CONV_CH
    wt = jnp.swapaxes(w_in, 0, 1)
    tm = min(ROW_TILE, seq)
    row = lambda n: pl.BlockSpec((None, tm, n), lambda b, i: (b, i, 0))
    shp = lambda n, dt: jax.ShapeDtypeStruct((bsz, seq, n), dt)
    widths = [(WIDTH_C, BF16), (WIDTH_C, F32), (WIDTH_C, F32), (WIDTH_C, F32), (WIDTH_C, F32), (WIDTH_C, BF16),
              (WIDTH_C, F32), (WIDTH_D, F32), (SSD_CONV_CH, F32), (LANES, F32)]
    return pl.pallas_call(
        _proj_odd_kernel,
        grid=(bsz, seq // tm),
        in_specs=[row(d),
                  pl.BlockSpec((None, 1, mod.shape[-1]), lambda b, i: (b, 0, 0)),
                  pl.BlockSpec(wt.shape, lambda b, i: (0, 0), pipeline_mode=pl.Buffered(1)),
                  pl.BlockSpec(hgrn_lb.shape, lambda b, i: (0, 0))],
        out_specs=[row(n) for n, _ in widths],
        out_shape=[shp(n, dt) for n, dt in widths],
        scratch_shapes=[pltpu.VMEM((tm, d), BF16), pltpu.VMEM((n_main + LANES, d), BF16)],
        compiler_params=_cparams("arbitrary", "arbitrary"),
        name="proj_odd",
    )(x, mod, wt, hgrn_lb)


def _out_ln_kernel(ya_ref, yb_ref, w_ref, x_ref, mod_ref, g_ref, b_ref, o_ref):
    d = x_ref.shape[-1]
    wa = ya_ref.shape[-1]
    y = _dot(ya_ref[...], w_ref[:wa, :]) + _dot(yb_ref[...], w_ref[wa:, :])
    gate = mod_ref[...][:, 2 * d:]
    t = DEEPNORM_ALPHA * x_ref[...] + gate * y
    mu = jnp.mean(t, axis=-1, keepdims=True)
    tc = t - mu
    var = jnp.mean(tc * tc, axis=-1, keepdims=True)
    o_ref[...] = tc * lax.rsqrt(var + LN_EPS) * g_ref[...] + b_ref[...]


def _out_ln(ya, yb, w_out, x, mod, ln_g, ln_b):
    bsz, seq, d = x.shape
    tm = min(OUT_ROW_TILE, seq)
    w = w_out.astype(BF16)
    row = lambda n: pl.BlockSpec((None, tm, n), lambda b, i: (b, i, 0))
    vec = pl.BlockSpec((1, d), lambda b, i: (0, 0))
    return pl.pallas_call(
        _out_ln_kernel,
        grid=(bsz, seq // tm),
        in_specs=[row(ya.shape[-1]), row(yb.shape[-1]),
                  pl.BlockSpec(w.shape, lambda b, i: (0, 0)),
                  row(d),
                  pl.BlockSpec((None, 1, mod.shape[-1]), lambda b, i: (b, 0, 0)),
                  vec, vec],
        out_specs=row(d),
        out_shape=jax.ShapeDtypeStruct((bsz, seq, d), F32),
        compiler_params=_cparams("parallel", "arbitrary"),
        name="out_ln",
    )(ya, yb, w, x, mod, ln_g.reshape(1, d), ln_b.reshape(1, d))


NA_BIAS_TYPES = NA_WIN_ROWS
NA_KEYS = NA_WIN_ROWS * GRID_W


def _na_bias_kernel(rpb_ref, o_ref):
    h = pl.program_id(0)
    shape = (GRID_W, LANES)
    q = lax.broadcasted_iota(jnp.int32, shape, 0)
    lane = lax.broadcasted_iota(jnp.int32, shape, 1)
    kc = lane & (GRID_W - 1)
    dc = kc - q + (NA_WIN_COLS - 1)
    cs = jnp.clip(q - NA_WIN_COLS // 2, 0, GRID_W - NA_WIN_COLS)
    valid = (kc >= cs) & (kc < cs + NA_WIN_COLS)
    upper = lane >= GRID_W
    n_dr, n_dc = 2 * NA_WIN_ROWS - 1, 2 * NA_WIN_COLS - 1
    tiles = []
    for dr in range(n_dr):
        acc = jnp.zeros(shape, F32)
        for d in range(n_dc):
            acc = jnp.where(dc == d, rpb_ref[(h * n_dr + dr) * n_dc + d], acc)
        tiles.append(jnp.where(valid, acc, NEG_INF))
    for t in range(NA_BIAS_TYPES):
        for c in range(NA_KEYS // LANES):
            dr0 = 2 * c + NA_WIN_ROWS - 1 - t
            o_ref[t, :, c * LANES:(c + 1) * LANES] = jnp.where(upper, tiles[dr0 + 1], tiles[dr0])


def _na_bias(rpb):
    return pl.pallas_call(
        _na_bias_kernel,
        grid=(NA_HEADS,),
        in_specs=[pl.BlockSpec(memory_space=pltpu.SMEM)],
        out_specs=pl.BlockSpec((NA_BIAS_TYPES, None, GRID_W, NA_KEYS), lambda h: (0, h, 0, 0)),
        out_shape=jax.ShapeDtypeStruct((NA_BIAS_TYPES, NA_HEADS, GRID_W, NA_KEYS), F32),
        compiler_params=_cparams("arbitrary"),
        name="na_bias",
    )(rpb.reshape(-1))


def _na_kernel(q_ref, k_ref, v_ref, g_ref, bias_ref, o_ref, *, n_rows):
    i = pl.program_id(1)
    lane = lax.broadcasted_iota(jnp.int32, (1, LANES), 1)
    head_mask = [(lane < NA_HEAD_DIM).astype(BF16), (lane >= NA_HEAD_DIM).astype(BF16)]
    half = NA_WIN_ROWS // 2
    pair_cols = [slice(p * LANES, (p + 1) * LANES) for p in range(WIDTH_A // LANES)]

    def stage_scores(rr):
        r = i * NA_ROWS_PER_STEP + rr
        row_start = jnp.clip(r - half, 0, n_rows - NA_WIN_ROWS)
        t = jnp.where(r < half, r, jnp.where(r > n_rows - half, r - (n_rows - NA_WIN_ROWS), half))
        koff = pl.multiple_of(row_start * GRID_W, GRID_W)
        rows = slice(rr * GRID_W, (rr + 1) * GRID_W)
        scores = []
        for p, cols in enumerate(pair_cols):
            qp = q_ref[rows, cols]
            qs = jnp.concatenate([qp * head_mask[0], qp * head_mask[1]], axis=0)
            kp = k_ref[pl.ds(koff, NA_KEYS), cols]
            scores.append(_dot_nt(qs, kp) + bias_ref[t, p])
        return rows, koff, scores

    def stage_out(rows, koff, scores):
        probs, norms = [], []
        for s in scores:
            e = jnp.exp(s - jnp.max(s, axis=-1, keepdims=True))
            norms.append(jnp.sum(e, axis=-1, keepdims=True))
            probs.append(e.astype(BF16))
        for p, cols in enumerate(pair_cols):
            vp = v_ref[pl.ds(koff, NA_KEYS), cols]
            o2 = _dot(probs[p], vp) / norms[p]
            o = jnp.where(lane < NA_HEAD_DIM, o2[:GRID_W], o2[GRID_W:])
            o_ref[rows, cols] = (o * _silu(g_ref[rows, cols])).astype(BF16)

    pending = None
    for rr in range(NA_ROWS_PER_STEP):
        cur = stage_scores(rr)
        if pending is not None:
            stage_out(*pending)
        pending = cur
    stage_out(*pending)


def _neighbourhood_attention(aq, ak, av, ag, bias):
    bsz, seq, w = aq.shape
    n_rows = seq // GRID_W
    tq = NA_ROWS_PER_STEP * GRID_W
    bias = bias.reshape(NA_BIAS_TYPES, NA_HEADS // 2, 2 * GRID_W, NA_KEYS)
    row = pl.BlockSpec((None, tq, w), lambda b, i: (b, i, 0))
    full = pl.BlockSpec((None, seq, w), lambda b, i: (b, 0, 0))
    return pl.pallas_call(
        functools.partial(_na_kernel, n_rows=n_rows),
        grid=(bsz, n_rows // NA_ROWS_PER_STEP),
        in_specs=[row, full, full, row,
                  pl.BlockSpec(bias.shape, lambda b, i: (0, 0, 0, 0))],
        out_specs=row,
        out_shape=jax.ShapeDtypeStruct((bsz, seq, w), BF16),
        compiler_params=_cparams("parallel", "arbitrary"),
        name="na_attn",
    )(aq, ak, av, ag, bias)


def _rec_consts(chunk, reverse):
    nlev = int(math.log2(chunk))
    assert 1 << nlev == chunk
    w = np.zeros(((nlev + 2), chunk, chunk), np.float32)
    lv = np.full((chunk, chunk), -1, np.int32)
    idx = np.arange(chunk)
    for l in range(nlev):
        s = 1 << l
        off = idx % (2 * s)
        mid = idx - off + s - 1
        for p in range(chunk):
            if off[p] >= s:
                w[l, p, mid[p] + 1:p + 1] = 1.0
            else:
                w[l, p, p + 1:mid[p] + 1] = 1.0
        same = (idx[:, None] // (2 * s)) == (idx[None, :] // (2 * s))
        lv[same & (off[:, None] >= s) & (off[None, :] < s)] = l
    lv[idx, idx] = nlev
    w[nlev] = (idx[None, :] <= idx[:, None])
    if reverse:
        w = w[:, ::-1, ::-1]
        lv = lv[::-1, ::-1]
    w = np.concatenate([w[nlev]] + [w[l] for l in REC_SEL_LEVELS], axis=0)
    return (jnp.asarray(w, BF16), jnp.asarray(np.ascontiguousarray(lv)), nlev)


def _rec_kernel(*refs, final, reverse, heads, dk, dv, nlev):
    chunk = REC_CHUNK
    q_ref, k_ref, lf_ref, v_ref, w_ref, lv_ref = refs[:6]
    if final:
        prev_ref, gate_ref, ng_ref, o_ref, st_ref = refs[6:]
    else:
        o_ref, st_ref = refs[6:]

    @pl.when(pl.program_id(1) == 0)
    def _():
        st_ref[...] = jnp.zeros_like(st_ref)

    lane = lax.broadcasted_iota(jnp.int32, (1, LANES), 1)
    row = lax.broadcasted_iota(jnp.int32, (chunk, LANES), 0)
    per_slab = LANES // dk
    n_slabs = heads * dk // LANES
    lv = lv_ref[...]
    level_masks = [lv == l for l in range(nlev + 1)]
    last = 0 if reverse else chunk - 1

    def is_query_side(l):
        off = row & (2 * (1 << l) - 1)
        return (off < (1 << l)) if reverse else (off >= (1 << l))

    query_side = [is_query_side(l) for l in range(nlev) if (1 << l) < SUBLANES]

    def boundary_exponent(b, l):
        s = 1 << l
        parts = []
        for blk in range(chunk // (2 * s)):
            mid = blk * 2 * s + (s if reverse else s - 1)
            b_mid = jnp.broadcast_to(b[mid:mid + 1, :], (s, LANES))
            lo, hi = b[blk * 2 * s:blk * 2 * s + s], b[blk * 2 * s + s:(blk + 1) * 2 * s]
            parts += [lo - b_mid, b_mid - hi] if reverse else [b_mid - lo, hi - b_mid]
        return jnp.concatenate(parts, axis=0)

    def query_or_key(q, k, l):
        if (1 << l) < SUBLANES:
            return jnp.where(query_side[l], q, k)
        s = 1 << l
        parts = []
        for blk in range(chunk // (2 * s)):
            lo, hi = slice(blk * 2 * s, blk * 2 * s + s), slice(blk * 2 * s + s, (blk + 1) * 2 * s)
            parts += [q[lo], k[hi]] if reverse else [k[lo], q[hi]]
        return jnp.concatenate(parts, axis=0)

    head_masks = [((lane >= hh * dk) & (lane < (hh + 1) * dk)) for hh in range(per_slab)]
    head_masks_bf = [hm.astype(BF16) for hm in head_masks]

    def stage_scale(ci):
        rows = slice(ci * chunk, (ci + 1) * chunk)
        lf_all = lf_ref[rows, :]
        sums = _sel_dot2(w_ref[...], lf_all)
        slabs = []
        for s in range(n_slabs):
            cols = slice(s * LANES, (s + 1) * LANES)
            q = q_ref[rows, cols].astype(F32)
            k = k_ref[rows, cols].astype(F32)
            lf = lf_all[:, cols]
            b = sums[:chunk, cols]
            ys = []
            for l in range(nlev):
                if l == 0:
                    e = jnp.where(query_side[0], lf, 0.0)
                elif l in REC_SEL_LEVELS:
                    i0 = (1 + REC_SEL_LEVELS.index(l)) * chunk
                    e = sums[i0:i0 + chunk, cols]
                else:
                    e = boundary_exponent(b, l)
                ys.append((jnp.exp2(e) * query_or_key(q, k, l)).astype(BF16))
            eb = jnp.exp2(b)
            b_end = jnp.broadcast_to(b[last:last + 1, :], (chunk, LANES))
            slabs.append(dict(
                lhs=ys + [q.astype(BF16)], rhs=ys + [k.astype(BF16)],
                q_in=(eb * q).astype(BF16),
                k_up=(jnp.exp2(b_end - b) * k).astype(BF16),
                decay=eb[last:last + 1]))
        return rows, slabs

    lane2 = lax.broadcasted_iota(jnp.int32, (1, 2 * dv), 1)
    v_first, v_second = (lane2 < dv).astype(BF16), (lane2 >= dv).astype(BF16)
    pair_masks = [jnp.concatenate([m, m], axis=1) for m in level_masks]

    def block_diag(a, b_):
        za, zb = jnp.zeros_like(a), jnp.zeros_like(b_)
        return jnp.concatenate([jnp.concatenate([a, zb], axis=1), jnp.concatenate([za, b_], axis=1)], axis=0)

    def pair_operands(slabs, pr, key, l=None):
        pick = (lambda d: d[key][l]) if l is not None else (lambda d: d[key])
        if per_slab == 2:
            t = pick(slabs[pr])
            return t, jnp.concatenate([t * head_masks_bf[0], t * head_masks_bf[1]], axis=0)
        a, b_ = pick(slabs[2 * pr]), pick(slabs[2 * pr + 1])
        return jnp.concatenate([a, b_], axis=1), block_diag(a, b_)

    def stage_intra(slabs):
        mats = []
        for pr in range(heads // 2):
            a_mat = jnp.zeros((chunk, 2 * chunk), F32)
            for l in range(nlev + 1):
                lhs = pair_operands(slabs, pr, "lhs", l)[0]
                rhs = pair_operands(slabs, pr, "rhs", l)[1]
                a_mat = jnp.where(pair_masks[l], _dot_nt(lhs, rhs), a_mat)
            mats.append(a_mat.astype(BF16))
        return mats

    def stage_out(rows, slabs, mats):
        for pr in range(heads // 2):
            pcols = slice(pr * 2 * dv, (pr + 1) * 2 * dv)
            vp = v_ref[rows, pcols]
            v_bd = jnp.concatenate([vp * v_first, vp * v_second], axis=0)
            q_in = pair_operands(slabs, pr, "q_in")[0]
            k_up = pair_operands(slabs, pr, "k_up")[0]
            if per_slab == 2:
                st = st_ref[pr]
                st_bf = st.astype(BF16)
                st_bd = jnp.concatenate([st_bf * head_masks_bf[0], st_bf * head_masks_bf[1]], axis=0)
            else:
                st_a, st_b = st_ref[2 * pr], st_ref[2 * pr + 1]
                st_bd = block_diag(st_a.astype(BF16), st_b.astype(BF16))
            o = _dot(mats[pr], v_bd) + _dot_nt(q_in, st_bd)
            u = _dot_tn(vp, k_up)
            if per_slab == 2:
                st_ref[pr] = st * slabs[pr]["decay"] + jnp.where(head_masks[0], u[:dv], u[dv:])
            else:
                st_ref[2 * pr] = st_a * slabs[2 * pr]["decay"] + u[:dv, :LANES]
                st_ref[2 * pr + 1] = st_b * slabs[2 * pr + 1]["decay"] + u[dv:, LANES:]
            if final:
                for hh in range(2):
                    ocols = slice((2 * pr + hh) * dv, (2 * pr + hh + 1) * dv)
                    tot = prev_ref[rows, ocols] + o[:, hh * dv:(hh + 1) * dv]
                    ms = jnp.mean(tot * tot, axis=-1, keepdims=True)
                    y = tot * lax.rsqrt(ms + RMS_EPS) * ng_ref[...]
                    o_ref[rows, ocols] = (y * _silu(gate_ref[rows, ocols])).astype(o_ref.dtype)
            else:
                o_ref[rows, pcols] = o

    n_sub = q_ref.shape[0] // chunk
    order = list(range(n_sub - 1, -1, -1) if reverse else range(n_sub))
    scaled, intra = {}, {}
    for t in range(n_sub + 2):
        if t < n_sub:
            scaled[t] = stage_scale(order[t])
        if 0 <= t - 1 < n_sub:
            intra[t - 1] = stage_intra(scaled[t - 1][1])
        if 0 <= t - 2 < n_sub:
            stage_out(*scaled.pop(t - 2), intra.pop(t - 2))


def _gated_recurrence(name, q, k, lf, v, prev, gate, norm_g, *, reverse, heads, dk, dv):
    bsz, seq, _ = q.shape
    chunk = REC_CHUNK
    blk_rows = min(REC_CHUNKS_PER_STEP * chunk, seq)
    n = seq // blk_rows
    final = prev is not None
    w_sel, lv, nlev = _rec_consts(chunk, reverse)
    cidx = (lambda b, i: (b, n - 1 - i, 0)) if reverse else (lambda b, i: (b, i, 0))
    row = lambda width: pl.BlockSpec((None, blk_rows, width), cidx)
    const2 = lambda a: pl.BlockSpec(a.shape, lambda b, i: (0, 0))
    in_specs = [row(heads * dk), row(heads * dk), row(heads * dk), row(heads * dv), const2(w_sel), const2(lv)]
    args = [q, k, lf, v, w_sel, lv]
    if final:
        in_specs += [row(heads * dv), row(heads * dv), pl.BlockSpec((1, dv), lambda b, i: (0, 0))]
        args += [prev, gate, norm_g.reshape(1, dv)]
    return pl.pallas_call(
        functools.partial(_rec_kernel, final=final, reverse=reverse, heads=heads, dk=dk, dv=dv, nlev=nlev),
        grid=(bsz, n),
        in_specs=in_specs,
        out_specs=row(heads * dv),
        out_shape=jax.ShapeDtypeStruct((bsz, seq, heads * dv), BF16 if final else F32),
        scratch_shapes=[pltpu.VMEM((heads * dk // LANES, dv, LANES), F32)],
        compiler_params=_cparams("parallel", "arbitrary"),
        name=f"{name}_{'bwd' if reverse else 'fwd'}",
    )(*args)


def _ssd_consts(chunk, reverse):
    idx = np.arange(chunk)
    if reverse:
        tri = idx[None, :] >= idx[:, None]
        rest = idx[None, :] < idx[:, None]
    else:
        tri = idx[None, :] <= idx[:, None]
        rest = idx[None, :] > idx[:, None]
    sel = np.concatenate([tri, rest], axis=0).astype(np.float32)
    expand = np.zeros((LANES, WIDTH_D), np.float32)
    for h in range(SSD_HEADS):
        expand[h + (SSD_HEADS if reverse else 0), h * SSD_HEAD_DIM:(h + 1) * SSD_HEAD_DIM] = 1.0
    return jnp.asarray(sel, BF16), jnp.asarray(expand, BF16)


def _ssd_kernel(*refs, final, reverse, n_blocks):
    chunk = REC_CHUNK
    if final:
        (xs_ref, bc_ref, dt_ref, dtb_ref, alog_ref, sel_ref, exp_ref,
         prev_ref, dz_ref, dskip_ref, ng_ref, o_ref, st_ref) = refs
    else:
        (x_ref, xprev_ref, xnext_ref, dt_ref, cw_ref, cb_ref, dtb_ref, alog_ref, sel_ref, exp_ref,
         o_ref, xs_ref, bc_ref, ext_ref, st_ref) = refs
    step = pl.program_id(1)
    blk = (n_blocks - 1 - step) if reverse else step
    blk_rows = dt_ref.shape[0]

    @pl.when(step == 0)
    def _():
        st_ref[...] = jnp.zeros_like(st_ref)

    if not final:
        halo = SUBLANES
        ext_ref[0:halo] = jnp.where(blk == 0, 0.0, xprev_ref[...])
        ext_ref[halo:halo + blk_rows] = x_ref[...]
        ext_ref[halo + blk_rows:2 * halo + blk_rows] = jnp.where(blk == n_blocks - 1, 0.0, xnext_ref[...])
        acc = cb_ref[...]
        for tap in range(SSD_CONV):
            o0 = halo - SSD_CONV // 2 + tap
            acc = acc + ext_ref[o0:o0 + blk_rows] * cw_ref[tap:tap + 1]
        xbc = _silu(acc)
        xs_ref[...] = xbc[:, :WIDTH_D]
        bc_ref[...] = xbc[:, WIDTH_D:].astype(BF16)

    gs = SSD_GROUPS * SSD_STATE
    last = 0 if reverse else chunk - 1
    ii = lax.broadcasted_iota(jnp.int32, (chunk, chunk), 0)
    jj = lax.broadcasted_iota(jnp.int32, (chunk, chunk), 1)
    causal = (ii <= jj) if reverse else (ii >= jj)
    lane = lax.broadcasted_iota(jnp.int32, (1, LANES), 1)
    lo_half = lane < SSD_HEAD_DIM
    col0 = SSD_HEADS if reverse else 0
    rep = SSD_HEADS // SSD_GROUPS
    n_pairs = SSD_HEADS // 2
    neg_a = -jnp.exp(alog_ref[...])

    def stage_decay(ci):
        rows = slice(ci * chunk, (ci + 1) * chunk)
        xs = xs_ref[rows, :]
        bm, cm = bc_ref[rows, :gs], bc_ref[rows, gs:]
        dt = _softplus(dt_ref[rows, :] + dtb_ref[...])
        sums = _sel_dot2(sel_ref[...], dt * neg_a)
        a_cum, a_rem = sums[:chunk], sums[chunk:]
        a_cum_t = a_cum.T
        expand = exp_ref[...]
        x_dt = xs * _dot_sel2(dt, expand)
        e_cum = jnp.exp(_dot_sel2(a_cum, expand))
        x_rem = (x_dt * jnp.exp(_dot_sel2(a_rem, expand))).astype(BF16)
        cbs = [_dot_nt(cm[:, g * SSD_STATE:(g + 1) * SSD_STATE], bm[:, g * SSD_STATE:(g + 1) * SSD_STATE])
               for g in range(SSD_GROUPS)]
        mats, stacks = [], []
        for p in range(n_pairs):
            pair = []
            for hh in range(2):
                ch = col0 + 2 * p + hh
                diff = a_cum[:, ch:ch + 1] - a_cum_t[ch:ch + 1, :]
                seg = jnp.exp(jnp.where(causal, diff, NEG_INF))
                pair.append((cbs[(2 * p) // rep] * seg).astype(BF16))
            mats.append(jnp.concatenate(pair, axis=1))
            xp = x_dt[:, p * LANES:(p + 1) * LANES]
            stacks.append(jnp.concatenate([jnp.where(lo_half, xp, 0.0), jnp.where(lo_half, 0.0, xp)],
                                          axis=0).astype(BF16))
        return dict(rows=rows, xs=xs, bm=bm, cm=cm, e_cum=e_cum, x_rem=x_rem, mats=mats, stacks=stacks)

    def stage_out(d):
        rows = d["rows"]
        ys = []
        for p in range(n_pairs):
            g = (2 * p) // rep
            gcols = slice(g * SSD_STATE, (g + 1) * SSD_STATE)
            pcols = slice(p * LANES, (p + 1) * LANES)
            st = st_ref[p]
            y = _dot(d["mats"][p], d["stacks"][p]) + d["e_cum"][:, pcols] * _dot(d["cm"][:, gcols], st.astype(BF16))
            upd = _dot_tn(d["bm"][:, gcols], d["x_rem"][:, pcols])
            st_ref[p] = st * d["e_cum"][last:last + 1, pcols] + upd
            if final:
                y = y + prev_ref[rows, pcols] + dskip_ref[:, pcols] * d["xs"][:, pcols]
                ys.append(y * _silu(dz_ref[rows, pcols]))
            else:
                o_ref[rows, pcols] = y
        if final:
            t = jnp.concatenate(ys, axis=1)
            ms = jnp.mean(t * t, axis=-1, keepdims=True)
            o_ref[rows, :] = (t * lax.rsqrt(ms + RMS_EPS) * ng_ref[...]).astype(o_ref.dtype)

    n_sub = blk_rows // chunk
    order = list(range(n_sub - 1, -1, -1) if reverse else range(n_sub))
    pending = None
    for ci in order:
        cur = stage_decay(ci)
        if pending is not None:
            stage_out(pending)
        pending = cur
    stage_out(pending)


def _ssd(dxbc, dt_raw, conv_w, conv_b, dt_bias, a_log, dz, d_skip, norm_g):
    bsz, seq, nch = dxbc.shape
    chunk = REC_CHUNK
    blk_rows = min(REC_CHUNKS_PER_STEP * chunk, seq)
    n = seq // blk_rows
    hb = blk_rows // SUBLANES
    nb = seq // SUBLANES
    const2 = lambda a: pl.BlockSpec(a.shape, lambda b, i: (0, 0))
    dtb = jnp.zeros((1, LANES), F32).at[0, :2 * SSD_HEADS].set(dt_bias.reshape(-1))
    alog = jnp.full((1, LANES), NEG_INF, F32).at[0, :2 * SSD_HEADS].set(a_log.reshape(-1))
    cb2 = conv_b.reshape(1, nch)
    state = pltpu.VMEM((SSD_HEADS // 2, SSD_STATE, LANES), F32)
    shp = lambda width, dt: jax.ShapeDtypeStruct((bsz, seq, width), dt)

    sel, expand = _ssd_consts(chunk, False)
    row = lambda width: pl.BlockSpec((None, blk_rows, width), lambda b, i: (b, i, 0))
    y_fwd, xs, bc = pl.pallas_call(
        functools.partial(_ssd_kernel, final=False, reverse=False, n_blocks=n),
        grid=(bsz, n),
        in_specs=[row(nch),
                  pl.BlockSpec((None, SUBLANES, nch), lambda b, i: (b, jnp.maximum(i * hb - 1, 0), 0)),
                  pl.BlockSpec((None, SUBLANES, nch), lambda b, i: (b, jnp.minimum((i + 1) * hb, nb - 1), 0)),
                  row(LANES), const2(conv_w), const2(cb2), const2(dtb), const2(alog), const2(sel), const2(expand)],
        out_specs=[row(WIDTH_D), row(WIDTH_D), row(nch - WIDTH_D)],
        out_shape=[shp(WIDTH_D, F32), shp(WIDTH_D, F32), shp(nch - WIDTH_D, BF16)],
        scratch_shapes=[pltpu.VMEM((blk_rows + 2 * SUBLANES, nch), F32), state],
        compiler_params=_cparams("parallel", "arbitrary"),
        name="ssd_fwd",
    )(dxbc, dxbc, dxbc, dt_raw, conv_w, cb2, dtb, alog, sel, expand)

    sel, expand = _ssd_consts(chunk, True)
    row = lambda width: pl.BlockSpec((None, blk_rows, width), lambda b, i: (b, n - 1 - i, 0))
    dsk = jnp.repeat(d_skip, SSD_HEAD_DIM).reshape(1, WIDTH_D)
    ng = norm_g.reshape(1, WIDTH_D)
    return pl.pallas_call(
        functools.partial(_ssd_kernel, final=True, reverse=True, n_blocks=n),
        grid=(bsz, n),
        in_specs=[row(WIDTH_D), row(nch - WIDTH_D), row(LANES), const2(dtb), const2(alog), const2(sel), const2(expand),
                  row(WIDTH_D), row(WIDTH_D), const2(dsk), const2(ng)],
        out_specs=row(WIDTH_D),
        out_shape=shp(WIDTH_D, BF16),
        scratch_shapes=[state],
        compiler_params=_cparams("parallel", "arbitrary"),
        name="ssd_bwd",
    )(xs, bc, dt_raw, dtb, alog, sel, expand, y_fwd, dz, dsk, ng)


def kernel(x, c, ada_w, ada_b, ln_g, ln_b, e_w_in, e_rpb, e_gla_w_up, e_gla_b, e_gla_norm_g, e_w_out,
           o_w_in, hgrn_lb, o_hgrn_norm_g, o_conv_w, o_conv_b, o_dt_bias, o_a_log, o_d_skip,
           o_ssm_norm_g, o_w_out):
    mod = _ada_mod(c, ada_w, ada_b)

    aq, ak, av, ag, bq, bk, bv, bg, lf_f, lf_b = _proj_even(x, mod[0], e_w_in[0], e_gla_w_up[0], e_gla_b[0])
    ya = _neighbourhood_attention(aq, ak, av, ag, _na_bias(e_rpb[0]))
    gla = functools.partial(_gated_recurrence, "gla", heads=GLA_HEADS, dk=GLA_DK, dv=GLA_DV)
    o_fwd = gla(bq, bk, lf_f, bv, None, None, None, reverse=False)
    yb = gla(bq, bk, lf_b, bv, o_fwd, bg, e_gla_norm_g[0], reverse=True)
    x = _out_ln(ya, yb, e_w_out[0], x, mod[0], ln_g[0], ln_b[0])

    cq, ck_f, lf_f, ck_b, lf_b, ci, cg, dz, dxbc, dt_raw = _proj_odd(x, mod[1], o_w_in[0], hgrn_lb)
    hgrn = functools.partial(_gated_recurrence, "hgrn", heads=HGRN_HEADS, dk=HGRN_DIM, dv=HGRN_DIM)
    o_fwd = hgrn(cq, ck_f, lf_f, ci, None, None, None, reverse=False)
    yc = hgrn(cq, ck_b, lf_b, ci, o_fwd, cg, o_hgrn_norm_g[0], reverse=True)
    yd = _ssd(dxbc, dt_raw, o_conv_w[0], o_conv_b[0], o_dt_bias[0], o_a_log[0], dz, o_d_skip[0], o_ssm_norm_g[0])
    return _out_ln(yc, yd, o_w_out[0], x, mod[1], ln_g[1], ln_b[1])
```

```python
import functools
import math

import numpy as np
import jax
import jax.numpy as jnp
from jax import lax
from jax.experimental import pallas as pl
from jax.experimental.pallas import tpu as pltpu

F32 = jnp.float32
BF16 = jnp.bfloat16

GRID_W = 64
NA_HEADS, NA_HEAD_DIM = 8, 64
NA_WIN_ROWS, NA_WIN_COLS = 8, 16
WIDTH_A = NA_HEADS * NA_HEAD_DIM
GLA_HEADS, GLA_DK, GLA_DV, GLA_RANK = 4, 64, 128, 16
GLA_GATE_NORM = 16.0
WIDTH_B = GLA_HEADS * GLA_DV
HGRN_HEADS, HGRN_DIM = 4, 128
WIDTH_C = HGRN_HEADS * HGRN_DIM
SSD_HEADS, SSD_HEAD_DIM, SSD_GROUPS, SSD_STATE, SSD_CONV = 8, 64, 2, 128, 4
WIDTH_D = SSD_HEADS * SSD_HEAD_DIM
SSD_CONV_CH = WIDTH_D + 2 * SSD_GROUPS * SSD_STATE
DEPTH = 2
DEEPNORM_ALPHA = (2 * DEPTH) ** 0.25
LN_EPS = 1e-5
RMS_EPS = 1e-6

LANES = 128
SUBLANES = 8
VMEM_LIMIT = 56 * 1024 * 1024

ROW_TILE = 512
OUT_ROW_TILE = 1024
OUT_ROW_SUBTILE = 256
REC_CHUNK = 128
REC_CHUNKS_PER_STEP = 8
REC_SEL_LEVELS = (1, 2)
NA_ROWS_PER_STEP = 8
NEG_INF = float("-inf")
LOG2_E = math.log2(math.e)


def _cparams(*sem):
    return pltpu.CompilerParams(dimension_semantics=sem, vmem_limit_bytes=VMEM_LIMIT)


def _dot(a, b):
    return jnp.dot(a, b, preferred_element_type=F32)


def _dot_nt(a, b):
    return lax.dot_general(a, b, (((1,), (1,)), ((), ())), preferred_element_type=F32)


def _dot_tn(a, b):
    return lax.dot_general(a, b, (((0,), (0,)), ((), ())), preferred_element_type=F32)


def _sel_dot2(sel, x):
    hi = x.astype(BF16)
    lo = (x - hi.astype(F32)).astype(BF16)
    return _dot(sel, hi) + _dot(sel, lo)


def _dot_sel2(x, sel):
    hi = x.astype(BF16)
    lo = (x - hi.astype(F32)).astype(BF16)
    return _dot(hi, sel) + _dot(lo, sel)


def _sigmoid(x):
    return 1.0 / (1.0 + jnp.exp(-x))


def _silu(x):
    return x * _sigmoid(x)


def _log1pexp_neg_abs(x):
    return jnp.log1p(jnp.exp(-jnp.abs(x)))


def _softplus(x):
    return jnp.maximum(x, 0.0) + _log1pexp_neg_abs(x)


def _ada_kernel(c_ref, w_ref, b_ref, o_ref):
    cond = _silu(c_ref[...]).astype(BF16)
    o_ref[...] = _dot(cond, w_ref[...].astype(BF16)) + b_ref[...]


def _ada_mod(c, ada_w, ada_b):
    bsz, d = c.shape
    bp = -(-bsz // SUBLANES) * SUBLANES
    n3 = ada_w.shape[-1]
    tn = OUT_ROW_TILE
    c_pad = jnp.zeros((bp, d), F32).at[:bsz].set(c)
    out = pl.pallas_call(
        _ada_kernel,
        grid=(DEPTH, n3 // tn),
        in_specs=[pl.BlockSpec((bp, d), lambda l, j: (0, 0)),
                  pl.BlockSpec((None, d, tn), lambda l, j: (l, 0, j)),
                  pl.BlockSpec((None, 1, tn), lambda l, j: (l, 0, j))],
        out_specs=pl.BlockSpec((None, bp, tn), lambda l, j: (l, 0, j)),
        out_shape=jax.ShapeDtypeStruct((DEPTH, bp, n3), F32),
        compiler_params=_cparams("arbitrary", "arbitrary"),
        name="ada_mod",
    )(c_pad, ada_w, ada_b.reshape(DEPTH, 1, n3))
    return out[:, :bsz].reshape(DEPTH, bsz, 1, n3)


def _modulated(x_ref, mod_ref):
    d = x_ref.shape[-1]
    mod = mod_ref[...]
    return (x_ref[...] * (1.0 + mod[:, d:2 * d]) + mod[:, :d]).astype(BF16)


def _stage_weights(wt_ref, wbf_ref):
    n = wt_ref.shape[0]
    n_full = n // LANES * LANES

    @pl.when((pl.program_id(0) == 0) & (pl.program_id(1) == 0))
    def _():
        def body(r, carry):
            rows = pl.ds(pl.multiple_of(r * LANES, LANES), LANES)
            wbf_ref[rows, :] = wt_ref[rows, :].astype(BF16)
            return carry
        lax.fori_loop(0, n_full // LANES, body, 0)
        if n_full < wbf_ref.shape[0]:
            wbf_ref[n_full:, :] = jnp.zeros((wbf_ref.shape[0] - n_full, wbf_ref.shape[1]), BF16)
            wbf_ref[n_full:n, :] = wt_ref[n_full:n, :].astype(BF16)


def _proj_even_kernel(x_ref, mod_ref, wf_ref, wup_ref, gb_ref,
                      aq_ref, ak_ref, av_ref, ag_ref, bq_ref, bk_ref, bv_ref, bg_ref, lff_ref, lfb_ref, h_ref, w_ref):
    _stage_weights(wf_ref, w_ref)
    h_ref[...] = _modulated(x_ref, mod_ref)

    def mm(lo, hi):
        return _dot_nt(h_ref[...], w_ref[lo:hi, :])

    a, kb = WIDTH_A, GLA_HEADS * GLA_DK
    aq_ref[...] = (mm(0, a) * (NA_HEAD_DIM ** -0.5)).astype(BF16)
    o = 4 * a + 2 * kb + 2 * WIDTH_B
    lr = mm(o, o + LANES)
    lr_hi = lr.astype(BF16)
    lr_lo = (lr - lr_hi.astype(F32)).astype(BF16)
    ak_ref[...] = mm(a, 2 * a).astype(BF16)

    def gate_logits(d, out_ref):
        wu = wup_ref[d]
        wu_hi = wu.astype(BF16)
        wu_lo = (wu - wu_hi.astype(F32)).astype(BF16)
        out_ref[...] = _dot(lr_hi, wu_hi) + _dot(lr_lo, wu_hi) + _dot(lr_hi, wu_lo) + gb_ref[d]

    def finish_gate(out_ref):
        z = out_ref[...]
        log_sig = jnp.minimum(z, 0.0) - jnp.log(1.0 + jnp.exp2(jnp.abs(z) * (-LOG2_E)))
        out_ref[...] = log_sig * (LOG2_E / GLA_GATE_NORM)

    gate_logits(0, lff_ref)
    av_ref[...] = mm(2 * a, 3 * a).astype(BF16)
    gate_logits(1, lfb_ref)
    ag_ref[...] = mm(3 * a, 4 * a)
    finish_gate(lff_ref)
    o = 4 * a
    bqk = mm(o, o + 2 * kb)
    bq_ref[...] = (bqk[:, :kb] * (GLA_DK ** -0.5)).astype(BF16)
    bk_ref[...] = bqk[:, kb:].astype(BF16)
    finish_gate(lfb_ref)
    o += 2 * kb
    bv_ref[...] = mm(o, o + WIDTH_B).astype(BF16)
    bg_ref[...] = mm(o + WIDTH_B, o + 2 * WIDTH_B)


def _proj_even(x, mod, w_in, gla_w_up, gla_b):
    bsz, seq, d = x.shape
    kb = GLA_HEADS * GLA_DK
    n_main = 4 * WIDTH_A + 2 * kb + 2 * WIDTH_B
    wt = jnp.swapaxes(w_in, 0, 1)
    wup = jnp.zeros((2, LANES, kb), F32)
    wup = wup.at[0, :GLA_RANK].set(gla_w_up[0]).at[1, GLA_RANK:2 * GLA_RANK].set(gla_w_up[1])
    tm = min(ROW_TILE, seq)
    row = lambda n: pl.BlockSpec((None, tm, n), lambda b, i: (b, i, 0))
    shp = lambda n, dt: jax.ShapeDtypeStruct((bsz, seq, n), dt)
    widths = [(WIDTH_A, BF16), (WIDTH_A, BF16), (WIDTH_A, BF16), (WIDTH_A, F32),
              (kb, BF16), (kb, BF16), (WIDTH_B, BF16), (WIDTH_B, F32), (kb, F32), (kb, F32)]
    return pl.pallas_call(
        _proj_even_kernel,
        grid=(bsz, seq // tm),
        in_specs=[row(d),
                  pl.BlockSpec((None, 1, mod.shape[-1]), lambda b, i: (b, 0, 0)),
                  pl.BlockSpec(wt.shape, lambda b, i: (0, 0), pipeline_mode=pl.Buffered(1)),
                  pl.BlockSpec(wup.shape, lambda b, i: (0, 0, 0)),
                  pl.BlockSpec((2, 1, kb), lambda b, i: (0, 0, 0))],
        out_specs=[row(n) for n, _ in widths],
        out_shape=[shp(n, dt) for n, dt in widths],
        scratch_shapes=[pltpu.VMEM((tm, d), BF16), pltpu.VMEM((n_main + LANES, d), BF16)],
        compiler_params=_cparams("arbitrary", "arbitrary"),
        name="proj_even",
    )(x, mod, wt, wup, gla_b.reshape(2, 1, kb))


def _proj_odd_kernel(x_ref, mod_ref, wf_ref, lbraw_ref,
                     cq_ref, kf_ref, lff_ref, kb_ref, lfb_ref, ci_ref, cg_ref, dz_ref, dxbc_ref, dt_ref, h_ref, w_ref):
    _stage_weights(wf_ref, w_ref)
    h_ref[...] = _modulated(x_ref, mod_ref)

    def mm(lo, hi):
        return _dot_nt(h_ref[...], w_ref[lo:hi, :])

    c = WIDTH_C
    lbr = lbraw_ref[...]
    mx = jnp.maximum(lbr[0:1], lbr[1:2])
    e0, e1 = jnp.exp(lbr[0:1] - mx), jnp.exp(lbr[1:2] - mx)
    lb = e1 / (e0 + e1)
    log_lb, log_ub = jnp.log(lb), jnp.log1p(-lb)
    gate_refs = ((kf_ref, lff_ref), (kb_ref, lfb_ref))
    for d, (_, lf_ref) in enumerate(gate_refs):
        lf_ref[...] = mm((1 + d) * c, (2 + d) * c)

    half = x_ref.shape[0] // 2

    def finish_gate(d, part):
        k_ref, lf_ref = gate_refs[d]
        rows = slice(part * half, (part + 1) * half)
        z = lf_ref[rows, :]
        t = jnp.exp2(jnp.abs(z) * (-LOG2_E))
        u = 1.0 + t
        k_ref[rows, :] = (1.0 - lb) * (jnp.where(z > 0.0, t, 1.0) / u)
        lc = log_ub + (jnp.minimum(z, 0.0) - jnp.log(u))
        w = 1.0 + jnp.exp2(jnp.abs(log_lb - lc) * (-LOG2_E))
        lf_ref[rows, :] = (jnp.maximum(log_lb, lc) + jnp.log(w)) * LOG2_E

    cq_ref[...] = (mm(0, c) * (HGRN_DIM ** -0.5)).astype(BF16)
    finish_gate(0, 0)
    ci_ref[...] = mm(3 * c, 4 * c).astype(BF16)
    finish_gate(0, 1)
    cg_ref[...] = mm(4 * c, 5 * c)
    finish_gate(1, 0)
    o = 5 * c
    dz_ref[...] = mm(o, o + WIDTH_D)
    finish_gate(1, 1)
    o += WIDTH_D
    dxbc_ref[...] = mm(o, o + SSD_CONV_CH)
    o += SSD_CONV_CH
    dt_ref[...] = mm(o, o + LANES)


def _proj_odd(x, mod, w_in, hgrn_lb):
    bsz, seq, d = x.shape
    n_main = 5 * WIDTH_C + WIDTH_D + SSD_CONV_CH
    wt = jnp.swapaxes(w_in, 0, 1)
    tm = min(ROW_TILE, seq)
    row = lambda n: pl.BlockSpec((None, tm, n), lambda b, i: (b, i, 0))
    shp = lambda n, dt: jax.ShapeDtypeStruct((bsz, seq, n), dt)
    widths = [(WIDTH_C, BF16), (WIDTH_C, F32), (WIDTH_C, F32), (WIDTH_C, F32), (WIDTH_C, F32), (WIDTH_C, BF16),
              (WIDTH_C, F32), (WIDTH_D, F32), (SSD_CONV_CH, F32), (LANES, F32)]
    return pl.pallas_call(
        _proj_odd_kernel,
        grid=(bsz, seq // tm),
        in_specs=[row(d),
                  pl.BlockSpec((None, 1, mod.shape[-1]), lambda b, i: (b, 0, 0)),
                  pl.BlockSpec(wt.shape, lambda b, i: (0, 0), pipeline_mode=pl.Buffered(1)),
                  pl.BlockSpec(hgrn_lb.shape, lambda b, i: (0, 0))],
        out_specs=[row(n) for n, _ in widths],
        out_shape=[shp(n, dt) for n, dt in widths],
        scratch_shapes=[pltpu.VMEM((tm, d), BF16), pltpu.VMEM((n_main + LANES, d), BF16)],
        compiler_params=_cparams("arbitrary", "arbitrary"),
        name="proj_odd",
    )(x, mod, wt, hgrn_lb)


def _out_ln_kernel(ya_ref, yb_ref, w_ref, x_ref, mod_ref, g_ref, b_ref, o_ref):
    d = x_ref.shape[-1]
    wa = ya_ref.shape[-1]
    gate = mod_ref[...][:, 2 * d:]
    sub = OUT_ROW_SUBTILE

    def project(r):
        rows = slice(r * sub, (r + 1) * sub)
        return rows, _dot(ya_ref[rows, :], w_ref[:wa, :]) + _dot(yb_ref[rows, :], w_ref[wa:, :])

    def normalise(rows, y):
        t = DEEPNORM_ALPHA * x_ref[rows, :] + gate * y
        mu = jnp.mean(t, axis=-1, keepdims=True)
        tc = t - mu
        var = jnp.mean(tc * tc, axis=-1, keepdims=True)
        o_ref[rows, :] = tc * lax.rsqrt(var + LN_EPS) * g_ref[...] + b_ref[...]

    pending = None
    for r in range(x_ref.shape[0] // sub):
        cur = project(r)
        if pending is not None:
            normalise(*pending)
        pending = cur
    normalise(*pending)


def _out_ln(ya, yb, w_out, x, mod, ln_g, ln_b):
    bsz, seq, d = x.shape
    tm = min(OUT_ROW_TILE, seq)
    w = w_out.astype(BF16)
    row = lambda n: pl.BlockSpec((None, tm, n), lambda b, i: (b, i, 0))
    vec = pl.BlockSpec((1, d), lambda b, i: (0, 0))
    return pl.pallas_call(
        _out_ln_kernel,
        grid=(bsz, seq // tm),
        in_specs=[row(ya.shape[-1]), row(yb.shape[-1]),
                  pl.BlockSpec(w.shape, lambda b, i: (0, 0)),
                  row(d),
                  pl.BlockSpec((None, 1, mod.shape[-1]), lambda b, i: (b, 0, 0)),
                  vec, vec],
        out_specs=row(d),
        out_shape=jax.ShapeDtypeStruct((bsz, seq, d), F32),
        compiler_params=_cparams("parallel", "arbitrary"),
        name="out_ln",
    )(ya, yb, w, x, mod, ln_g.reshape(1, d), ln_b.reshape(1, d))


NA_BIAS_TYPES = NA_WIN_ROWS
NA_KEYS = NA_WIN_ROWS * GRID_W


def _na_bias_kernel(rpb_ref, o_ref):
    h = pl.program_id(0)
    shape = (GRID_W, LANES)
    q = lax.broadcasted_iota(jnp.int32, shape, 0)
    lane = lax.broadcasted_iota(jnp.int32, shape, 1)
    kc = lane & (GRID_W - 1)
    dc = kc - q + (NA_WIN_COLS - 1)
    cs = jnp.clip(q - NA_WIN_COLS // 2, 0, GRID_W - NA_WIN_COLS)
    valid = (kc >= cs) & (kc < cs + NA_WIN_COLS)
    upper = lane >= GRID_W
    n_dr, n_dc = 2 * NA_WIN_ROWS - 1, 2 * NA_WIN_COLS - 1
    tiles = []
    for dr in range(n_dr):
        acc = jnp.zeros(shape, F32)
        for d in range(n_dc):
            acc = jnp.where(dc == d, rpb_ref[(h * n_dr + dr) * n_dc + d], acc)
        tiles.append(jnp.where(valid, acc, NEG_INF))
    for t in range(NA_BIAS_TYPES):
        for c in range(NA_KEYS // LANES):
            dr0 = 2 * c + NA_WIN_ROWS - 1 - t
            o_ref[t, :, c * LANES:(c + 1) * LANES] = jnp.where(upper, tiles[dr0 + 1], tiles[dr0])


def _na_bias(rpb):
    return pl.pallas_call(
        _na_bias_kernel,
        grid=(NA_HEADS,),
        in_specs=[pl.BlockSpec(memory_space=pltpu.SMEM)],
        out_specs=pl.BlockSpec((NA_BIAS_TYPES, None, GRID_W, NA_KEYS), lambda h: (0, h, 0, 0)),
        out_shape=jax.ShapeDtypeStruct((NA_BIAS_TYPES, NA_HEADS, GRID_W, NA_KEYS), F32),
        compiler_params=_cparams("arbitrary"),
        name="na_bias",
    )(rpb.reshape(-1))


def _na_kernel(q_ref, k_ref, v_ref, g_ref, bias_ref, o_ref, *, n_rows):
    i = pl.program_id(1)
    lane = lax.broadcasted_iota(jnp.int32, (1, LANES), 1)
    head_mask = [(lane < NA_HEAD_DIM).astype(BF16), (lane >= NA_HEAD_DIM).astype(BF16)]
    half = NA_WIN_ROWS // 2
    pair_cols = [slice(p * LANES, (p + 1) * LANES) for p in range(WIDTH_A // LANES)]

    def stage_scores(rr):
        r = i * NA_ROWS_PER_STEP + rr
        row_start = jnp.clip(r - half, 0, n_rows - NA_WIN_ROWS)
        t = jnp.where(r < half, r, jnp.where(r > n_rows - half, r - (n_rows - NA_WIN_ROWS), half))
        koff = pl.multiple_of(row_start * GRID_W, GRID_W)
        rows = slice(rr * GRID_W, (rr + 1) * GRID_W)
        scores = []
        for p, cols in enumerate(pair_cols):
            qp = q_ref[rows, cols]
            qs = jnp.concatenate([qp * head_mask[0], qp * head_mask[1]], axis=0)
            kp = k_ref[pl.ds(koff, NA_KEYS), cols]
            scores.append(_dot_nt(qs, kp) + bias_ref[t, p])
        return rows, koff, scores

    def stage_out(rows, koff, scores):
        probs, norms = [], []
        for s in scores:
            e = jnp.exp(s - jnp.max(s, axis=-1, keepdims=True))
            norms.append(jnp.sum(e, axis=-1, keepdims=True))
            probs.append(e.astype(BF16))
        for p, cols in enumerate(pair_cols):
            vp = v_ref[pl.ds(koff, NA_KEYS), cols]
            o2 = _dot(probs[p], vp) / norms[p]
            o = jnp.where(lane < NA_HEAD_DIM, o2[:GRID_W], o2[GRID_W:])
            o_ref[rows, cols] = (o * _silu(g_ref[rows, cols])).astype(BF16)

    pending = None
    for rr in range(NA_ROWS_PER_STEP):
        cur = stage_scores(rr)
        if pending is not None:
            stage_out(*pending)
        pending = cur
    stage_out(*pending)


def _neighbourhood_attention(aq, ak, av, ag, bias):
    bsz, seq, w = aq.shape
    n_rows = seq // GRID_W
    tq = NA_ROWS_PER_STEP * GRID_W
    bias = bias.reshape(NA_BIAS_TYPES, NA_HEADS // 2, 2 * GRID_W, NA_KEYS)
    row = pl.BlockSpec((None, tq, w), lambda b, i: (b, i, 0))
    full = pl.BlockSpec((None, seq, w), lambda b, i: (b, 0, 0))
    return pl.pallas_call(
        functools.partial(_na_kernel, n_rows=n_rows),
        grid=(bsz, n_rows // NA_ROWS_PER_STEP),
        in_specs=[row, full, full, row,
                  pl.BlockSpec(bias.shape, lambda b, i: (0, 0, 0, 0))],
        out_specs=row,
        out_shape=jax.ShapeDtypeStruct((bsz, seq, w), BF16),
        compiler_params=_cparams("parallel", "arbitrary"),
        name="na_attn",
    )(aq, ak, av, ag, bias)


def _rec_consts(chunk, reverse):
    nlev = int(math.log2(chunk))
    assert 1 << nlev == chunk
    w = np.zeros(((nlev + 2), chunk, chunk), np.float32)
    lv = np.full((chunk, chunk), -1, np.int32)
    idx = np.arange(chunk)
    for l in range(nlev):
        s = 1 << l
        off = idx % (2 * s)
        mid = idx - off + s - 1
        for p in range(chunk):
            if off[p] >= s:
                w[l, p, mid[p] + 1:p + 1] = 1.0
            else:
                w[l, p, p + 1:mid[p] + 1] = 1.0
        same = (idx[:, None] // (2 * s)) == (idx[None, :] // (2 * s))
        lv[same & (off[:, None] >= s) & (off[None, :] < s)] = l
    lv[idx, idx] = nlev
    w[nlev] = (idx[None, :] <= idx[:, None])
    if reverse:
        w = w[:, ::-1, ::-1]
        lv = lv[::-1, ::-1]
    w = np.concatenate([w[nlev]] + [w[l] for l in REC_SEL_LEVELS], axis=0)
    return (jnp.asarray(w, BF16), jnp.asarray(np.ascontiguousarray(lv)), nlev)


def _rec_kernel(*refs, final, reverse, heads, dk, dv, nlev):
    chunk = REC_CHUNK
    q_ref, k_ref, lf_ref, v_ref, w_ref, lv_ref = refs[:6]
    if final:
        prev_ref, gate_ref, ng_ref, o_ref, st_ref = refs[6:]
    else:
        o_ref, st_ref = refs[6:]

    @pl.when(pl.program_id(1) == 0)
    def _():
        st_ref[...] = jnp.zeros_like(st_ref)

    lane = lax.broadcasted_iota(jnp.int32, (1, LANES), 1)
    row = lax.broadcasted_iota(jnp.int32, (chunk, LANES), 0)
    per_slab = LANES // dk
    n_slabs = heads * dk // LANES
    lv = lv_ref[...]
    level_masks = [lv == l for l in range(nlev + 1)]
    last = 0 if reverse else chunk - 1

    def is_query_side(l):
        off = row & (2 * (1 << l) - 1)
        return (off < (1 << l)) if reverse else (off >= (1 << l))

    query_side = [is_query_side(l) for l in range(nlev) if (1 << l) < SUBLANES]

    def boundary_exponent(b, l):
        s = 1 << l
        parts = []
        for blk in range(chunk // (2 * s)):
            mid = blk * 2 * s + (s if reverse else s - 1)
            b_mid = jnp.broadcast_to(b[mid:mid + 1, :], (s, LANES))
            lo, hi = b[blk * 2 * s:blk * 2 * s + s], b[blk * 2 * s + s:(blk + 1) * 2 * s]
            parts += [lo - b_mid, b_mid - hi] if reverse else [b_mid - lo, hi - b_mid]
        return jnp.concatenate(parts, axis=0)

    def query_or_key(q, k, l):
        if (1 << l) < SUBLANES:
            return jnp.where(query_side[l], q, k)
        s = 1 << l
        parts = []
        for blk in range(chunk // (2 * s)):
            lo, hi = slice(blk * 2 * s, blk * 2 * s + s), slice(blk * 2 * s + s, (blk + 1) * 2 * s)
            parts += [q[lo], k[hi]] if reverse else [k[lo], q[hi]]
        return jnp.concatenate(parts, axis=0)

    head_masks = [((lane >= hh * dk) & (lane < (hh + 1) * dk)) for hh in range(per_slab)]
    head_masks_bf = [hm.astype(BF16) for hm in head_masks]

    def stage_scale(ci):
        rows = slice(ci * chunk, (ci + 1) * chunk)
        lf_all = lf_ref[rows, :]
        sums = _sel_dot2(w_ref[...], lf_all)
        slabs = []
        for s in range(n_slabs):
            cols = slice(s * LANES, (s + 1) * LANES)
            q = q_ref[rows, cols].astype(F32)
            k = k_ref[rows, cols].astype(F32)
            lf = lf_all[:, cols]
            b = sums[:chunk, cols]
            ys = []
            for l in range(nlev):
                if l == 0:
                    e = jnp.where(query_side[0], lf, 0.0)
                elif l in REC_SEL_LEVELS:
                    i0 = (1 + REC_SEL_LEVELS.index(l)) * chunk
                    e = sums[i0:i0 + chunk, cols]
                else:
                    e = boundary_exponent(b, l)
                ys.append((jnp.exp2(e) * query_or_key(q, k, l)).astype(BF16))
            eb = jnp.exp2(b)
            b_end = jnp.broadcast_to(b[last:last + 1, :], (chunk, LANES))
            slabs.append(dict(
                lhs=ys + [q.astype(BF16)], rhs=ys + [k.astype(BF16)],
                q_in=(eb * q).astype(BF16),
                k_up=(jnp.exp2(b_end - b) * k).astype(BF16),
                decay=eb[last:last + 1]))
        return rows, slabs

    lane2 = lax.broadcasted_iota(jnp.int32, (1, 2 * dv), 1)
    v_first, v_second = (lane2 < dv).astype(BF16), (lane2 >= dv).astype(BF16)
    pair_masks = [jnp.concatenate([m, m], axis=1) for m in level_masks]

    def block_diag(a, b_):
        za, zb = jnp.zeros_like(a), jnp.zeros_like(b_)
        return jnp.concatenate([jnp.concatenate([a, zb], axis=1), jnp.concatenate([za, b_], axis=1)], axis=0)

    def pair_operands(slabs, pr, key, l=None):
        pick = (lambda d: d[key][l]) if l is not None else (lambda d: d[key])
        if per_slab == 2:
            t = pick(slabs[pr])
            return t, jnp.concatenate([t * head_masks_bf[0], t * head_masks_bf[1]], axis=0)
        a, b_ = pick(slabs[2 * pr]), pick(slabs[2 * pr + 1])
        return jnp.concatenate([a, b_], axis=1), block_diag(a, b_)

    def stage_intra(slabs):
        mats = []
        for pr in range(heads // 2):
            a_mat = jnp.zeros((chunk, 2 * chunk), F32)
            for l in range(nlev + 1):
                lhs = pair_operands(slabs, pr, "lhs", l)[0]
                rhs = pair_operands(slabs, pr, "rhs", l)[1]
                a_mat = jnp.where(pair_masks[l], _dot_nt(lhs, rhs), a_mat)
            mats.append(a_mat.astype(BF16))
        return mats

    def stage_out(rows, slabs, mats):
        for pr in range(heads // 2):
            pcols = slice(pr * 2 * dv, (pr + 1) * 2 * dv)
            vp = v_ref[rows, pcols]
            v_bd = jnp.concatenate([vp * v_first, vp * v_second], axis=0)
            q_in = pair_operands(slabs, pr, "q_in")[0]
            k_up = pair_operands(slabs, pr, "k_up")[0]
            if per_slab == 2:
                st = st_ref[pr]
                st_bf = st.astype(BF16)
                st_bd = jnp.concatenate([st_bf * head_masks_bf[0], st_bf * head_masks_bf[1]], axis=0)
            else:
                st_a, st_b = st_ref[2 * pr], st_ref[2 * pr + 1]
                st_bd = block_diag(st_a.astype(BF16), st_b.astype(BF16))
            o = _dot(mats[pr], v_bd) + _dot_nt(q_in, st_bd)
            u = _dot_tn(vp, k_up)
            if per_slab == 2:
                st_ref[pr] = st * slabs[pr]["decay"] + jnp.where(head_masks[0], u[:dv], u[dv:])
            else:
                st_ref[2 * pr] = st_a * slabs[2 * pr]["decay"] + u[:dv, :LANES]
                st_ref[2 * pr + 1] = st_b * slabs[2 * pr + 1]["decay"] + u[dv:, LANES:]
            if final:
                for hh in range(2):
                    ocols = slice((2 * pr + hh) * dv, (2 * pr + hh + 1) * dv)
                    tot = prev_ref[rows, ocols] + o[:, hh * dv:(hh + 1) * dv]
                    ms = jnp.mean(tot * tot, axis=-1, keepdims=True)
                    y = tot * lax.rsqrt(ms + RMS_EPS) * ng_ref[...]
                    o_ref[rows, ocols] = (y * _silu(gate_ref[rows, ocols])).astype(o_ref.dtype)
            else:
                o_ref[rows, pcols] = o

    n_sub = q_ref.shape[0] // chunk
    order = list(range(n_sub - 1, -1, -1) if reverse else range(n_sub))
    pending = None
    for ci in order:
        cur = stage_scale(ci)
        if pending is not None:
            stage_out(*pending, stage_intra(pending[1]))
        pending = cur
    stage_out(*pending, stage_intra(pending[1]))


def _gated_recurrence(name, q, k, lf, v, prev, gate, norm_g, *, reverse, heads, dk, dv):
    bsz, seq, _ = q.shape
    chunk = REC_CHUNK
    blk_rows = min(REC_CHUNKS_PER_STEP * chunk, seq)
    n = seq // blk_rows
    final = prev is not None
    w_sel, lv, nlev = _rec_consts(chunk, reverse)
    cidx = (lambda b, i: (b, n - 1 - i, 0)) if reverse else (lambda b, i: (b, i, 0))
    row = lambda width: pl.BlockSpec((None, blk_rows, width), cidx)
    const2 = lambda a: pl.BlockSpec(a.shape, lambda b, i: (0, 0))
    in_specs = [row(heads * dk), row(heads * dk), row(heads * dk), row(heads * dv), const2(w_sel), const2(lv)]
    args = [q, k, lf, v, w_sel, lv]
    if final:
        in_specs += [row(heads * dv), row(heads * dv), pl.BlockSpec((1, dv), lambda b, i: (0, 0))]
        args += [prev, gate, norm_g.reshape(1, dv)]
    return pl.pallas_call(
        functools.partial(_rec_kernel, final=final, reverse=reverse, heads=heads, dk=dk, dv=dv, nlev=nlev),
        grid=(bsz, n),
        in_specs=in_specs,
        out_specs=row(heads * dv),
        out_shape=jax.ShapeDtypeStruct((bsz, seq, heads * dv), BF16 if final else F32),
        scratch_shapes=[pltpu.VMEM((heads * dk // LANES, dv, LANES), F32)],
        compiler_params=_cparams("parallel", "arbitrary"),
        name=f"{name}_{'bwd' if reverse else 'fwd'}",
    )(*args)


def _ssd_consts(chunk, reverse):
    idx = np.arange(chunk)
    if reverse:
        tri = idx[None, :] >= idx[:, None]
        rest = idx[None, :] < idx[:, None]
    else:
        tri = idx[None, :] <= idx[:, None]
        rest = idx[None, :] > idx[:, None]
    sel = np.concatenate([tri, rest], axis=0).astype(np.float32)
    expand = np.zeros((LANES, WIDTH_D), np.float32)
    for h in range(SSD_HEADS):
        expand[h + (SSD_HEADS if reverse else 0), h * SSD_HEAD_DIM:(h + 1) * SSD_HEAD_DIM] = 1.0
    return jnp.asarray(sel, BF16), jnp.asarray(expand, BF16)


def _ssd_kernel(*refs, final, reverse, n_blocks):
    chunk = REC_CHUNK
    if final:
        (xs_ref, bc_ref, dt_ref, dtb_ref, alog_ref, sel_ref, exp_ref,
         prev_ref, dz_ref, dskip_ref, ng_ref, o_ref, st_ref) = refs
    else:
        (x_ref, xprev_ref, xnext_ref, dt_ref, cw_ref, cb_ref, dtb_ref, alog_ref, sel_ref, exp_ref,
         o_ref, xs_ref, bc_ref, ext_ref, st_ref) = refs
    step = pl.program_id(1)
    blk = (n_blocks - 1 - step) if reverse else step
    blk_rows = dt_ref.shape[0]

    @pl.when(step == 0)
    def _():
        st_ref[...] = jnp.zeros_like(st_ref)

    if not final:
        halo = SUBLANES
        ext_ref[0:halo] = jnp.where(blk == 0, 0.0, xprev_ref[...])
        ext_ref[halo:halo + blk_rows] = x_ref[...]
        ext_ref[halo + blk_rows:2 * halo + blk_rows] = jnp.where(blk == n_blocks - 1, 0.0, xnext_ref[...])
        acc = cb_ref[...]
        for tap in range(SSD_CONV):
            o0 = halo - SSD_CONV // 2 + tap
            acc = acc + ext_ref[o0:o0 + blk_rows] * cw_ref[tap:tap + 1]
        xbc = _silu(acc)
        xs_ref[...] = xbc[:, :WIDTH_D]
        bc_ref[...] = xbc[:, WIDTH_D:].astype(BF16)

    gs = SSD_GROUPS * SSD_STATE
    last = 0 if reverse else chunk - 1
    ii = lax.broadcasted_iota(jnp.int32, (chunk, chunk), 0)
    jj = lax.broadcasted_iota(jnp.int32, (chunk, chunk), 1)
    causal = (ii <= jj) if reverse else (ii >= jj)
    lane = lax.broadcasted_iota(jnp.int32, (1, LANES), 1)
    lo_half = lane < SSD_HEAD_DIM
    col0 = SSD_HEADS if reverse else 0
    rep = SSD_HEADS // SSD_GROUPS
    n_pairs = SSD_HEADS // 2
    neg_a = -jnp.exp(alog_ref[...])

    def stage_decay(ci):
        rows = slice(ci * chunk, (ci + 1) * chunk)
        xs = xs_ref[rows, :]
        bm, cm = bc_ref[rows, :gs], bc_ref[rows, gs:]
        dt = _softplus(dt_ref[rows, :] + dtb_ref[...])
        sums = _sel_dot2(sel_ref[...], dt * neg_a)
        a_cum, a_rem = sums[:chunk], sums[chunk:]
        a_cum_t = a_cum.T
        expand = exp_ref[...]
        x_dt = xs * _dot_sel2(dt, expand)
        e_cum = jnp.exp(_dot_sel2(a_cum, expand))
        x_rem = (x_dt * jnp.exp(_dot_sel2(a_rem, expand))).astype(BF16)
        cbs = [_dot_nt(cm[:, g * SSD_STATE:(g + 1) * SSD_STATE], bm[:, g * SSD_STATE:(g + 1) * SSD_STATE])
               for g in range(SSD_GROUPS)]
        mats, stacks = [], []
        for p in range(n_pairs):
            pair = []
            for hh in range(2):
                ch = col0 + 2 * p + hh
                diff = a_cum[:, ch:ch + 1] - a_cum_t[ch:ch + 1, :]
                seg = jnp.exp(jnp.where(causal, diff, NEG_INF))
                pair.append((cbs[(2 * p) // rep] * seg).astype(BF16))
            mats.append(jnp.concatenate(pair, axis=1))
            xp = x_dt[:, p * LANES:(p + 1) * LANES]
            stacks.append(jnp.concatenate([jnp.where(lo_half, xp, 0.0), jnp.where(lo_half, 0.0, xp)],
                                          axis=0).astype(BF16))
        return dict(rows=rows, xs=xs, bm=bm, cm=cm, e_cum=e_cum, x_rem=x_rem, mats=mats, stacks=stacks)

    def stage_out(d):
        rows = d["rows"]
        ys = []
        for p in range(n_pairs):
            g = (2 * p) // rep
            gcols = slice(g * SSD_STATE, (g + 1) * SSD_STATE)
            pcols = slice(p * LANES, (p + 1) * LANES)
            st = st_ref[p]
            y = _dot(d["mats"][p], d["stacks"][p]) + d["e_cum"][:, pcols] * _dot(d["cm"][:, gcols], st.astype(BF16))
            upd = _dot_tn(d["bm"][:, gcols], d["x_rem"][:, pcols])
            st_ref[p] = st * d["e_cum"][last:last + 1, pcols] + upd
            if final:
                y = y + prev_ref[rows, pcols] + dskip_ref[:, pcols] * d["xs"][:, pcols]
                ys.append(y * _silu(dz_ref[rows, pcols]))
            else:
                o_ref[rows, pcols] = y
        if final:
            t = jnp.concatenate(ys, axis=1)
            ms = jnp.mean(t * t, axis=-1, keepdims=True)
            o_ref[rows, :] = (t * lax.rsqrt(ms + RMS_EPS) * ng_ref[...]).astype(o_ref.dtype)

    n_sub = blk_rows // chunk
    order = list(range(n_sub - 1, -1, -1) if reverse else range(n_sub))
    pending = None
    for ci in order:
        cur = stage_decay(ci)
        if pending is not None:
            stage_out(pending)
        pending = cur
    stage_out(pending)


def _ssd(dxbc, dt_raw, conv_w, conv_b, dt_bias, a_log, dz, d_skip, norm_g):
    bsz, seq, nch = dxbc.shape
    chunk = REC_CHUNK
    blk_rows = min(REC_CHUNKS_PER_STEP * chunk, seq)
    n = seq // blk_rows
    hb = blk_rows // SUBLANES
    nb = seq // SUBLANES
    const2 = lambda a: pl.BlockSpec(a.shape, lambda b, i: (0, 0))
    dtb = jnp.zeros((1, LANES), F32).at[0, :2 * SSD_HEADS].set(dt_bias.reshape(-1))
    alog = jnp.full((1, LANES), NEG_INF, F32).at[0, :2 * SSD_HEADS].set(a_log.reshape(-1))
    cb2 = conv_b.reshape(1, nch)
    state = pltpu.VMEM((SSD_HEADS // 2, SSD_STATE, LANES), F32)
    shp = lambda width, dt: jax.ShapeDtypeStruct((bsz, seq, width), dt)

    sel, expand = _ssd_consts(chunk, False)
    row = lambda width: pl.BlockSpec((None, blk_rows, width), lambda b, i: (b, i, 0))
    y_fwd, xs, bc = pl.pallas_call(
        functools.partial(_ssd_kernel, final=False, reverse=False, n_blocks=n),
        grid=(bsz, n),
        in_specs=[row(nch),
                  pl.BlockSpec((None, SUBLANES, nch), lambda b, i: (b, jnp.maximum(i * hb - 1, 0), 0)),
                  pl.BlockSpec((None, SUBLANES, nch), lambda b, i: (b, jnp.minimum((i + 1) * hb, nb - 1), 0)),
                  row(LANES), const2(conv_w), const2(cb2), const2(dtb), const2(alog), const2(sel), const2(expand)],
        out_specs=[row(WIDTH_D), row(WIDTH_D), row(nch - WIDTH_D)],
        out_shape=[shp(WIDTH_D, F32), shp(WIDTH_D, F32), shp(nch - WIDTH_D, BF16)],
        scratch_shapes=[pltpu.VMEM((blk_rows + 2 * SUBLANES, nch), F32), state],
        compiler_params=_cparams("parallel", "arbitrary"),
        name="ssd_fwd",
    )(dxbc, dxbc, dxbc, dt_raw, conv_w, cb2, dtb, alog, sel, expand)

    sel, expand = _ssd_consts(chunk, True)
    row = lambda width: pl.BlockSpec((None, blk_rows, width), lambda b, i: (b, n - 1 - i, 0))
    dsk = jnp.repeat(d_skip, SSD_HEAD_DIM).reshape(1, WIDTH_D)
    ng = norm_g.reshape(1, WIDTH_D)
    return pl.pallas_call(
        functools.partial(_ssd_kernel, final=True, reverse=True, n_blocks=n),
        grid=(bsz, n),
        in_specs=[row(WIDTH_D), row(nch - WIDTH_D), row(LANES), const2(dtb), const2(alog), const2(sel), const2(expand),
                  row(WIDTH_D), row(WIDTH_D), const2(dsk), const2(ng)],
        out_specs=row(WIDTH_D),
        out_shape=shp(WIDTH_D, BF16),
        scratch_shapes=[state],
        compiler_params=_cparams("parallel", "arbitrary"),
        name="ssd_bwd",
    )(xs, bc, dt_raw, dtb, alog, sel, expand, y_fwd, dz, dsk, ng)


def kernel(x, c, ada_w, ada_b, ln_g, ln_b, e_w_in, e_rpb, e_gla_w_up, e_gla_b, e_gla_norm_g, e_w_out,
           o_w_in, hgrn_lb, o_hgrn_norm_g, o_conv_w, o_conv_b, o_dt_bias, o_a_log, o_d_skip,
           o_ssm_norm_g, o_w_out):
    mod = _ada_mod(c, ada_w, ada_b)

    aq, ak, av, ag, bq, bk, bv, bg, lf_f, lf_b = _proj_even(x, mod[0], e_w_in[0], e_gla_w_up[0], e_gla_b[0])
    ya = _neighbourhood_attention(aq, ak, av, ag, _na_bias(e_rpb[0]))
    gla = functools.partial(_gated_recurrence, "gla", heads=GLA_HEADS, dk=GLA_DK, dv=GLA_DV)
    o_fwd = gla(bq, bk, lf_f, bv, None, None, None, reverse=False)
    yb = gla(bq, bk, lf_b, bv, o_fwd, bg, e_gla_norm_g[0], reverse=True)
    x = _out_ln(ya, yb, e_w_out[0], x, mod[0], ln_g[0], ln_b[0])

    cq, ck_f, lf_f, ck_b, lf_b, ci, cg, dz, dxbc, dt_raw = _proj_odd(x, mod[1], o_w_in[0], hgrn_lb)
    hgrn = functools.partial(_gated_recurrence, "hgrn", heads=HGRN_HEADS, dk=HGRN_DIM, dv=HGRN_DIM)
    o_fwd = hgrn(cq, ck_f, lf_f, ci, None, None, None, reverse=False)
    yc = hgrn(cq, ck_b, lf_b, ci, o_fwd, cg, o_hgrn_norm_g[0], reverse=True)
    yd = _ssd(dxbc, dt_raw, o_conv_w[0], o_conv_b[0], o_dt_bias[0], o_a_log[0], dz, o_d_skip[0], o_ssm_norm_g[0])
    return _out_ln(yc, yd, o_w_out[0], x, mod[1], ln_g[1], ln_b[1])
```

```python
import functools
import math

import numpy as np
import jax
import jax.numpy as jnp
from jax import lax
from jax.experimental import pallas as pl
from jax.experimental.pallas import tpu as pltpu

F32 = jnp.float32
BF16 = jnp.bfloat16

GRID_W = 64
NA_HEADS, NA_HEAD_DIM = 8, 64
NA_WIN_ROWS, NA_WIN_COLS = 8, 16
WIDTH_A = NA_HEADS * NA_HEAD_DIM
GLA_HEADS, GLA_DK, GLA_DV, GLA_RANK = 4, 64, 128, 16
GLA_GATE_NORM = 16.0
WIDTH_B = GLA_HEADS * GLA_DV
HGRN_HEADS, HGRN_DIM = 4, 128
WIDTH_C = HGRN_HEADS * HGRN_DIM
SSD_HEADS, SSD_HEAD_DIM, SSD_GROUPS, SSD_STATE, SSD_CONV = 8, 64, 2, 128, 4
WIDTH_D = SSD_HEADS * SSD_HEAD_DIM
SSD_CONV_CH = WIDTH_D + 2 * SSD_GROUPS * SSD_STATE
DEPTH = 2
DEEPNORM_ALPHA = (2 * DEPTH) ** 0.25
LN_EPS = 1e-5
RMS_EPS = 1e-6

LANES = 128
SUBLANES = 8
VMEM_LIMIT = 56 * 1024 * 1024

ROW_TILE = 512
ADA_COL_TILE = 1024
OUT_ROW_TILE = 2048
OUT_ROW_SUBTILE = 256
REC_CHUNK = 128
REC_CHUNKS_PER_STEP = 16
REC_SEL_LEVELS = (1, 2)
NA_ROWS_PER_STEP = 8
NEG_INF = float("-inf")
LOG2_E = math.log2(math.e)


def _cparams(*sem):
    return pltpu.CompilerParams(dimension_semantics=sem, vmem_limit_bytes=VMEM_LIMIT)


def _dot(a, b):
    return jnp.dot(a, b, preferred_element_type=F32)


def _dot_nt(a, b):
    return lax.dot_general(a, b, (((1,), (1,)), ((), ())), preferred_element_type=F32)


def _dot_tn(a, b):
    return lax.dot_general(a, b, (((0,), (0,)), ((), ())), preferred_element_type=F32)


def _sel_dot2(sel, x):
    hi = x.astype(BF16)
    lo = (x - hi.astype(F32)).astype(BF16)
    return _dot(sel, hi) + _dot(sel, lo)


def _dot_sel2(x, sel):
    hi = x.astype(BF16)
    lo = (x - hi.astype(F32)).astype(BF16)
    return _dot(hi, sel) + _dot(lo, sel)


def _sigmoid(x):
    return 1.0 / (1.0 + jnp.exp(-x))


def _silu(x):
    return x * _sigmoid(x)


def _log1pexp_neg_abs(x):
    return jnp.log1p(jnp.exp(-jnp.abs(x)))


def _softplus(x):
    return jnp.maximum(x, 0.0) + _log1pexp_neg_abs(x)


def _ada_kernel(c_ref, w_ref, b_ref, o_ref):
    cond = _silu(c_ref[...]).astype(BF16)
    o_ref[...] = _dot(cond, w_ref[...].astype(BF16)) + b_ref[...]


def _ada_mod(c, ada_w, ada_b):
    bsz, d = c.shape
    bp = -(-bsz // SUBLANES) * SUBLANES
    n3 = ada_w.shape[-1]
    tn = ADA_COL_TILE
    c_pad = jnp.zeros((bp, d), F32).at[:bsz].set(c)
    out = pl.pallas_call(
        _ada_kernel,
        grid=(DEPTH, n3 // tn),
        in_specs=[pl.BlockSpec((bp, d), lambda l, j: (0, 0)),
                  pl.BlockSpec((None, d, tn), lambda l, j: (l, 0, j)),
                  pl.BlockSpec((None, 1, tn), lambda l, j: (l, 0, j))],
        out_specs=pl.BlockSpec((None, bp, tn), lambda l, j: (l, 0, j)),
        out_shape=jax.ShapeDtypeStruct((DEPTH, bp, n3), F32),
        compiler_params=_cparams("arbitrary", "arbitrary"),
        name="ada_mod",
    )(c_pad, ada_w, ada_b.reshape(DEPTH, 1, n3))
    return out[:, :bsz].reshape(DEPTH, bsz, 1, n3)


def _modulated(x_ref, mod_ref):
    d = x_ref.shape[-1]
    mod = mod_ref[...]
    return (x_ref[...] * (1.0 + mod[:, d:2 * d]) + mod[:, :d]).astype(BF16)


def _stage_weights(wt_ref, wbf_ref):
    n = wt_ref.shape[0]
    n_full = n // LANES * LANES

    @pl.when((pl.program_id(0) == 0) & (pl.program_id(1) == 0))
    def _():
        def body(r, carry):
            rows = pl.ds(pl.multiple_of(r * LANES, LANES), LANES)
            wbf_ref[rows, :] = wt_ref[rows, :].astype(BF16)
            return carry
        lax.fori_loop(0, n_full // LANES, body, 0)
        if n_full < wbf_ref.shape[0]:
            wbf_ref[n_full:, :] = jnp.zeros((wbf_ref.shape[0] - n_full, wbf_ref.shape[1]), BF16)
            wbf_ref[n_full:n, :] = wt_ref[n_full:n, :].astype(BF16)


def _proj_even_kernel(x_ref, mod_ref, wf_ref, wup_ref, gb_ref,
                      aq_ref, ak_ref, av_ref, ag_ref, bq_ref, bk_ref, bv_ref, bg_ref, lff_ref, lfb_ref, h_ref, w_ref):
    _stage_weights(wf_ref, w_ref)
    h_ref[...] = _modulated(x_ref, mod_ref)

    def mm(lo, hi):
        return _dot_nt(h_ref[...], w_ref[lo:hi, :])

    a, kb = WIDTH_A, GLA_HEADS * GLA_DK
    aq_ref[...] = (mm(0, a) * (NA_HEAD_DIM ** -0.5)).astype(BF16)
    o = 4 * a + 2 * kb + 2 * WIDTH_B
    lr = mm(o, o + LANES)
    lr_hi = lr.astype(BF16)
    lr_lo = (lr - lr_hi.astype(F32)).astype(BF16)
    ak_ref[...] = mm(a, 2 * a).astype(BF16)

    def gate_logits(d, out_ref):
        wu = wup_ref[d]
        wu_hi = wu.astype(BF16)
        wu_lo = (wu - wu_hi.astype(F32)).astype(BF16)
        out_ref[...] = _dot(lr_hi, wu_hi) + _dot(lr_lo, wu_hi) + _dot(lr_hi, wu_lo) + gb_ref[d]

    def finish_gate(out_ref):
        z = out_ref[...]
        log_sig = jnp.minimum(z, 0.0) - jnp.log(1.0 + jnp.exp2(jnp.abs(z) * (-LOG2_E)))
        out_ref[...] = log_sig * (LOG2_E / GLA_GATE_NORM)

    gate_logits(0, lff_ref)
    av_ref[...] = mm(2 * a, 3 * a).astype(BF16)
    gate_logits(1, lfb_ref)
    ag_ref[...] = mm(3 * a, 4 * a)
    finish_gate(lff_ref)
    o = 4 * a
    bqk = mm(o, o + 2 * kb)
    bq_ref[...] = (bqk[:, :kb] * (GLA_DK ** -0.5)).astype(BF16)
    bk_ref[...] = bqk[:, kb:].astype(BF16)
    finish_gate(lfb_ref)
    o += 2 * kb
    bv_ref[...] = mm(o, o + WIDTH_B).astype(BF16)
    bg_ref[...] = mm(o + WIDTH_B, o + 2 * WIDTH_B)


def _proj_even(x, mod, w_in, gla_w_up, gla_b):
    bsz, seq, d = x.shape
    kb = GLA_HEADS * GLA_DK
    n_main = 4 * WIDTH_A + 2 * kb + 2 * WIDTH_B
    wt = jnp.swapaxes(w_in, 0, 1)
    wup = jnp.zeros((2, LANES, kb), F32)
    wup = wup.at[0, :GLA_RANK].set(gla_w_up[0]).at[1, GLA_RANK:2 * GLA_RANK].set(gla_w_up[1])
    tm = min(ROW_TILE, seq)
    row = lambda n: pl.BlockSpec((None, tm, n), lambda b, i: (b, i, 0))
    shp = lambda n, dt: jax.ShapeDtypeStruct((bsz, seq, n), dt)
    widths = [(WIDTH_A, BF16), (WIDTH_A, BF16), (WIDTH_A, BF16), (WIDTH_A, F32),
              (kb, BF16), (kb, BF16), (WIDTH_B, BF16), (WIDTH_B, F32), (kb, F32), (kb, F32)]
    return pl.pallas_call(
        _proj_even_kernel,
        grid=(bsz, seq // tm),
        in_specs=[row(d),
                  pl.BlockSpec((None, 1, mod.shape[-1]), lambda b, i: (b, 0, 0)),
                  pl.BlockSpec(wt.shape, lambda b, i: (0, 0), pipeline_mode=pl.Buffered(1)),
                  pl.BlockSpec(wup.shape, lambda b, i: (0, 0, 0)),
                  pl.BlockSpec((2, 1, kb), lambda b, i: (0, 0, 0))],
        out_specs=[row(n) for n, _ in widths],
        out_shape=[shp(n, dt) for n, dt in widths],
        scratch_shapes=[pltpu.VMEM((tm, d), BF16), pltpu.VMEM((n_main + LANES, d), BF16)],
        compiler_params=_cparams("arbitrary", "arbitrary"),
        name="proj_even",
    )(x, mod, wt, wup, gla_b.reshape(2, 1, kb))


def _proj_odd_kernel(x_ref, mod_ref, wf_ref, lbraw_ref,
                     cq_ref, kf_ref, lff_ref, kb_ref, lfb_ref, ci_ref, cg_ref, dz_ref, dxbc_ref, dt_ref, h_ref, w_ref):
    _stage_weights(wf_ref, w_ref)
    h_ref[...] = _modulated(x_ref, mod_ref)

    def mm(lo, hi):
        return _dot_nt(h_ref[...], w_ref[lo:hi, :])

    c = WIDTH_C
    lbr = lbraw_ref[...]
    mx = jnp.maximum(lbr[0:1], lbr[1:2])
    e0, e1 = jnp.exp(lbr[0:1] - mx), jnp.exp(lbr[1:2] - mx)
    lb = e1 / (e0 + e1)
    log_lb, log_ub = jnp.log(lb), jnp.log1p(-lb)
    gate_refs = ((kf_ref, lff_ref), (kb_ref, lfb_ref))
    for d, (_, lf_ref) in enumerate(gate_refs):
        lf_ref[...] = mm((1 + d) * c, (2 + d) * c)

    half = x_ref.shape[0] // 2

    def finish_gate(d, part):
        k_ref, lf_ref = gate_refs[d]
        rows = slice(part * half, (part + 1) * half)
        z = lf_ref[rows, :]
        t = jnp.exp2(jnp.abs(z) * (-LOG2_E))
        u = 1.0 + t
        k_ref[rows, :] = (1.0 - lb) * (jnp.where(z > 0.0, t, 1.0) / u)
        lc = log_ub + (jnp.minimum(z, 0.0) - jnp.log(u))
        w = 1.0 + jnp.exp2(jnp.abs(log_lb - lc) * (-LOG2_E))
        lf_ref[rows, :] = (jnp.maximum(log_lb, lc) + jnp.log(w)) * LOG2_E

    cq_ref[...] = (mm(0, c) * (HGRN_DIM ** -0.5)).astype(BF16)
    finish_gate(0, 0)
    ci_ref[...] = mm(3 * c, 4 * c).astype(BF16)
    finish_gate(0, 1)
    cg_ref[...] = mm(4 * c, 5 * c)
    finish_gate(1, 0)
    o = 5 * c
    dz_ref[...] = mm(o, o + WIDTH_D)
    finish_gate(1, 1)
    o += WIDTH_D
    dxbc_ref[...] = mm(o, o + SSD_CONV_CH)
    o += SSD_CONV_CH
    dt_ref[...] = mm(o, o + LANES)


def _proj_odd(x, mod, w_in, hgrn_lb):
    bsz, seq, d = x.shape
    n_main = 5 * WIDTH_C + WIDTH_D + SSD_CONV_CH
    wt = jnp.swapaxes(w_in, 0, 1)
    tm = min(ROW_TILE, seq)
    row = lambda n: pl.BlockSpec((None, tm, n), lambda b, i: (b, i, 0))
    shp = lambda n, dt: jax.ShapeDtypeStruct((bsz, seq, n), dt)
    widths = [(WIDTH_C, BF16), (WIDTH_C, F32), (WIDTH_C, F32), (WIDTH_C, F32), (WIDTH_C, F32), (WIDTH_C, BF16),
              (WIDTH_C, F32), (WIDTH_D, F32), (SSD_CONV_CH, F32), (LANES, F32)]
    return pl.pallas_call(
        _proj_odd_kernel,
        grid=(bsz, seq // tm),
        in_specs=[row(d),
                  pl.BlockSpec((None, 1, mod.shape[-1]), lambda b, i: (b, 0, 0)),
                  pl.BlockSpec(wt.shape, lambda b, i: (0, 0), pipeline_mode=pl.Buffered(1)),
                  pl.BlockSpec(hgrn_lb.shape, lambda b, i: (0, 0))],
        out_specs=[row(n) for n, _ in widths],
        out_shape=[shp(n, dt) for n, dt in widths],
        scratch_shapes=[pltpu.VMEM((tm, d), BF16), pltpu.VMEM((n_main + LANES, d), BF16)],
        compiler_params=_cparams("arbitrary", "arbitrary"),
        name="proj_odd",
    )(x, mod, wt, hgrn_lb)


def _out_ln_kernel(ya_ref, yb_ref, w_ref, x_ref, mod_ref, g_ref, b_ref, o_ref):
    d = x_ref.shape[-1]
    wa = ya_ref.shape[-1]
    gate = mod_ref[...][:, 2 * d:]
    sub = OUT_ROW_SUBTILE

    def project(r):
        rows = slice(r * sub, (r + 1) * sub)
        return rows, _dot(ya_ref[rows, :], w_ref[:wa, :]) + _dot(yb_ref[rows, :], w_ref[wa:, :])

    def normalise(rows, y):
        t = DEEPNORM_ALPHA * x_ref[rows, :] + gate * y
        mu = jnp.mean(t, axis=-1, keepdims=True)
        tc = t - mu
        var = jnp.mean(tc * tc, axis=-1, keepdims=True)
        o_ref[rows, :] = tc * lax.rsqrt(var + LN_EPS) * g_ref[...] + b_ref[...]

    pending = None
    for r in range(x_ref.shape[0] // sub):
        cur = project(r)
        if pending is not None:
            normalise(*pending)
        pending = cur
    normalise(*pending)


def _out_ln(ya, yb, w_out, x, mod, ln_g, ln_b):
    bsz, seq, d = x.shape
    tm = min(OUT_ROW_TILE, seq)
    w = w_out.astype(BF16)
    row = lambda n: pl.BlockSpec((None, tm, n), lambda b, i: (b, i, 0))
    vec = pl.BlockSpec((1, d), lambda b, i: (0, 0))
    return pl.pallas_call(
        _out_ln_kernel,
        grid=(bsz, seq // tm),
        in_specs=[row(ya.shape[-1]), row(yb.shape[-1]),
                  pl.BlockSpec(w.shape, lambda b, i: (0, 0)),
                  row(d),
                  pl.BlockSpec((None, 1, mod.shape[-1]), lambda b, i: (b, 0, 0)),
                  vec, vec],
        out_specs=row(d),
        out_shape=jax.ShapeDtypeStruct((bsz, seq, d), F32),
        compiler_params=_cparams("parallel", "arbitrary"),
        name="out_ln",
    )(ya, yb, w, x, mod, ln_g.reshape(1, d), ln_b.reshape(1, d))


NA_BIAS_TYPES = NA_WIN_ROWS
NA_KEYS = NA_WIN_ROWS * GRID_W


def _na_bias_kernel(rpb_ref, o_ref):
    h = pl.program_id(0)
    shape = (GRID_W, LANES)
    q = lax.broadcasted_iota(jnp.int32, shape, 0)
    lane = lax.broadcasted_iota(jnp.int32, shape, 1)
    kc = lane & (GRID_W - 1)
    dc = kc - q + (NA_WIN_COLS - 1)
    cs = jnp.clip(q - NA_WIN_COLS // 2, 0, GRID_W - NA_WIN_COLS)
    valid = (kc >= cs) & (kc < cs + NA_WIN_COLS)
    upper = lane >= GRID_W
    n_dr, n_dc = 2 * NA_WIN_ROWS - 1, 2 * NA_WIN_COLS - 1
    tiles = []
    for dr in range(n_dr):
        acc = jnp.zeros(shape, F32)
        for d in range(n_dc):
            acc = jnp.where(dc == d, rpb_ref[(h * n_dr + dr) * n_dc + d], acc)
        tiles.append(jnp.where(valid, acc, NEG_INF))
    for t in range(NA_BIAS_TYPES):
        for c in range(NA_KEYS // LANES):
            dr0 = 2 * c + NA_WIN_ROWS - 1 - t
            o_ref[t, :, c * LANES:(c + 1) * LANES] = jnp.where(upper, tiles[dr0 + 1], tiles[dr0])


def _na_bias(rpb):
    return pl.pallas_call(
        _na_bias_kernel,
        grid=(NA_HEADS,),
        in_specs=[pl.BlockSpec(memory_space=pltpu.SMEM)],
        out_specs=pl.BlockSpec((NA_BIAS_TYPES, None, GRID_W, NA_KEYS), lambda h: (0, h, 0, 0)),
        out_shape=jax.ShapeDtypeStruct((NA_BIAS_TYPES, NA_HEADS, GRID_W, NA_KEYS), F32),
        compiler_params=_cparams("arbitrary"),
        name="na_bias",
    )(rpb.reshape(-1))


def _na_kernel(q_ref, k_ref, v_ref, g_ref, bias_ref, o_ref, *, n_rows):
    i = pl.program_id(1)
    lane = lax.broadcasted_iota(jnp.int32, (1, LANES), 1)
    head_mask = [(lane < NA_HEAD_DIM).astype(BF16), (lane >= NA_HEAD_DIM).astype(BF16)]
    half = NA_WIN_ROWS // 2
    pair_cols = [slice(p * LANES, (p + 1) * LANES) for p in range(WIDTH_A // LANES)]

    def stage_scores(rr):
        r = i * NA_ROWS_PER_STEP + rr
        row_start = jnp.clip(r - half, 0, n_rows - NA_WIN_ROWS)
        t = jnp.where(r < half, r, jnp.where(r > n_rows - half, r - (n_rows - NA_WIN_ROWS), half))
        koff = pl.multiple_of(row_start * GRID_W, GRID_W)
        rows = slice(rr * GRID_W, (rr + 1) * GRID_W)
        scores = []
        for p, cols in enumerate(pair_cols):
            qp = q_ref[rows, cols]
            qs = jnp.concatenate([qp * head_mask[0], qp * head_mask[1]], axis=0)
            kp = k_ref[pl.ds(koff, NA_KEYS), cols]
            scores.append(_dot_nt(qs, kp) + bias_ref[t, p])
        return rows, koff, scores

    def stage_out(rows, koff, scores):
        probs, norms = [], []
        for s in scores:
            e = jnp.exp(s - jnp.max(s, axis=-1, keepdims=True))
            norms.append(jnp.sum(e, axis=-1, keepdims=True))
            probs.append(e.astype(BF16))
        for p, cols in enumerate(pair_cols):
            vp = v_ref[pl.ds(koff, NA_KEYS), cols]
            o2 = _dot(probs[p], vp) / norms[p]
            o = jnp.where(lane < NA_HEAD_DIM, o2[:GRID_W], o2[GRID_W:])
            o_ref[rows, cols] = (o * _silu(g_ref[rows, cols])).astype(BF16)

    pending = None
    for rr in range(NA_ROWS_PER_STEP):
        cur = stage_scores(rr)
        if pending is not None:
            stage_out(*pending)
        pending = cur
    stage_out(*pending)


def _neighbourhood_attention(aq, ak, av, ag, bias):
    bsz, seq, w = aq.shape
    n_rows = seq // GRID_W
    tq = NA_ROWS_PER_STEP * GRID_W
    bias = bias.reshape(NA_BIAS_TYPES, NA_HEADS // 2, 2 * GRID_W, NA_KEYS)
    row = pl.BlockSpec((None, tq, w), lambda b, i: (b, i, 0))
    full = pl.BlockSpec((None, seq, w), lambda b, i: (b, 0, 0))
    return pl.pallas_call(
        functools.partial(_na_kernel, n_rows=n_rows),
        grid=(bsz, n_rows // NA_ROWS_PER_STEP),
        in_specs=[row, full, full, row,
                  pl.BlockSpec(bias.shape, lambda b, i: (0, 0, 0, 0))],
        out_specs=row,
        out_shape=jax.ShapeDtypeStruct((bsz, seq, w), BF16),
        compiler_params=_cparams("parallel", "arbitrary"),
        name="na_attn",
    )(aq, ak, av, ag, bias)


def _rec_consts(chunk, reverse):
    nlev = int(math.log2(chunk))
    assert 1 << nlev == chunk
    w = np.zeros(((nlev + 2), chunk, chunk), np.float32)
    lv = np.full((chunk, chunk), -1, np.int32)
    idx = np.arange(chunk)
    for l in range(nlev):
        s = 1 << l
        off = idx % (2 * s)
        mid = idx - off + s - 1
        for p in range(chunk):
            if off[p] >= s:
                w[l, p, mid[p] + 1:p + 1] = 1.0
            else:
                w[l, p, p + 1:mid[p] + 1] = 1.0
        same = (idx[:, None] // (2 * s)) == (idx[None, :] // (2 * s))
        lv[same & (off[:, None] >= s) & (off[None, :] < s)] = l
    lv[idx, idx] = nlev
    w[nlev] = (idx[None, :] <= idx[:, None])
    if reverse:
        w = w[:, ::-1, ::-1]
        lv = lv[::-1, ::-1]
    w = np.concatenate([w[nlev]] + [w[l] for l in REC_SEL_LEVELS], axis=0)
    return (jnp.asarray(w, BF16), jnp.asarray(np.ascontiguousarray(lv)), nlev)


def _rec_kernel(*refs, final, reverse, heads, dk, dv, nlev):
    chunk = REC_CHUNK
    q_ref, k_ref, lf_ref, v_ref, w_ref, lv_ref = refs[:6]
    if final:
        prev_ref, gate_ref, ng_ref, o_ref, st_ref = refs[6:]
    else:
        o_ref, st_ref = refs[6:]

    @pl.when(pl.program_id(1) == 0)
    def _():
        st_ref[...] = jnp.zeros_like(st_ref)

    lane = lax.broadcasted_iota(jnp.int32, (1, LANES), 1)
    row = lax.broadcasted_iota(jnp.int32, (chunk, LANES), 0)
    per_slab = LANES // dk
    n_slabs = heads * dk // LANES
    lv = lv_ref[...]
    level_masks = [lv == l for l in range(nlev + 1)]
    last = 0 if reverse else chunk - 1

    def is_query_side(l):
        off = row & (2 * (1 << l) - 1)
        return (off < (1 << l)) if reverse else (off >= (1 << l))

    query_side = [is_query_side(l) for l in range(nlev) if (1 << l) < SUBLANES]

    def boundary_exponent(b, l):
        s = 1 << l
        parts = []
        for blk in range(chunk // (2 * s)):
            mid = blk * 2 * s + (s if reverse else s - 1)
            b_mid = jnp.broadcast_to(b[mid:mid + 1, :], (s, LANES))
            lo, hi = b[blk * 2 * s:blk * 2 * s + s], b[blk * 2 * s + s:(blk + 1) * 2 * s]
            parts += [lo - b_mid, b_mid - hi] if reverse else [b_mid - lo, hi - b_mid]
        return jnp.concatenate(parts, axis=0)

    def query_or_key(q, k, l):
        if (1 << l) < SUBLANES:
            return jnp.where(query_side[l], q, k)
        s = 1 << l
        parts = []
        for blk in range(chunk // (2 * s)):
            lo, hi = slice(blk * 2 * s, blk * 2 * s + s), slice(blk * 2 * s + s, (blk + 1) * 2 * s)
            parts += [q[lo], k[hi]] if reverse else [k[lo], q[hi]]
        return jnp.concatenate(parts, axis=0)

    head_masks = [((lane >= hh * dk) & (lane < (hh + 1) * dk)) for hh in range(per_slab)]
    head_masks_bf = [hm.astype(BF16) for hm in head_masks]

    def stage_scale(ci):
        rows = slice(ci * chunk, (ci + 1) * chunk)
        lf_all = lf_ref[rows, :]
        sums = _sel_dot2(w_ref[...], lf_all)
        slabs = []
        for s in range(n_slabs):
            cols = slice(s * LANES, (s + 1) * LANES)
            q = q_ref[rows, cols].astype(F32)
            k = k_ref[rows, cols].astype(F32)
            lf = lf_all[:, cols]
            b = sums[:chunk, cols]
            ys = []
            for l in range(nlev):
                if l == 0:
                    e = jnp.where(query_side[0], lf, 0.0)
                elif l in REC_SEL_LEVELS:
                    i0 = (1 + REC_SEL_LEVELS.index(l)) * chunk
                    e = sums[i0:i0 + chunk, cols]
                else:
                    e = boundary_exponent(b, l)
                ys.append((jnp.exp2(e) * query_or_key(q, k, l)).astype(BF16))
            eb = jnp.exp2(b)
            b_end = jnp.broadcast_to(b[last:last + 1, :], (chunk, LANES))
            slabs.append(dict(
                lhs=ys + [q.astype(BF16)], rhs=ys + [k.astype(BF16)],
                q_in=(eb * q).astype(BF16),
                k_up=(jnp.exp2(b_end - b) * k).astype(BF16),
                decay=eb[last:last + 1]))
        return rows, slabs

    lane2 = lax.broadcasted_iota(jnp.int32, (1, 2 * dv), 1)
    v_first, v_second = (lane2 < dv).astype(BF16), (lane2 >= dv).astype(BF16)
    pair_masks = [jnp.concatenate([m, m], axis=1) for m in level_masks]

    def block_diag(a, b_):
        za, zb = jnp.zeros_like(a), jnp.zeros_like(b_)
        return jnp.concatenate([jnp.concatenate([a, zb], axis=1), jnp.concatenate([za, b_], axis=1)], axis=0)

    def pair_operands(slabs, pr, key, l=None):
        pick = (lambda d: d[key][l]) if l is not None else (lambda d: d[key])
        if per_slab == 2:
            t = pick(slabs[pr])
            return t, jnp.concatenate([t * head_masks_bf[0], t * head_masks_bf[1]], axis=0)
        a, b_ = pick(slabs[2 * pr]), pick(slabs[2 * pr + 1])
        return jnp.concatenate([a, b_], axis=1), block_diag(a, b_)

    def stage_intra(slabs):
        mats = []
        for pr in range(heads // 2):
            a_mat = jnp.zeros((chunk, 2 * chunk), F32)
            for l in range(nlev + 1):
                lhs = pair_operands(slabs, pr, "lhs", l)[0]
                rhs = pair_operands(slabs, pr, "rhs", l)[1]
                a_mat = jnp.where(pair_masks[l], _dot_nt(lhs, rhs), a_mat)
            mats.append(a_mat.astype(BF16))
        return mats

    def stage_out(rows, slabs, mats):
        for pr in range(heads // 2):
            pcols = slice(pr * 2 * dv, (pr + 1) * 2 * dv)
            vp = v_ref[rows, pcols]
            v_bd = jnp.concatenate([vp * v_first, vp * v_second], axis=0)
            q_in = pair_operands(slabs, pr, "q_in")[0]
            k_up = pair_operands(slabs, pr, "k_up")[0]
            if per_slab == 2:
                st = st_ref[pr]
                st_bf = st.astype(BF16)
                st_bd = jnp.concatenate([st_bf * head_masks_bf[0], st_bf * head_masks_bf[1]], axis=0)
            else:
                st_a, st_b = st_ref[2 * pr], st_ref[2 * pr + 1]
                st_bd = block_diag(st_a.astype(BF16), st_b.astype(BF16))
            o = _dot(mats[pr], v_bd) + _dot_nt(q_in, st_bd)
            u = _dot_tn(vp, k_up)
            if per_slab == 2:
                st_ref[pr] = st * slabs[pr]["decay"] + jnp.where(head_masks[0], u[:dv], u[dv:])
            else:
                st_ref[2 * pr] = st_a * slabs[2 * pr]["decay"] + u[:dv, :LANES]
                st_ref[2 * pr + 1] = st_b * slabs[2 * pr + 1]["decay"] + u[dv:, LANES:]
            if final:
                for hh in range(2):
                    ocols = slice((2 * pr + hh) * dv, (2 * pr + hh + 1) * dv)
                    tot = prev_ref[rows, ocols] + o[:, hh * dv:(hh + 1) * dv]
                    ms = jnp.mean(tot * tot, axis=-1, keepdims=True)
                    y = tot * lax.rsqrt(ms + RMS_EPS) * ng_ref[...]
                    o_ref[rows, ocols] = (y * _silu(gate_ref[rows, ocols])).astype(o_ref.dtype)
            else:
                o_ref[rows, pcols] = o

    n_sub = q_ref.shape[0] // chunk
    order = list(range(n_sub - 1, -1, -1) if reverse else range(n_sub))
    pending = None
    for ci in order:
        cur = stage_scale(ci)
        if pending is not None:
            stage_out(*pending, stage_intra(pending[1]))
        pending = cur
    stage_out(*pending, stage_intra(pending[1]))


def _gated_recurrence(name, q, k, lf, v, prev, gate, norm_g, *, reverse, heads, dk, dv):
    bsz, seq, _ = q.shape
    chunk = REC_CHUNK
    blk_rows = min(REC_CHUNKS_PER_STEP * chunk, seq)
    n = seq // blk_rows
    final = prev is not None
    w_sel, lv, nlev = _rec_consts(chunk, reverse)
    cidx = (lambda b, i: (b, n - 1 - i, 0)) if reverse else (lambda b, i: (b, i, 0))
    row = lambda width: pl.BlockSpec((None, blk_rows, width), cidx)
    const2 = lambda a: pl.BlockSpec(a.shape, lambda b, i: (0, 0))
    in_specs = [row(heads * dk), row(heads * dk), row(heads * dk), row(heads * dv), const2(w_sel), const2(lv)]
    args = [q, k, lf, v, w_sel, lv]
    if final:
        in_specs += [row(heads * dv), row(heads * dv), pl.BlockSpec((1, dv), lambda b, i: (0, 0))]
        args += [prev, gate, norm_g.reshape(1, dv)]
    return pl.pallas_call(
        functools.partial(_rec_kernel, final=final, reverse=reverse, heads=heads, dk=dk, dv=dv, nlev=nlev),
        grid=(bsz, n),
        in_specs=in_specs,
        out_specs=row(heads * dv),
        out_shape=jax.ShapeDtypeStruct((bsz, seq, heads * dv), BF16 if final else F32),
        scratch_shapes=[pltpu.VMEM((heads * dk // LANES, dv, LANES), F32)],
        compiler_params=_cparams("parallel", "arbitrary"),
        name=f"{name}_{'bwd' if reverse else 'fwd'}",
    )(*args)


def _ssd_consts(chunk, reverse):
    idx = np.arange(chunk)
    if reverse:
        tri = idx[None, :] >= idx[:, None]
        rest = idx[None, :] < idx[:, None]
    else:
        tri = idx[None, :] <= idx[:, None]
        rest = idx[None, :] > idx[:, None]
    sel = np.concatenate([tri, rest], axis=0).astype(np.float32)
    expand = np.zeros((LANES, WIDTH_D), np.float32)
    for h in range(SSD_HEADS):
        expand[h + (SSD_HEADS if reverse else 0), h * SSD_HEAD_DIM:(h + 1) * SSD_HEAD_DIM] = 1.0
    return jnp.asarray(sel, BF16), jnp.asarray(expand, BF16)


def _ssd_kernel(*refs, final, reverse, n_blocks):
    chunk = REC_CHUNK
    if final:
        (xs_ref, bc_ref, dt_ref, dtb_ref, alog_ref, sel_ref, exp_ref,
         prev_ref, dz_ref, dskip_ref, ng_ref, o_ref, st_ref) = refs
    else:
        (x_ref, xprev_ref, xnext_ref, dt_ref, cw_ref, cb_ref, dtb_ref, alog_ref, sel_ref, exp_ref,
         o_ref, xs_ref, bc_ref, ext_ref, st_ref) = refs
    step = pl.program_id(1)
    blk = (n_blocks - 1 - step) if reverse else step
    blk_rows = dt_ref.shape[0]

    @pl.when(step == 0)
    def _():
        st_ref[...] = jnp.zeros_like(st_ref)

    if not final:
        halo = SUBLANES
        ext_ref[0:halo] = jnp.where(blk == 0, 0.0, xprev_ref[...])
        ext_ref[halo:halo + blk_rows] = x_ref[...]
        ext_ref[halo + blk_rows:2 * halo + blk_rows] = jnp.where(blk == n_blocks - 1, 0.0, xnext_ref[...])
        acc = cb_ref[...]
        for tap in range(SSD_CONV):
            o0 = halo - SSD_CONV // 2 + tap
            acc = acc + ext_ref[o0:o0 + blk_rows] * cw_ref[tap:tap + 1]
        xbc = _silu(acc)
        xs_ref[...] = xbc[:, :WIDTH_D]
        bc_ref[...] = xbc[:, WIDTH_D:].astype(BF16)

    gs = SSD_GROUPS * SSD_STATE
    last = 0 if reverse else chunk - 1
    ii = lax.broadcasted_iota(jnp.int32, (chunk, chunk), 0)
    jj = lax.broadcasted_iota(jnp.int32, (chunk, chunk), 1)
    causal = (ii <= jj) if reverse else (ii >= jj)
    lane = lax.broadcasted_iota(jnp.int32, (1, LANES), 1)
    lo_half = lane < SSD_HEAD_DIM
    col0 = SSD_HEADS if reverse else 0
    rep = SSD_HEADS // SSD_GROUPS
    n_pairs = SSD_HEADS // 2
    neg_a = -jnp.exp(alog_ref[...])

    def stage_decay(ci):
        rows = slice(ci * chunk, (ci + 1) * chunk)
        xs = xs_ref[rows, :]
        bm, cm = bc_ref[rows, :gs], bc_ref[rows, gs:]
        dt = _softplus(dt_ref[rows, :] + dtb_ref[...])
        sums = _sel_dot2(sel_ref[...], dt * neg_a)
        a_cum, a_rem = sums[:chunk], sums[chunk:]
        a_cum_t = a_cum.T
        expand = exp_ref[...]
        x_dt = xs * _dot_sel2(dt, expand)
        e_cum = jnp.exp(_dot_sel2(a_cum, expand))
        x_rem = (x_dt * jnp.exp(_dot_sel2(a_rem, expand))).astype(BF16)
        cbs = [_dot_nt(cm[:, g * SSD_STATE:(g + 1) * SSD_STATE], bm[:, g * SSD_STATE:(g + 1) * SSD_STATE])
               for g in range(SSD_GROUPS)]
        mats, stacks = [], []
        for p in range(n_pairs):
            pair = []
            for hh in range(2):
                ch = col0 + 2 * p + hh
                diff = a_cum[:, ch:ch + 1] - a_cum_t[ch:ch + 1, :]
                seg = jnp.exp(jnp.where(causal, diff, NEG_INF))
                pair.append((cbs[(2 * p) // rep] * seg).astype(BF16))
            mats.append(jnp.concatenate(pair, axis=1))
            xp = x_dt[:, p * LANES:(p + 1) * LANES]
            stacks.append(jnp.concatenate([jnp.where(lo_half, xp, 0.0), jnp.where(lo_half, 0.0, xp)],
                                          axis=0).astype(BF16))
        return dict(rows=rows, xs=xs, bm=bm, cm=cm, e_cum=e_cum, x_rem=x_rem, mats=mats, stacks=stacks)

    def stage_out(d):
        rows = d["rows"]
        ys = []
        for p in range(n_pairs):
            g = (2 * p) // rep
            gcols = slice(g * SSD_STATE, (g + 1) * SSD_STATE)
            pcols = slice(p * LANES, (p + 1) * LANES)
            st = st_ref[p]
            y = _dot(d["mats"][p], d["stacks"][p]) + d["e_cum"][:, pcols] * _dot(d["cm"][:, gcols], st.astype(BF16))
            upd = _dot_tn(d["bm"][:, gcols], d["x_rem"][:, pcols])
            st_ref[p] = st * d["e_cum"][last:last + 1, pcols] + upd
            if final:
                y = y + prev_ref[rows, pcols] + dskip_ref[:, pcols] * d["xs"][:, pcols]
                ys.append(y * _silu(dz_ref[rows, pcols]))
            else:
                o_ref[rows, pcols] = y
        if final:
            t = jnp.concatenate(ys, axis=1)
            ms = jnp.mean(t * t, axis=-1, keepdims=True)
            o_ref[rows, :] = (t * lax.rsqrt(ms + RMS_EPS) * ng_ref[...]).astype(o_ref.dtype)

    n_sub = blk_rows // chunk
    order = list(range(n_sub - 1, -1, -1) if reverse else range(n_sub))
    pending = None
    for ci in order:
        cur = stage_decay(ci)
        if pending is not None:
            stage_out(pending)
        pending = cur
    stage_out(pending)


def _ssd(dxbc, dt_raw, conv_w, conv_b, dt_bias, a_log, dz, d_skip, norm_g):
    bsz, seq, nch = dxbc.shape
    chunk = REC_CHUNK
    blk_rows = min(REC_CHUNKS_PER_STEP * chunk, seq)
    n = seq // blk_rows
    hb = blk_rows // SUBLANES
    nb = seq // SUBLANES
    const2 = lambda a: pl.BlockSpec(a.shape, lambda b, i: (0, 0))
    dtb = jnp.zeros((1, LANES), F32).at[0, :2 * SSD_HEADS].set(dt_bias.reshape(-1))
    alog = jnp.full((1, LANES), NEG_INF, F32).at[0, :2 * SSD_HEADS].set(a_log.reshape(-1))
    cb2 = conv_b.reshape(1, nch)
    state = pltpu.VMEM((SSD_HEADS // 2, SSD_STATE, LANES), F32)
    shp = lambda width, dt: jax.ShapeDtypeStruct((bsz, seq, width), dt)

    sel, expand = _ssd_consts(chunk, False)
    row = lambda width: pl.BlockSpec((None, blk_rows, width), lambda b, i: (b, i, 0))
    y_fwd, xs, bc = pl.pallas_call(
        functools.partial(_ssd_kernel, final=False, reverse=False, n_blocks=n),
        grid=(bsz, n),
        in_specs=[row(nch),
                  pl.BlockSpec((None, SUBLANES, nch), lambda b, i: (b, jnp.maximum(i * hb - 1, 0), 0)),
                  pl.BlockSpec((None, SUBLANES, nch), lambda b, i: (b, jnp.minimum((i + 1) * hb, nb - 1), 0)),
                  row(LANES), const2(conv_w), const2(cb2), const2(dtb), const2(alog), const2(sel), const2(expand)],
        out_specs=[row(WIDTH_D), row(WIDTH_D), row(nch - WIDTH_D)],
        out_shape=[shp(WIDTH_D, F32), shp(WIDTH_D, F32), shp(nch - WIDTH_D, BF16)],
        scratch_shapes=[pltpu.VMEM((blk_rows + 2 * SUBLANES, nch), F32), state],
        compiler_params=_cparams("parallel", "arbitrary"),
        name="ssd_fwd",
    )(dxbc, dxbc, dxbc, dt_raw, conv_w, cb2, dtb, alog, sel, expand)

    sel, expand = _ssd_consts(chunk, True)
    row = lambda width: pl.BlockSpec((None, blk_rows, width), lambda b, i: (b, n - 1 - i, 0))
    dsk = jnp.repeat(d_skip, SSD_HEAD_DIM).reshape(1, WIDTH_D)
    ng = norm_g.reshape(1, WIDTH_D)
    return pl.pallas_call(
        functools.partial(_ssd_kernel, final=True, reverse=True, n_blocks=n),
        grid=(bsz, n),
        in_specs=[row(WIDTH_D), row(nch - WIDTH_D), row(LANES), const2(dtb), const2(alog), const2(sel), const2(expand),
                  row(WIDTH_D), row(WIDTH_D), const2(dsk), const2(ng)],
        out_specs=row(WIDTH_D),
        out_shape=shp(WIDTH_D, BF16),
        scratch_shapes=[state],
        compiler_params=_cparams("parallel", "arbitrary"),
        name="ssd_bwd",
    )(xs, bc, dt_raw, dtb, alog, sel, expand, y_fwd, dz, dsk, ng)


def kernel(x, c, ada_w, ada_b, ln_g, ln_b, e_w_in, e_rpb, e_gla_w_up, e_gla_b, e_gla_norm_g, e_w_out,
           o_w_in, hgrn_lb, o_hgrn_norm_g, o_conv_w, o_conv_b, o_dt_bias, o_a_log, o_d_skip,
           o_ssm_norm_g, o_w_out):
    mod = _ada_mod(c, ada_w, ada_b)

    aq, ak, av, ag, bq, bk, bv, bg, lf_f, lf_b = _proj_even(x, mod[0], e_w_in[0], e_gla_w_up[0], e_gla_b[0])
    ya = _neighbourhood_attention(aq, ak, av, ag, _na_bias(e_rpb[0]))
    gla = functools.partial(_gated_recurrence, "gla", heads=GLA_HEADS, dk=GLA_DK, dv=GLA_DV)
    o_fwd = gla(bq, bk, lf_f, bv, None, None, None, reverse=False)
    yb = gla(bq, bk, lf_b, bv, o_fwd, bg, e_gla_norm_g[0], reverse=True)
    x = _out_ln(ya, yb, e_w_out[0], x, mod[0], ln_g[0], ln_b[0])

    cq, ck_f, lf_f, ck_b, lf_b, ci, cg, dz, dxbc, dt_raw = _proj_odd(x, mod[1], o_w_in[0], hgrn_lb)
    hgrn = functools.partial(_gated_recurrence, "hgrn", heads=HGRN_HEADS, dk=HGRN_DIM, dv=HGRN_DIM)
    o_fwd = hgrn(cq, ck_f, lf_f, ci, None, None, None, reverse=False)
    yc = hgrn(cq, ck_b, lf_b, ci, o_fwd, cg, o_hgrn_norm_g[0], reverse=True)
    yd = _ssd(dxbc, dt_raw, o_conv_w[0], o_conv_b[0], o_dt_bias[0], o_a_log[0], dz, o_d_skip[0], o_ssm_norm_g[0])
    return _out_ln(yc, yd, o_w_out[0], x, mod[1], ln_g[1], ln_b[1])
```

```python
import functools
import math

import numpy as np
import jax
import jax.numpy as jnp
from jax import lax
from jax.experimental import pallas as pl
from jax.experimental.pallas import tpu as pltpu

F32 = jnp.float32
BF16 = jnp.bfloat16

GRID_W = 64
NA_HEADS, NA_HEAD_DIM = 8, 64
NA_WIN_ROWS, NA_WIN_COLS = 8, 16
WIDTH_A = NA_HEADS * NA_HEAD_DIM
GLA_HEADS, GLA_DK, GLA_DV, GLA_RANK = 4, 64, 128, 16
GLA_GATE_NORM = 16.0
WIDTH_B = GLA_HEADS * GLA_DV
HGRN_HEADS, HGRN_DIM = 4, 128
WIDTH_C = HGRN_HEADS * HGRN_DIM
SSD_HEADS, SSD_HEAD_DIM, SSD_GROUPS, SSD_STATE, SSD_CONV = 8, 64, 2, 128, 4
WIDTH_D = SSD_HEADS * SSD_HEAD_DIM
SSD_CONV_CH = WIDTH_D + 2 * SSD_GROUPS * SSD_STATE
DEPTH = 2
DEEPNORM_ALPHA = (2 * DEPTH) ** 0.25
LN_EPS = 1e-5
RMS_EPS = 1e-6

LANES = 128
SUBLANES = 8
VMEM_LIMIT = 56 * 1024 * 1024

ROW_TILE = 512
ADA_COL_TILE = 1024
OUT_ROW_TILE = 1024
OUT_X_SLOTS = 3
OUT_ROW_SUBTILE = 256
REC_CHUNK = 128
REC_CHUNKS_PER_STEP = 16
REC_SEL_LEVELS = (1, 2)
NA_ROWS_PER_STEP = 8
NEG_INF = float("-inf")
LOG2_E = math.log2(math.e)


def _cparams(*sem):
    return pltpu.CompilerParams(dimension_semantics=sem, vmem_limit_bytes=VMEM_LIMIT)


def _dot(a, b):
    return jnp.dot(a, b, preferred_element_type=F32)


def _dot_nt(a, b):
    return lax.dot_general(a, b, (((1,), (1,)), ((), ())), preferred_element_type=F32)


def _dot_tn(a, b):
    return lax.dot_general(a, b, (((0,), (0,)), ((), ())), preferred_element_type=F32)


def _sel_dot2(sel, x):
    hi = x.astype(BF16)
    lo = (x - hi.astype(F32)).astype(BF16)
    return _dot(sel, hi) + _dot(sel, lo)


def _dot_sel2(x, sel):
    hi = x.astype(BF16)
    lo = (x - hi.astype(F32)).astype(BF16)
    return _dot(hi, sel) + _dot(lo, sel)


def _sigmoid(x):
    return 1.0 / (1.0 + jnp.exp(-x))


def _silu(x):
    return x * _sigmoid(x)


def _log1pexp_neg_abs(x):
    return jnp.log1p(jnp.exp(-jnp.abs(x)))


def _softplus(x):
    return jnp.maximum(x, 0.0) + _log1pexp_neg_abs(x)


def _ada_kernel(c_ref, w_ref, b_ref, o_ref):
    cond = _silu(c_ref[...]).astype(BF16)
    o_ref[...] = _dot(cond, w_ref[...].astype(BF16)) + b_ref[...]


def _ada_mod(c, ada_w, ada_b):
    bsz, d = c.shape
    bp = -(-bsz // SUBLANES) * SUBLANES
    n3 = ada_w.shape[-1]
    tn = ADA_COL_TILE
    c_pad = jnp.zeros((bp, d), F32).at[:bsz].set(c)
    out = pl.pallas_call(
        _ada_kernel,
        grid=(DEPTH, n3 // tn),
        in_specs=[pl.BlockSpec((bp, d), lambda l, j: (0, 0)),
                  pl.BlockSpec((None, d, tn), lambda l, j: (l, 0, j)),
                  pl.BlockSpec((None, 1, tn), lambda l, j: (l, 0, j))],
        out_specs=pl.BlockSpec((None, bp, tn), lambda l, j: (l, 0, j)),
        out_shape=jax.ShapeDtypeStruct((DEPTH, bp, n3), F32),
        compiler_params=_cparams("arbitrary", "arbitrary"),
        name="ada_mod",
    )(c_pad, ada_w, ada_b.reshape(DEPTH, 1, n3))
    return out[:, :bsz].reshape(DEPTH, bsz, 1, n3)


def _modulated(x_ref, mod_ref):
    d = x_ref.shape[-1]
    mod = mod_ref[...]
    return (x_ref[...] * (1.0 + mod[:, d:2 * d]) + mod[:, :d]).astype(BF16)


def _stage_weights(wt_ref, wbf_ref):
    n = wt_ref.shape[0]
    n_full = n // LANES * LANES

    @pl.when((pl.program_id(0) == 0) & (pl.program_id(1) == 0))
    def _():
        def body(r, carry):
            rows = pl.ds(pl.multiple_of(r * LANES, LANES), LANES)
            wbf_ref[rows, :] = wt_ref[rows, :].astype(BF16)
            return carry
        lax.fori_loop(0, n_full // LANES, body, 0)
        if n_full < wbf_ref.shape[0]:
            wbf_ref[n_full:, :] = jnp.zeros((wbf_ref.shape[0] - n_full, wbf_ref.shape[1]), BF16)
            wbf_ref[n_full:n, :] = wt_ref[n_full:n, :].astype(BF16)


def _proj_even_kernel(x_ref, mod_ref, wf_ref, wup_ref, gb_ref,
                      aq_ref, ak_ref, av_ref, ag_ref, bq_ref, bk_ref, bv_ref, bg_ref, lff_ref, lfb_ref, h_ref, w_ref):
    _stage_weights(wf_ref, w_ref)
    h_ref[...] = _modulated(x_ref, mod_ref)

    def mm(lo, hi):
        return _dot_nt(h_ref[...], w_ref[lo:hi, :])

    a, kb = WIDTH_A, GLA_HEADS * GLA_DK
    aq_ref[...] = (mm(0, a) * (NA_HEAD_DIM ** -0.5)).astype(BF16)
    o = 4 * a + 2 * kb + 2 * WIDTH_B
    lr = mm(o, o + LANES)
    lr_hi = lr.astype(BF16)
    lr_lo = (lr - lr_hi.astype(F32)).astype(BF16)
    ak_ref[...] = mm(a, 2 * a).astype(BF16)

    def gate_logits(d, out_ref):
        wu = wup_ref[d]
        wu_hi = wu.astype(BF16)
        wu_lo = (wu - wu_hi.astype(F32)).astype(BF16)
        out_ref[...] = _dot(lr_hi, wu_hi) + _dot(lr_lo, wu_hi) + _dot(lr_hi, wu_lo) + gb_ref[d]

    def finish_gate(out_ref):
        z = out_ref[...]
        log_sig = jnp.minimum(z, 0.0) - jnp.log(1.0 + jnp.exp2(jnp.abs(z) * (-LOG2_E)))
        out_ref[...] = log_sig * (LOG2_E / GLA_GATE_NORM)

    gate_logits(0, lff_ref)
    av_ref[...] = mm(2 * a, 3 * a).astype(BF16)
    gate_logits(1, lfb_ref)
    ag_ref[...] = mm(3 * a, 4 * a)
    finish_gate(lff_ref)
    o = 4 * a
    bqk = mm(o, o + 2 * kb)
    bq_ref[...] = (bqk[:, :kb] * (GLA_DK ** -0.5)).astype(BF16)
    bk_ref[...] = bqk[:, kb:].astype(BF16)
    finish_gate(lfb_ref)
    o += 2 * kb
    bv_ref[...] = mm(o, o + WIDTH_B).astype(BF16)
    bg_ref[...] = mm(o + WIDTH_B, o + 2 * WIDTH_B)


def _proj_even(x, mod, w_in, gla_w_up, gla_b):
    bsz, seq, d = x.shape
    kb = GLA_HEADS * GLA_DK
    n_main = 4 * WIDTH_A + 2 * kb + 2 * WIDTH_B
    wt = jnp.swapaxes(w_in, 0, 1)
    wup = jnp.zeros((2, LANES, kb), F32)
    wup = wup.at[0, :GLA_RANK].set(gla_w_up[0]).at[1, GLA_RANK:2 * GLA_RANK].set(gla_w_up[1])
    tm = min(ROW_TILE, seq)
    row = lambda n: pl.BlockSpec((None, tm, n), lambda b, i: (b, i, 0))
    shp = lambda n, dt: jax.ShapeDtypeStruct((bsz, seq, n), dt)
    widths = [(WIDTH_A, BF16), (WIDTH_A, BF16), (WIDTH_A, BF16), (WIDTH_A, F32),
              (kb, BF16), (kb, BF16), (WIDTH_B, BF16), (WIDTH_B, F32), (kb, F32), (kb, F32)]
    return pl.pallas_call(
        _proj_even_kernel,
        grid=(bsz, seq // tm),
        in_specs=[row(d),
                  pl.BlockSpec((None, 1, mod.shape[-1]), lambda b, i: (b, 0, 0)),
                  pl.BlockSpec(wt.shape, lambda b, i: (0, 0), pipeline_mode=pl.Buffered(1)),
                  pl.BlockSpec(wup.shape, lambda b, i: (0, 0, 0)),
                  pl.BlockSpec((2, 1, kb), lambda b, i: (0, 0, 0))],
        out_specs=[row(n) for n, _ in widths],
        out_shape=[shp(n, dt) for n, dt in widths],
        scratch_shapes=[pltpu.VMEM((tm, d), BF16), pltpu.VMEM((n_main + LANES, d), BF16)],
        compiler_params=_cparams("arbitrary", "arbitrary"),
        name="proj_even",
    )(x, mod, wt, wup, gla_b.reshape(2, 1, kb))


def _proj_odd_kernel(x_ref, mod_ref, wf_ref, lbraw_ref,
                     cq_ref, kf_ref, lff_ref, kb_ref, lfb_ref, ci_ref, cg_ref, dz_ref, dxbc_ref, dt_ref, h_ref, w_ref):
    _stage_weights(wf_ref, w_ref)
    h_ref[...] = _modulated(x_ref, mod_ref)

    def mm(lo, hi):
        return _dot_nt(h_ref[...], w_ref[lo:hi, :])

    c = WIDTH_C
    lbr = lbraw_ref[...]
    mx = jnp.maximum(lbr[0:1], lbr[1:2])
    e0, e1 = jnp.exp(lbr[0:1] - mx), jnp.exp(lbr[1:2] - mx)
    lb = e1 / (e0 + e1)
    log_lb, log_ub = jnp.log(lb), jnp.log1p(-lb)
    gate_refs = ((kf_ref, lff_ref), (kb_ref, lfb_ref))
    for d, (_, lf_ref) in enumerate(gate_refs):
        lf_ref[...] = mm((1 + d) * c, (2 + d) * c)

    half = x_ref.shape[0] // 2

    def finish_gate(d, part):
        k_ref, lf_ref = gate_refs[d]
        rows = slice(part * half, (part + 1) * half)
        z = lf_ref[rows, :]
        t = jnp.exp2(jnp.abs(z) * (-LOG2_E))
        u = 1.0 + t
        k_ref[rows, :] = (1.0 - lb) * (jnp.where(z > 0.0, t, 1.0) / u)
        lc = log_ub + (jnp.minimum(z, 0.0) - jnp.log(u))
        w = 1.0 + jnp.exp2(jnp.abs(log_lb - lc) * (-LOG2_E))
        lf_ref[rows, :] = (jnp.maximum(log_lb, lc) + jnp.log(w)) * LOG2_E

    cq_ref[...] = (mm(0, c) * (HGRN_DIM ** -0.5)).astype(BF16)
    finish_gate(0, 0)
    ci_ref[...] = mm(3 * c, 4 * c).astype(BF16)
    finish_gate(0, 1)
    cg_ref[...] = mm(4 * c, 5 * c)
    finish_gate(1, 0)
    o = 5 * c
    dz_ref[...] = mm(o, o + WIDTH_D)
    finish_gate(1, 1)
    o += WIDTH_D
    dxbc_ref[...] = mm(o, o + SSD_CONV_CH)
    o += SSD_CONV_CH
    dt_ref[...] = mm(o, o + LANES)


def _proj_odd(x, mod, w_in, hgrn_lb):
    bsz, seq, d = x.shape
    n_main = 5 * WIDTH_C + WIDTH_D + SSD_CONV_CH
    wt = jnp.swapaxes(w_in, 0, 1)
    tm = min(ROW_TILE, seq)
    row = lambda n: pl.BlockSpec((None, tm, n), lambda b, i: (b, i, 0))
    shp = lambda n, dt: jax.ShapeDtypeStruct((bsz, seq, n), dt)
    widths = [(WIDTH_C, BF16), (WIDTH_C, F32), (WIDTH_C, F32), (WIDTH_C, F32), (WIDTH_C, F32), (WIDTH_C, BF16),
              (WIDTH_C, F32), (WIDTH_D, F32), (SSD_CONV_CH, F32), (LANES, F32)]
    return pl.pallas_call(
        _proj_odd_kernel,
        grid=(bsz, seq // tm),
        in_specs=[row(d),
                  pl.BlockSpec((None, 1, mod.shape[-1]), lambda b, i: (b, 0, 0)),
                  pl.BlockSpec(wt.shape, lambda b, i: (0, 0), pipeline_mode=pl.Buffered(1)),
                  pl.BlockSpec(hgrn_lb.shape, lambda b, i: (0, 0))],
        out_specs=[row(n) for n, _ in widths],
        out_shape=[shp(n, dt) for n, dt in widths],
        scratch_shapes=[pltpu.VMEM((tm, d), BF16), pltpu.VMEM((n_main + LANES, d), BF16)],
        compiler_params=_cparams("arbitrary", "arbitrary"),
        name="proj_odd",
    )(x, mod, wt, hgrn_lb)


def _out_ln_kernel(ya_ref, yb_ref, w_ref, x_hbm, mod_ref, g_ref, b_ref, o_ref, xbuf, sems, *, n_inner):
    tm, d = xbuf.shape[1:]
    wa = ya_ref.shape[-1]
    gate = mod_ref[...][:, 2 * d:]
    sub = OUT_ROW_SUBTILE
    n_steps = pl.num_programs(0) * n_inner
    step = pl.program_id(0) * n_inner + pl.program_id(1)

    def x_copy(s):
        s = jnp.asarray(s, jnp.int32)
        slot = lax.rem(s, OUT_X_SLOTS)
        rows = pl.ds(pl.multiple_of(lax.rem(s, n_inner) * tm, tm), tm)
        return pltpu.make_async_copy(x_hbm.at[lax.div(s, n_inner), rows, :], xbuf.at[slot], sems.at[slot])

    @pl.when(step == 0)
    def _():
        for s in range(OUT_X_SLOTS - 1):
            x_copy(s).start()

    ahead = step + (OUT_X_SLOTS - 1)

    @pl.when(ahead < n_steps)
    def _():
        x_copy(ahead).start()

    x_copy(step).wait()
    x_ref = xbuf.at[lax.rem(step, OUT_X_SLOTS)]

    def project(r):
        rows = slice(r * sub, (r + 1) * sub)
        return rows, _dot(ya_ref[rows, :], w_ref[:wa, :]) + _dot(yb_ref[rows, :], w_ref[wa:, :])

    def normalise(rows, y):
        t = DEEPNORM_ALPHA * x_ref[rows, :] + gate * y
        mu = jnp.mean(t, axis=-1, keepdims=True)
        tc = t - mu
        var = jnp.mean(tc * tc, axis=-1, keepdims=True)
        o_ref[rows, :] = tc * lax.rsqrt(var + LN_EPS) * g_ref[...] + b_ref[...]

    pending = None
    for r in range(x_ref.shape[0] // sub):
        cur = project(r)
        if pending is not None:
            normalise(*pending)
        pending = cur
    normalise(*pending)


def _out_ln(ya, yb, w_out, x, mod, ln_g, ln_b):
    bsz, seq, d = x.shape
    tm = min(OUT_ROW_TILE, seq)
    n_inner = seq // tm
    assert bsz * n_inner >= OUT_X_SLOTS - 1
    w = w_out.astype(BF16)
    row = lambda n: pl.BlockSpec((None, tm, n), lambda b, i: (b, i, 0))
    vec = pl.BlockSpec((1, d), lambda b, i: (0, 0))
    return pl.pallas_call(
        functools.partial(_out_ln_kernel, n_inner=n_inner),
        grid=(bsz, n_inner),
        in_specs=[row(ya.shape[-1]), row(yb.shape[-1]),
                  pl.BlockSpec(w.shape, lambda b, i: (0, 0)),
                  pl.BlockSpec(memory_space=pl.ANY),
                  pl.BlockSpec((None, 1, mod.shape[-1]), lambda b, i: (b, 0, 0)),
                  vec, vec],
        out_specs=row(d),
        out_shape=jax.ShapeDtypeStruct((bsz, seq, d), F32),
        scratch_shapes=[pltpu.VMEM((OUT_X_SLOTS, tm, d), F32), pltpu.SemaphoreType.DMA((OUT_X_SLOTS,))],
        compiler_params=_cparams("arbitrary", "arbitrary"),
        name="out_ln",
    )(ya, yb, w, x, mod, ln_g.reshape(1, d), ln_b.reshape(1, d))


NA_BIAS_TYPES = NA_WIN_ROWS
NA_KEYS = NA_WIN_ROWS * GRID_W


def _na_bias_kernel(rpb_ref, o_ref):
    h = pl.program_id(0)
    shape = (GRID_W, LANES)
    q = lax.broadcasted_iota(jnp.int32, shape, 0)
    lane = lax.broadcasted_iota(jnp.int32, shape, 1)
    kc = lane & (GRID_W - 1)
    dc = kc - q + (NA_WIN_COLS - 1)
    cs = jnp.clip(q - NA_WIN_COLS // 2, 0, GRID_W - NA_WIN_COLS)
    valid = (kc >= cs) & (kc < cs + NA_WIN_COLS)
    upper = lane >= GRID_W
    n_dr, n_dc = 2 * NA_WIN_ROWS - 1, 2 * NA_WIN_COLS - 1
    tiles = []
    for dr in range(n_dr):
        acc = jnp.zeros(shape, F32)
        for d in range(n_dc):
            acc = jnp.where(dc == d, rpb_ref[(h * n_dr + dr) * n_dc + d], acc)
        tiles.append(jnp.where(valid, acc, NEG_INF))
    for t in range(NA_BIAS_TYPES):
        for c in range(NA_KEYS // LANES):
            dr0 = 2 * c + NA_WIN_ROWS - 1 - t
            o_ref[t, :, c * LANES:(c + 1) * LANES] = jnp.where(upper, tiles[dr0 + 1], tiles[dr0])


def _na_bias(rpb):
    return pl.pallas_call(
        _na_bias_kernel,
        grid=(NA_HEADS,),
        in_specs=[pl.BlockSpec(memory_space=pltpu.SMEM)],
        out_specs=pl.BlockSpec((NA_BIAS_TYPES, None, GRID_W, NA_KEYS), lambda h: (0, h, 0, 0)),
        out_shape=jax.ShapeDtypeStruct((NA_BIAS_TYPES, NA_HEADS, GRID_W, NA_KEYS), F32),
        compiler_params=_cparams("arbitrary"),
        name="na_bias",
    )(rpb.reshape(-1))


def _na_kernel(q_ref, k_ref, v_ref, g_ref, bias_ref, o_ref, *, n_rows):
    i = pl.program_id(1)
    lane = lax.broadcasted_iota(jnp.int32, (1, LANES), 1)
    head_mask = [(lane < NA_HEAD_DIM).astype(BF16), (lane >= NA_HEAD_DIM).astype(BF16)]
    half = NA_WIN_ROWS // 2
    pair_cols = [slice(p * LANES, (p + 1) * LANES) for p in range(WIDTH_A // LANES)]

    def stage_scores(rr):
        r = i * NA_ROWS_PER_STEP + rr
        row_start = jnp.clip(r - half, 0, n_rows - NA_WIN_ROWS)
        t = jnp.where(r < half, r, jnp.where(r > n_rows - half, r - (n_rows - NA_WIN_ROWS), half))
        koff = pl.multiple_of(row_start * GRID_W, GRID_W)
        rows = slice(rr * GRID_W, (rr + 1) * GRID_W)
        scores = []
        for p, cols in enumerate(pair_cols):
            qp = q_ref[rows, cols]
            qs = jnp.concatenate([qp * head_mask[0], qp * head_mask[1]], axis=0)
            kp = k_ref[pl.ds(koff, NA_KEYS), cols]
            scores.append(_dot_nt(qs, kp) + bias_ref[t, p])
        return rows, koff, scores

    def stage_out(rows, koff, scores):
        probs, norms = [], []
        for s in scores:
            e = jnp.exp(s - jnp.max(s, axis=-1, keepdims=True))
            norms.append(jnp.sum(e, axis=-1, keepdims=True))
            probs.append(e.astype(BF16))
        for p, cols in enumerate(pair_cols):
            vp = v_ref[pl.ds(koff, NA_KEYS), cols]
            o2 = _dot(probs[p], vp) / norms[p]
            o = jnp.where(lane < NA_HEAD_DIM, o2[:GRID_W], o2[GRID_W:])
            o_ref[rows, cols] = (o * _silu(g_ref[rows, cols])).astype(BF16)

    pending = None
    for rr in range(NA_ROWS_PER_STEP):
        cur = stage_scores(rr)
        if pending is not None:
            stage_out(*pending)
        pending = cur
    stage_out(*pending)


def _neighbourhood_attention(aq, ak, av, ag, bias):
    bsz, seq, w = aq.shape
    n_rows = seq // GRID_W
    tq = NA_ROWS_PER_STEP * GRID_W
    bias = bias.reshape(NA_BIAS_TYPES, NA_HEADS // 2, 2 * GRID_W, NA_KEYS)
    row = pl.BlockSpec((None, tq, w), lambda b, i: (b, i, 0))
    full = pl.BlockSpec((None, seq, w), lambda b, i: (b, 0, 0))
    return pl.pallas_call(
        functools.partial(_na_kernel, n_rows=n_rows),
        grid=(bsz, n_rows // NA_ROWS_PER_STEP),
        in_specs=[row, full, full, row,
                  pl.BlockSpec(bias.shape, lambda b, i: (0, 0, 0, 0))],
        out_specs=row,
        out_shape=jax.ShapeDtypeStruct((bsz, seq, w), BF16),
        compiler_params=_cparams("parallel", "arbitrary"),
        name="na_attn",
    )(aq, ak, av, ag, bias)


def _rec_consts(chunk, reverse):
    nlev = int(math.log2(chunk))
    assert 1 << nlev == chunk
    w = np.zeros(((nlev + 2), chunk, chunk), np.float32)
    lv = np.full((chunk, chunk), -1, np.int32)
    idx = np.arange(chunk)
    for l in range(nlev):
        s = 1 << l
        off = idx % (2 * s)
        mid = idx - off + s - 1
        for p in range(chunk):
            if off[p] >= s:
                w[l, p, mid[p] + 1:p + 1] = 1.0
            else:
                w[l, p, p + 1:mid[p] + 1] = 1.0
        same = (idx[:, None] // (2 * s)) == (idx[None, :] // (2 * s))
        lv[same & (off[:, None] >= s) & (off[None, :] < s)] = l
    lv[idx, idx] = nlev
    w[nlev] = (idx[None, :] <= idx[:, None])
    if reverse:
        w = w[:, ::-1, ::-1]
        lv = lv[::-1, ::-1]
    w = np.concatenate([w[nlev]] + [w[l] for l in REC_SEL_LEVELS], axis=0)
    return (jnp.asarray(w, BF16), jnp.asarray(np.ascontiguousarray(lv)), nlev)


def _rec_kernel(*refs, final, reverse, heads, dk, dv, nlev):
    chunk = REC_CHUNK
    q_ref, k_ref, lf_ref, v_ref, w_ref, lv_ref = refs[:6]
    if final:
        prev_ref, gate_ref, ng_ref, o_ref, st_ref = refs[6:]
    else:
        o_ref, st_ref = refs[6:]

    @pl.when(pl.program_id(1) == 0)
    def _():
        st_ref[...] = jnp.zeros_like(st_ref)

    lane = lax.broadcasted_iota(jnp.int32, (1, LANES), 1)
    row = lax.broadcasted_iota(jnp.int32, (chunk, LANES), 0)
    per_slab = LANES // dk
    n_slabs = heads * dk // LANES
    lv = lv_ref[...]
    level_masks = [lv == l for l in range(nlev + 1)]
    last = 0 if reverse else chunk - 1

    def is_query_side(l):
        off = row & (2 * (1 << l) - 1)
        return (off < (1 << l)) if reverse else (off >= (1 << l))

    query_side = [is_query_side(l) for l in range(nlev) if (1 << l) < SUBLANES]

    def boundary_exponent(b, l):
        s = 1 << l
        parts = []
        for blk in range(chunk // (2 * s)):
            mid = blk * 2 * s + (s if reverse else s - 1)
            b_mid = jnp.broadcast_to(b[mid:mid + 1, :], (s, LANES))
            lo, hi = b[blk * 2 * s:blk * 2 * s + s], b[blk * 2 * s + s:(blk + 1) * 2 * s]
            parts += [lo - b_mid, b_mid - hi] if reverse else [b_mid - lo, hi - b_mid]
        return jnp.concatenate(parts, axis=0)

    def query_or_key(q, k, l):
        if (1 << l) < SUBLANES:
            return jnp.where(query_side[l], q, k)
        s = 1 << l
        parts = []
        for blk in range(chunk // (2 * s)):
            lo, hi = slice(blk * 2 * s, blk * 2 * s + s), slice(blk * 2 * s + s, (blk + 1) * 2 * s)
            parts += [q[lo], k[hi]] if reverse else [k[lo], q[hi]]
        return jnp.concatenate(parts, axis=0)

    head_masks = [((lane >= hh * dk) & (lane < (hh + 1) * dk)) for hh in range(per_slab)]
    head_masks_bf = [hm.astype(BF16) for hm in head_masks]

    def stage_scale(ci):
        rows = slice(ci * chunk, (ci + 1) * chunk)
        lf_all = lf_ref[rows, :]
        sums = _sel_dot2(w_ref[...], lf_all)
        slabs = []
        for s in range(n_slabs):
            cols = slice(s * LANES, (s + 1) * LANES)
            q = q_ref[rows, cols].astype(F32)
            k = k_ref[rows, cols].astype(F32)
            lf = lf_all[:, cols]
            b = sums[:chunk, cols]
            ys = []
            for l in range(nlev):
                if l == 0:
                    e = jnp.where(query_side[0], lf, 0.0)
                elif l in REC_SEL_LEVELS:
                    i0 = (1 + REC_SEL_LEVELS.index(l)) * chunk
                    e = sums[i0:i0 + chunk, cols]
                else:
                    e = boundary_exponent(b, l)
                ys.append((jnp.exp2(e) * query_or_key(q, k, l)).astype(BF16))
            eb = jnp.exp2(b)
            b_end = jnp.broadcast_to(b[last:last + 1, :], (chunk, LANES))
            slabs.append(dict(
                lhs=ys + [q.astype(BF16)], rhs=ys + [k.astype(BF16)],
                q_in=(eb * q).astype(BF16),
                k_up=(jnp.exp2(b_end - b) * k).astype(BF16),
                decay=eb[last:last + 1]))
        return rows, slabs

    lane2 = lax.broadcasted_iota(jnp.int32, (1, 2 * dv), 1)
    v_first, v_second = (lane2 < dv).astype(BF16), (lane2 >= dv).astype(BF16)
    pair_masks = [jnp.concatenate([m, m], axis=1) for m in level_masks]

    def block_diag(a, b_):
        za, zb = jnp.zeros_like(a), jnp.zeros_like(b_)
        return jnp.concatenate([jnp.concatenate([a, zb], axis=1), jnp.concatenate([za, b_], axis=1)], axis=0)

    def pair_operands(slabs, pr, key, l=None):
        pick = (lambda d: d[key][l]) if l is not None else (lambda d: d[key])
        if per_slab == 2:
            t = pick(slabs[pr])
            return t, jnp.concatenate([t * head_masks_bf[0], t * head_masks_bf[1]], axis=0)
        a, b_ = pick(slabs[2 * pr]), pick(slabs[2 * pr + 1])
        return jnp.concatenate([a, b_], axis=1), block_diag(a, b_)

    def stage_intra(slabs):
        mats = []
        for pr in range(heads // 2):
            a_mat = jnp.zeros((chunk, 2 * chunk), F32)
            for l in range(nlev + 1):
                lhs = pair_operands(slabs, pr, "lhs", l)[0]
                rhs = pair_operands(slabs, pr, "rhs", l)[1]
                a_mat = jnp.where(pair_masks[l], _dot_nt(lhs, rhs), a_mat)
            mats.append(a_mat.astype(BF16))
        return mats

    def stage_out(rows, slabs, mats):
        for pr in range(heads // 2):
            pcols = slice(pr * 2 * dv, (pr + 1) * 2 * dv)
            vp = v_ref[rows, pcols]
            v_bd = jnp.concatenate([vp * v_first, vp * v_second], axis=0)
            q_in = pair_operands(slabs, pr, "q_in")[0]
            k_up = pair_operands(slabs, pr, "k_up")[0]
            if per_slab == 2:
                st = st_ref[pr]
                st_bf = st.astype(BF16)
                st_bd = jnp.concatenate([st_bf * head_masks_bf[0], st_bf * head_masks_bf[1]], axis=0)
            else:
                st_a, st_b = st_ref[2 * pr], st_ref[2 * pr + 1]
                st_bd = block_diag(st_a.astype(BF16), st_b.astype(BF16))
            o = _dot(mats[pr], v_bd) + _dot_nt(q_in, st_bd)
            u = _dot_tn(vp, k_up)
            if per_slab == 2:
                st_ref[pr] = st * slabs[pr]["decay"] + jnp.where(head_masks[0], u[:dv], u[dv:])
            else:
                st_ref[2 * pr] = st_a * slabs[2 * pr]["decay"] + u[:dv, :LANES]
                st_ref[2 * pr + 1] = st_b * slabs[2 * pr + 1]["decay"] + u[dv:, LANES:]
            if final:
                for hh in range(2):
                    ocols = slice((2 * pr + hh) * dv, (2 * pr + hh + 1) * dv)
                    tot = prev_ref[rows, ocols] + o[:, hh * dv:(hh + 1) * dv]
                    ms = jnp.mean(tot * tot, axis=-1, keepdims=True)
                    y = tot * lax.rsqrt(ms + RMS_EPS) * ng_ref[...]
                    o_ref[rows, ocols] = (y * _silu(gate_ref[rows, ocols])).astype(o_ref.dtype)
            else:
                o_ref[rows, pcols] = o

    n_sub = q_ref.shape[0] // chunk
    order = list(range(n_sub - 1, -1, -1) if reverse else range(n_sub))
    pending = None
    for ci in order:
        cur = stage_scale(ci)
        if pending is not None:
            stage_out(*pending, stage_intra(pending[1]))
        pending = cur
    stage_out(*pending, stage_intra(pending[1]))


def _gated_recurrence(name, q, k, lf, v, prev, gate, norm_g, *, reverse, heads, dk, dv):
    bsz, seq, _ = q.shape
    chunk = REC_CHUNK
    blk_rows = min(REC_CHUNKS_PER_STEP * chunk, seq)
    n = seq // blk_rows
    final = prev is not None
    w_sel, lv, nlev = _rec_consts(chunk, reverse)
    cidx = (lambda b, i: (b, n - 1 - i, 0)) if reverse else (lambda b, i: (b, i, 0))
    row = lambda width: pl.BlockSpec((None, blk_rows, width), cidx)
    const2 = lambda a: pl.BlockSpec(a.shape, lambda b, i: (0, 0))
    in_specs = [row(heads * dk), row(heads * dk), row(heads * dk), row(heads * dv), const2(w_sel), const2(lv)]
    args = [q, k, lf, v, w_sel, lv]
    if final:
        in_specs += [row(heads * dv), row(heads * dv), pl.BlockSpec((1, dv), lambda b, i: (0, 0))]
        args += [prev, gate, norm_g.reshape(1, dv)]
    return pl.pallas_call(
        functools.partial(_rec_kernel, final=final, reverse=reverse, heads=heads, dk=dk, dv=dv, nlev=nlev),
        grid=(bsz, n),
        in_specs=in_specs,
        out_specs=row(heads * dv),
        out_shape=jax.ShapeDtypeStruct((bsz, seq, heads * dv), BF16 if final else F32),
        scratch_shapes=[pltpu.VMEM((heads * dk // LANES, dv, LANES), F32)],
        compiler_params=_cparams("parallel", "arbitrary"),
        name=f"{name}_{'bwd' if reverse else 'fwd'}",
    )(*args)


def _ssd_consts(chunk, reverse):
    idx = np.arange(chunk)
    if reverse:
        tri = idx[None, :] >= idx[:, None]
        rest = idx[None, :] < idx[:, None]
    else:
        tri = idx[None, :] <= idx[:, None]
        rest = idx[None, :] > idx[:, None]
    sel = np.concatenate([tri, rest], axis=0).astype(np.float32)
    expand = np.zeros((LANES, WIDTH_D), np.float32)
    for h in range(SSD_HEADS):
        expand[h + (SSD_HEADS if reverse else 0), h * SSD_HEAD_DIM:(h + 1) * SSD_HEAD_DIM] = 1.0
    return jnp.asarray(sel, BF16), jnp.asarray(expand, BF16)


def _ssd_kernel(*refs, final, reverse, n_blocks):
    chunk = REC_CHUNK
    if final:
        (xs_ref, bc_ref, dt_ref, dtb_ref, alog_ref, sel_ref, exp_ref,
         prev_ref, dz_ref, dskip_ref, ng_ref, o_ref, st_ref) = refs
    else:
        (x_ref, xprev_ref, xnext_ref, dt_ref, cw_ref, cb_ref, dtb_ref, alog_ref, sel_ref, exp_ref,
         o_ref, xs_ref, bc_ref, ext_ref, st_ref) = refs
    step = pl.program_id(1)
    blk = (n_blocks - 1 - step) if reverse else step
    blk_rows = dt_ref.shape[0]

    @pl.when(step == 0)
    def _():
        st_ref[...] = jnp.zeros_like(st_ref)

    if not final:
        halo = SUBLANES
        ext_ref[0:halo] = jnp.where(blk == 0, 0.0, xprev_ref[...])
        ext_ref[halo:halo + blk_rows] = x_ref[...]
        ext_ref[halo + blk_rows:2 * halo + blk_rows] = jnp.where(blk == n_blocks - 1, 0.0, xnext_ref[...])
        acc = cb_ref[...]
        for tap in range(SSD_CONV):
            o0 = halo - SSD_CONV // 2 + tap
            acc = acc + ext_ref[o0:o0 + blk_rows] * cw_ref[tap:tap + 1]
        xbc = _silu(acc)
        xs_ref[...] = xbc[:, :WIDTH_D]
        bc_ref[...] = xbc[:, WIDTH_D:].astype(BF16)

    gs = SSD_GROUPS * SSD_STATE
    last = 0 if reverse else chunk - 1
    ii = lax.broadcasted_iota(jnp.int32, (chunk, chunk), 0)
    jj = lax.broadcasted_iota(jnp.int32, (chunk, chunk), 1)
    causal = (ii <= jj) if reverse else (ii >= jj)
    lane = lax.broadcasted_iota(jnp.int32, (1, LANES), 1)
    lo_half = lane < SSD_HEAD_DIM
    col0 = SSD_HEADS if reverse else 0
    rep = SSD_HEADS // SSD_GROUPS
    n_pairs = SSD_HEADS // 2
    neg_a = -jnp.exp(alog_ref[...])

    def stage_decay(ci):
        rows = slice(ci * chunk, (ci + 1) * chunk)
        xs = xs_ref[rows, :]
        bm, cm = bc_ref[rows, :gs], bc_ref[rows, gs:]
        dt = _softplus(dt_ref[rows, :] + dtb_ref[...])
        sums = _sel_dot2(sel_ref[...], dt * neg_a)
        a_cum, a_rem = sums[:chunk], sums[chunk:]
        a_cum_t = a_cum.T
        expand = exp_ref[...]
        x_dt = xs * _dot_sel2(dt, expand)
        e_cum = jnp.exp(_dot_sel2(a_cum, expand))
        x_rem = (x_dt * jnp.exp(_dot_sel2(a_rem, expand))).astype(BF16)
        cbs = [_dot_nt(cm[:, g * SSD_STATE:(g + 1) * SSD_STATE], bm[:, g * SSD_STATE:(g + 1) * SSD_STATE])
               for g in range(SSD_GROUPS)]
        mats, stacks = [], []
        for p in range(n_pairs):
            pair = []
            for hh in range(2):
                ch = col0 + 2 * p + hh
                diff = a_cum[:, ch:ch + 1] - a_cum_t[ch:ch + 1, :]
                seg = jnp.exp(jnp.where(causal, diff, NEG_INF))
                pair.append((cbs[(2 * p) // rep] * seg).astype(BF16))
            mats.append(jnp.concatenate(pair, axis=1))
            xp = x_dt[:, p * LANES:(p + 1) * LANES]
            stacks.append(jnp.concatenate([jnp.where(lo_half, xp, 0.0), jnp.where(lo_half, 0.0, xp)],
                                          axis=0).astype(BF16))
        return dict(rows=rows, xs=xs, bm=bm, cm=cm, e_cum=e_cum, x_rem=x_rem, mats=mats, stacks=stacks)

    def stage_out(d):
        rows = d["rows"]
        ys = []
        for p in range(n_pairs):
            g = (2 * p) // rep
            gcols = slice(g * SSD_STATE, (g + 1) * SSD_STATE)
            pcols = slice(p * LANES, (p + 1) * LANES)
            st = st_ref[p]
            y = _dot(d["mats"][p], d["stacks"][p]) + d["e_cum"][:, pcols] * _dot(d["cm"][:, gcols], st.astype(BF16))
            upd = _dot_tn(d["bm"][:, gcols], d["x_rem"][:, pcols])
            st_ref[p] = st * d["e_cum"][last:last + 1, pcols] + upd
            if final:
                y = y + prev_ref[rows, pcols] + dskip_ref[:, pcols] * d["xs"][:, pcols]
                ys.append(y * _silu(dz_ref[rows, pcols]))
            else:
                o_ref[rows, pcols] = y
        if final:
            t = jnp.concatenate(ys, axis=1)
            ms = jnp.mean(t * t, axis=-1, keepdims=True)
            o_ref[rows, :] = (t * lax.rsqrt(ms + RMS_EPS) * ng_ref[...]).astype(o_ref.dtype)

    n_sub = blk_rows // chunk
    order = list(range(n_sub - 1, -1, -1) if reverse else range(n_sub))
    pending = None
    for ci in order:
        cur = stage_decay(ci)
        if pending is not None:
            stage_out(pending)
        pending = cur
    stage_out(pending)


def _ssd(dxbc, dt_raw, conv_w, conv_b, dt_bias, a_log, dz, d_skip, norm_g):
    bsz, seq, nch = dxbc.shape
    chunk = REC_CHUNK
    blk_rows = min(REC_CHUNKS_PER_STEP * chunk, seq)
    n = seq // blk_rows
    hb = blk_rows // SUBLANES
    nb = seq // SUBLANES
    const2 = lambda a: pl.BlockSpec(a.shape, lambda b, i: (0, 0))
    dtb = jnp.zeros((1, LANES), F32).at[0, :2 * SSD_HEADS].set(dt_bias.reshape(-1))
    alog = jnp.full((1, LANES), NEG_INF, F32).at[0, :2 * SSD_HEADS].set(a_log.reshape(-1))
    cb2 = conv_b.reshape(1, nch)
    state = pltpu.VMEM((SSD_HEADS // 2, SSD_STATE, LANES), F32)
    shp = lambda width, dt: jax.ShapeDtypeStruct((bsz, seq, width), dt)

    sel, expand = _ssd_consts(chunk, False)
    row = lambda width: pl.BlockSpec((None, blk_rows, width), lambda b, i: (b, i, 0))
    y_fwd, xs, bc = pl.pallas_call(
        functools.partial(_ssd_kernel, final=False, reverse=False, n_blocks=n),
        grid=(bsz, n),
        in_specs=[row(nch),
                  pl.BlockSpec((None, SUBLANES, nch), lambda b, i: (b, jnp.maximum(i * hb - 1, 0), 0)),
                  pl.BlockSpec((None, SUBLANES, nch), lambda b, i: (b, jnp.minimum((i + 1) * hb, nb - 1), 0)),
                  row(LANES), const2(conv_w), const2(cb2), const2(dtb), const2(alog), const2(sel), const2(expand)],
        out_specs=[row(WIDTH_D), row(WIDTH_D), row(nch - WIDTH_D)],
        out_shape=[shp(WIDTH_D, F32), shp(WIDTH_D, F32), shp(nch - WIDTH_D, BF16)],
        scratch_shapes=[pltpu.VMEM((blk_rows + 2 * SUBLANES, nch), F32), state],
        compiler_params=_cparams("parallel", "arbitrary"),
        name="ssd_fwd",
    )(dxbc, dxbc, dxbc, dt_raw, conv_w, cb2, dtb, alog, sel, expand)

    sel, expand = _ssd_consts(chunk, True)
    row = lambda width: pl.BlockSpec((None, blk_rows, width), lambda b, i: (b, n - 1 - i, 0))
    dsk = jnp.repeat(d_skip, SSD_HEAD_DIM).reshape(1, WIDTH_D)
    ng = norm_g.reshape(1, WIDTH_D)
    return pl.pallas_call(
        functools.partial(_ssd_kernel, final=True, reverse=True, n_blocks=n),
        grid=(bsz, n),
        in_specs=[row(WIDTH_D), row(nch - WIDTH_D), row(LANES), const2(dtb), const2(alog), const2(sel), const2(expand),
                  row(WIDTH_D), row(WIDTH_D), const2(dsk), const2(ng)],
        out_specs=row(WIDTH_D),
        out_shape=shp(WIDTH_D, BF16),
        scratch_shapes=[state],
        compiler_params=_cparams("parallel", "arbitrary"),
        name="ssd_bwd",
    )(xs, bc, dt_raw, dtb, alog, sel, expand, y_fwd, dz, dsk, ng)


def kernel(x, c, ada_w, ada_b, ln_g, ln_b, e_w_in, e_rpb, e_gla_w_up, e_gla_b, e_gla_norm_g, e_w_out,
           o_w_in, hgrn_lb, o_hgrn_norm_g, o_conv_w, o_conv_b, o_dt_bias, o_a_log, o_d_skip,
           o_ssm_norm_g, o_w_out):
    mod = _ada_mod(c, ada_w, ada_b)

    aq, ak, av, ag, bq, bk, bv, bg, lf_f, lf_b = _proj_even(x, mod[0], e_w_in[0], e_gla_w_up[0], e_gla_b[0])
    ya = _neighbourhood_attention(aq, ak, av, ag, _na_bias(e_rpb[0]))
    gla = functools.partial(_gated_recurrence, "gla", heads=GLA_HEADS, dk=GLA_DK, dv=GLA_DV)
    o_fwd = gla(bq, bk, lf_f, bv, None, None, None, reverse=False)
    yb = gla(bq, bk, lf_b, bv, o_fwd, bg, e_gla_norm_g[0], reverse=True)
    x = _out_ln(ya, yb, e_w_out[0], x, mod[0], ln_g[0], ln_b[0])

    cq, ck_f, lf_f, ck_b, lf_b, ci, cg, dz, dxbc, dt_raw = _proj_odd(x, mod[1], o_w_in[0], hgrn_lb)
    hgrn = functools.partial(_gated_recurrence, "hgrn", heads=HGRN_HEADS, dk=HGRN_DIM, dv=HGRN_DIM)
    o_fwd = hgrn(cq, ck_f, lf_f, ci, None, None, None, reverse=False)
    yc = hgrn(cq, ck_b, lf_b, ci, o_fwd, cg, o_hgrn_norm_g[0], reverse=True)
    yd = _ssd(dxbc, dt_raw, o_conv_w[0], o_conv_b[0], o_dt_bias[0], o_a_log[0], dz, o_d_skip[0], o_ssm_norm_g[0])
    return _out_ln(yc, yd, o_w_out[0], x, mod[1], ln_g[1], ln_b[1])
```

```python
import functools
import math

import numpy as np
import jax
import jax.numpy as jnp
from jax import lax
from jax.experimental import pallas as pl
from jax.experimental.pallas import tpu as pltpu

F32 = jnp.float32
BF16 = jnp.bfloat16

GRID_W = 64
NA_HEADS, NA_HEAD_DIM = 8, 64
NA_WIN_ROWS, NA_WIN_COLS = 8, 16
WIDTH_A = NA_HEADS * NA_HEAD_DIM
GLA_HEADS, GLA_DK, GLA_DV, GLA_RANK = 4, 64, 128, 16
GLA_GATE_NORM = 16.0
WIDTH_B = GLA_HEADS * GLA_DV
HGRN_HEADS, HGRN_DIM = 4, 128
WIDTH_C = HGRN_HEADS * HGRN_DIM
SSD_HEADS, SSD_HEAD_DIM, SSD_GROUPS, SSD_STATE, SSD_CONV = 8, 64, 2, 128, 4
WIDTH_D = SSD_HEADS * SSD_HEAD_DIM
SSD_CONV_CH = WIDTH_D + 2 * SSD_GROUPS * SSD_STATE
DEPTH = 2
DEEPNORM_ALPHA = (2 * DEPTH) ** 0.25
LN_EPS = 1e-5
RMS_EPS = 1e-6

LANES = 128
SUBLANES = 8
VMEM_LIMIT = 56 * 1024 * 1024

ROW_TILE = 512
ADA_COL_TILE = 1024
OUT_ROW_TILE = 1024
OUT_X_SLOTS = 3
OUT_ROW_SUBTILE = 256
REC_CHUNK = 128
REC_CHUNKS_PER_STEP = 16
REC_SEL_LEVELS = (1, 2)
NA_ROWS_PER_STEP = 8
NEG_INF = float("-inf")
LOG2_E = math.log2(math.e)


def _cparams(*sem):
    return pltpu.CompilerParams(dimension_semantics=sem, vmem_limit_bytes=VMEM_LIMIT)


def _dot(a, b):
    return jnp.dot(a, b, preferred_element_type=F32)


def _dot_nt(a, b):
    return lax.dot_general(a, b, (((1,), (1,)), ((), ())), preferred_element_type=F32)


def _dot_tn(a, b):
    return lax.dot_general(a, b, (((0,), (0,)), ((), ())), preferred_element_type=F32)


def _sel_dot2(sel, x):
    hi = x.astype(BF16)
    lo = (x - hi.astype(F32)).astype(BF16)
    return _dot(sel, hi) + _dot(sel, lo)


def _dot_sel2(x, sel):
    hi = x.astype(BF16)
    lo = (x - hi.astype(F32)).astype(BF16)
    return _dot(hi, sel) + _dot(lo, sel)


def _sigmoid(x):
    return 1.0 / (1.0 + jnp.exp(-x))


def _silu(x):
    return x * _sigmoid(x)


def _log1pexp_neg_abs(x):
    return jnp.log1p(jnp.exp(-jnp.abs(x)))


def _softplus(x):
    return jnp.maximum(x, 0.0) + _log1pexp_neg_abs(x)


def _ada_kernel(c_ref, w_ref, b_ref, o_ref):
    cond = _silu(c_ref[...]).astype(BF16)
    o_ref[...] = _dot(cond, w_ref[...].astype(BF16)) + b_ref[...]


def _ada_mod(c, ada_w, ada_b):
    bsz, d = c.shape
    bp = -(-bsz // SUBLANES) * SUBLANES
    n3 = ada_w.shape[-1]
    tn = ADA_COL_TILE
    c_pad = jnp.zeros((bp, d), F32).at[:bsz].set(c)
    out = pl.pallas_call(
        _ada_kernel,
        grid=(DEPTH, n3 // tn),
        in_specs=[pl.BlockSpec((bp, d), lambda l, j: (0, 0)),
                  pl.BlockSpec((None, d, tn), lambda l, j: (l, 0, j)),
                  pl.BlockSpec((None, 1, tn), lambda l, j: (l, 0, j))],
        out_specs=pl.BlockSpec((None, bp, tn), lambda l, j: (l, 0, j)),
        out_shape=jax.ShapeDtypeStruct((DEPTH, bp, n3), F32),
        compiler_params=_cparams("arbitrary", "arbitrary"),
        name="ada_mod",
    )(c_pad, ada_w, ada_b.reshape(DEPTH, 1, n3))
    return out[:, :bsz].reshape(DEPTH, bsz, 1, n3)


def _modulated(x_ref, mod_ref):
    d = x_ref.shape[-1]
    mod = mod_ref[...]
    return (x_ref[...] * (1.0 + mod[:, d:2 * d]) + mod[:, :d]).astype(BF16)


def _stage_weights(wt_ref, wbf_ref):
    n = wt_ref.shape[0]
    n_full = n // LANES * LANES

    @pl.when((pl.program_id(0) == 0) & (pl.program_id(1) == 0))
    def _():
        def body(r, carry):
            rows = pl.ds(pl.multiple_of(r * LANES, LANES), LANES)
            wbf_ref[rows, :] = wt_ref[rows, :].astype(BF16)
            return carry
        lax.fori_loop(0, n_full // LANES, body, 0)
        if n_full < wbf_ref.shape[0]:
            wbf_ref[n_full:, :] = jnp.zeros((wbf_ref.shape[0] - n_full, wbf_ref.shape[1]), BF16)
            wbf_ref[n_full:n, :] = wt_ref[n_full:n, :].astype(BF16)


def _proj_even_kernel(x_ref, mod_ref, wf_ref, wup_ref, gb_ref,
                      aq_ref, ak_ref, av_ref, ag_ref, bq_ref, bk_ref, bv_ref, bg_ref, lff_ref, lfb_ref, h_ref, w_ref):
    _stage_weights(wf_ref, w_ref)
    h_ref[...] = _modulated(x_ref, mod_ref)

    def mm(lo, hi):
        return _dot_nt(h_ref[...], w_ref[lo:hi, :])

    a, kb = WIDTH_A, GLA_HEADS * GLA_DK
    aq_ref[...] = (mm(0, a) * (NA_HEAD_DIM ** -0.5)).astype(BF16)
    o = 4 * a + 2 * kb + 2 * WIDTH_B
    lr = mm(o, o + LANES)
    lr_hi = lr.astype(BF16)
    lr_lo = (lr - lr_hi.astype(F32)).astype(BF16)
    ak_ref[...] = mm(a, 2 * a).astype(BF16)

    def gate_logits(d, out_ref):
        wu = wup_ref[d]
        wu_hi = wu.astype(BF16)
        wu_lo = (wu - wu_hi.astype(F32)).astype(BF16)
        out_ref[...] = _dot(lr_hi, wu_hi) + _dot(lr_lo, wu_hi) + _dot(lr_hi, wu_lo) + gb_ref[d]

    def finish_gate(out_ref):
        z = out_ref[...]
        log_sig = jnp.minimum(z, 0.0) - jnp.log(1.0 + jnp.exp2(jnp.abs(z) * (-LOG2_E)))
        out_ref[...] = log_sig * (LOG2_E / GLA_GATE_NORM)

    gate_logits(0, lff_ref)
    av_ref[...] = mm(2 * a, 3 * a).astype(BF16)
    gate_logits(1, lfb_ref)
    ag_ref[...] = mm(3 * a, 4 * a)
    finish_gate(lff_ref)
    o = 4 * a
    bqk = mm(o, o + 2 * kb)
    bq_ref[...] = (bqk[:, :kb] * (GLA_DK ** -0.5)).astype(BF16)
    bk_ref[...] = bqk[:, kb:].astype(BF16)
    finish_gate(lfb_ref)
    o += 2 * kb
    bv_ref[...] = mm(o, o + WIDTH_B).astype(BF16)
    bg_ref[...] = mm(o + WIDTH_B, o + 2 * WIDTH_B)


def _proj_even(x, mod, w_in, gla_w_up, gla_b):
    bsz, seq, d = x.shape
    kb = GLA_HEADS * GLA_DK
    n_main = 4 * WIDTH_A + 2 * kb + 2 * WIDTH_B
    wt = jnp.swapaxes(w_in, 0, 1)
    wup = jnp.zeros((2, LANES, kb), F32)
    wup = wup.at[0, :GLA_RANK].set(gla_w_up[0]).at[1, GLA_RANK:2 * GLA_RANK].set(gla_w_up[1])
    tm = min(ROW_TILE, seq)
    row = lambda n: pl.BlockSpec((None, tm, n), lambda b, i: (b, i, 0))
    shp = lambda n, dt: jax.ShapeDtypeStruct((bsz, seq, n), dt)
    widths = [(WIDTH_A, BF16), (WIDTH_A, BF16), (WIDTH_A, BF16), (WIDTH_A, F32),
              (kb, BF16), (kb, BF16), (WIDTH_B, BF16), (WIDTH_B, F32), (kb, F32), (kb, F32)]
    return pl.pallas_call(
        _proj_even_kernel,
        grid=(bsz, seq // tm),
        in_specs=[row(d),
                  pl.BlockSpec((None, 1, mod.shape[-1]), lambda b, i: (b, 0, 0)),
                  pl.BlockSpec(wt.shape, lambda b, i: (0, 0), pipeline_mode=pl.Buffered(1)),
                  pl.BlockSpec(wup.shape, lambda b, i: (0, 0, 0)),
                  pl.BlockSpec((2, 1, kb), lambda b, i: (0, 0, 0))],
        out_specs=[row(n) for n, _ in widths],
        out_shape=[shp(n, dt) for n, dt in widths],
        scratch_shapes=[pltpu.VMEM((tm, d), BF16), pltpu.VMEM((n_main + LANES, d), BF16)],
        compiler_params=_cparams("arbitrary", "arbitrary"),
        name="proj_even",
    )(x, mod, wt, wup, gla_b.reshape(2, 1, kb))


def _proj_odd_kernel(x_ref, mod_ref, wf_ref, lbraw_ref,
                     cq_ref, kf_ref, lff_ref, kb_ref, lfb_ref, ci_ref, cg_ref, dz_ref, dxbc_ref, dt_ref, h_ref, w_ref):
    _stage_weights(wf_ref, w_ref)
    h_ref[...] = _modulated(x_ref, mod_ref)

    def mm(lo, hi):
        return _dot_nt(h_ref[...], w_ref[lo:hi, :])

    c = WIDTH_C
    lbr = lbraw_ref[...]
    mx = jnp.maximum(lbr[0:1], lbr[1:2])
    e0, e1 = jnp.exp(lbr[0:1] - mx), jnp.exp(lbr[1:2] - mx)
    lb = e1 / (e0 + e1)
    log_lb, log_ub = jnp.log(lb), jnp.log1p(-lb)
    gate_refs = ((kf_ref, lff_ref), (kb_ref, lfb_ref))
    for d, (_, lf_ref) in enumerate(gate_refs):
        lf_ref[...] = mm((1 + d) * c, (2 + d) * c)

    half = x_ref.shape[0] // 2

    def finish_gate(d, part):
        k_ref, lf_ref = gate_refs[d]
        rows = slice(part * half, (part + 1) * half)
        z = lf_ref[rows, :]
        t = jnp.exp2(jnp.abs(z) * (-LOG2_E))
        u = 1.0 + t
        k_ref[rows, :] = (1.0 - lb) * (jnp.where(z > 0.0, t, 1.0) / u)
        lc = log_ub + (jnp.minimum(z, 0.0) - jnp.log(u))
        w = 1.0 + jnp.exp2(jnp.abs(log_lb - lc) * (-LOG2_E))
        lf_ref[rows, :] = (jnp.maximum(log_lb, lc) + jnp.log(w)) * LOG2_E

    cq_ref[...] = (mm(0, c) * (HGRN_DIM ** -0.5)).astype(BF16)
    finish_gate(0, 0)
    ci_ref[...] = mm(3 * c, 4 * c).astype(BF16)
    finish_gate(0, 1)
    cg_ref[...] = mm(4 * c, 5 * c)
    finish_gate(1, 0)
    o = 5 * c
    dz_ref[...] = mm(o, o + WIDTH_D)
    finish_gate(1, 1)
    o += WIDTH_D
    dxbc_ref[...] = mm(o, o + SSD_CONV_CH)
    o += SSD_CONV_CH
    dt_ref[...] = mm(o, o + LANES)


def _proj_odd(x, mod, w_in, hgrn_lb):
    bsz, seq, d = x.shape
    n_main = 5 * WIDTH_C + WIDTH_D + SSD_CONV_CH
    wt = jnp.swapaxes(w_in, 0, 1)
    tm = min(ROW_TILE, seq)
    row = lambda n: pl.BlockSpec((None, tm, n), lambda b, i: (b, i, 0))
    shp = lambda n, dt: jax.ShapeDtypeStruct((bsz, seq, n), dt)
    widths = [(WIDTH_C, BF16), (WIDTH_C, F32), (WIDTH_C, F32), (WIDTH_C, F32), (WIDTH_C, F32), (WIDTH_C, BF16),
              (WIDTH_C, F32), (WIDTH_D, F32), (SSD_CONV_CH, F32), (LANES, F32)]
    return pl.pallas_call(
        _proj_odd_kernel,
        grid=(bsz, seq // tm),
        in_specs=[row(d),
                  pl.BlockSpec((None, 1, mod.shape[-1]), lambda b, i: (b, 0, 0)),
                  pl.BlockSpec(wt.shape, lambda b, i: (0, 0), pipeline_mode=pl.Buffered(1)),
                  pl.BlockSpec(hgrn_lb.shape, lambda b, i: (0, 0))],
        out_specs=[row(n) for n, _ in widths],
        out_shape=[shp(n, dt) for n, dt in widths],
        scratch_shapes=[pltpu.VMEM((tm, d), BF16), pltpu.VMEM((n_main + LANES, d), BF16)],
        compiler_params=_cparams("arbitrary", "arbitrary"),
        name="proj_odd",
    )(x, mod, wt, hgrn_lb)


def _out_ln_kernel(ya_hbm, yb_hbm, w_ref, x_hbm, mod_ref, g_ref, b_ref, o_ref, yabuf, ybbuf, xbuf, sems, *, n_inner):
    tm, d = xbuf.shape[1:]
    wa = yabuf.shape[-1]
    gate = mod_ref[...][:, 2 * d:]
    sub = OUT_ROW_SUBTILE
    n_steps = pl.num_programs(0) * n_inner
    step = pl.program_id(0) * n_inner + pl.program_id(1)
    streams = ((ya_hbm, yabuf), (yb_hbm, ybbuf), (x_hbm, xbuf))

    def tile_copies(s):
        s = jnp.asarray(s, jnp.int32)
        slot = lax.rem(s, OUT_X_SLOTS)
        rows = pl.ds(pl.multiple_of(lax.rem(s, n_inner) * tm, tm), tm)
        return [pltpu.make_async_copy(hbm.at[lax.div(s, n_inner), rows, :], buf.at[slot], sems.at[n, slot])
                for n, (hbm, buf) in enumerate(streams)]

    @pl.when(step == 0)
    def _():
        for s in range(OUT_X_SLOTS - 1):
            for c in tile_copies(s):
                c.start()

    ahead = step + (OUT_X_SLOTS - 1)

    @pl.when(ahead < n_steps)
    def _():
        for c in tile_copies(ahead):
            c.start()

    for c in tile_copies(step):
        c.wait()
    slot = lax.rem(step, OUT_X_SLOTS)
    ya_ref, yb_ref, x_ref = yabuf.at[slot], ybbuf.at[slot], xbuf.at[slot]

    def project(r):
        rows = slice(r * sub, (r + 1) * sub)
        return rows, _dot(ya_ref[rows, :], w_ref[:wa, :]) + _dot(yb_ref[rows, :], w_ref[wa:, :])

    def normalise(rows, y):
        t = DEEPNORM_ALPHA * x_ref[rows, :] + gate * y
        mu = jnp.mean(t, axis=-1, keepdims=True)
        tc = t - mu
        var = jnp.mean(tc * tc, axis=-1, keepdims=True)
        o_ref[rows, :] = tc * lax.rsqrt(var + LN_EPS) * g_ref[...] + b_ref[...]

    pending = None
    for r in range(x_ref.shape[0] // sub):
        cur = project(r)
        if pending is not None:
            normalise(*pending)
        pending = cur
    normalise(*pending)


def _out_ln(ya, yb, w_out, x, mod, ln_g, ln_b):
    bsz, seq, d = x.shape
    tm = min(OUT_ROW_TILE, seq)
    n_inner = seq // tm
    assert bsz * n_inner >= OUT_X_SLOTS - 1
    w = w_out.astype(BF16)
    row = lambda n: pl.BlockSpec((None, tm, n), lambda b, i: (b, i, 0))
    vec = pl.BlockSpec((1, d), lambda b, i: (0, 0))
    hbm = pl.BlockSpec(memory_space=pl.ANY)
    return pl.pallas_call(
        functools.partial(_out_ln_kernel, n_inner=n_inner),
        grid=(bsz, n_inner),
        in_specs=[hbm, hbm,
                  pl.BlockSpec(w.shape, lambda b, i: (0, 0)),
                  hbm,
                  pl.BlockSpec((None, 1, mod.shape[-1]), lambda b, i: (b, 0, 0)),
                  vec, vec],
        out_specs=row(d),
        out_shape=jax.ShapeDtypeStruct((bsz, seq, d), F32),
        scratch_shapes=[pltpu.VMEM((OUT_X_SLOTS, tm, ya.shape[-1]), ya.dtype),
                        pltpu.VMEM((OUT_X_SLOTS, tm, yb.shape[-1]), yb.dtype),
                        pltpu.VMEM((OUT_X_SLOTS, tm, d), F32),
                        pltpu.SemaphoreType.DMA((3, OUT_X_SLOTS))],
        compiler_params=_cparams("arbitrary", "arbitrary"),
        name="out_ln",
    )(ya, yb, w, x, mod, ln_g.reshape(1, d), ln_b.reshape(1, d))


NA_BIAS_TYPES = NA_WIN_ROWS
NA_KEYS = NA_WIN_ROWS * GRID_W


def _na_bias_kernel(rpb_ref, o_ref):
    h = pl.program_id(0)
    shape = (GRID_W, LANES)
    q = lax.broadcasted_iota(jnp.int32, shape, 0)
    lane = lax.broadcasted_iota(jnp.int32, shape, 1)
    kc = lane & (GRID_W - 1)
    dc = kc - q + (NA_WIN_COLS - 1)
    cs = jnp.clip(q - NA_WIN_COLS // 2, 0, GRID_W - NA_WIN_COLS)
    valid = (kc >= cs) & (kc < cs + NA_WIN_COLS)
    upper = lane >= GRID_W
    n_dr, n_dc = 2 * NA_WIN_ROWS - 1, 2 * NA_WIN_COLS - 1
    tiles = []
    for dr in range(n_dr):
        acc = jnp.zeros(shape, F32)
        for d in range(n_dc):
            acc = jnp.where(dc == d, rpb_ref[(h * n_dr + dr) * n_dc + d], acc)
        tiles.append(jnp.where(valid, acc, NEG_INF))
    for t in range(NA_BIAS_TYPES):
        for c in range(NA_KEYS // LANES):
            dr0 = 2 * c + NA_WIN_ROWS - 1 - t
            o_ref[t, :, c * LANES:(c + 1) * LANES] = jnp.where(upper, tiles[dr0 + 1], tiles[dr0])


def _na_bias(rpb):
    return pl.pallas_call(
        _na_bias_kernel,
        grid=(NA_HEADS,),
        in_specs=[pl.BlockSpec(memory_space=pltpu.SMEM)],
        out_specs=pl.BlockSpec((NA_BIAS_TYPES, None, GRID_W, NA_KEYS), lambda h: (0, h, 0, 0)),
        out_shape=jax.ShapeDtypeStruct((NA_BIAS_TYPES, NA_HEADS, GRID_W, NA_KEYS), F32),
        compiler_params=_cparams("arbitrary"),
        name="na_bias",
    )(rpb.reshape(-1))


def _na_kernel(q_ref, k_ref, v_ref, g_ref, bias_ref, o_ref, *, n_rows):
    i = pl.program_id(1)
    lane = lax.broadcasted_iota(jnp.int32, (1, LANES), 1)
    head_mask = [(lane < NA_HEAD_DIM).astype(BF16), (lane >= NA_HEAD_DIM).astype(BF16)]
    half = NA_WIN_ROWS // 2
    pair_cols = [slice(p * LANES, (p + 1) * LANES) for p in range(WIDTH_A // LANES)]

    def stage_scores(rr):
        r = i * NA_ROWS_PER_STEP + rr
        row_start = jnp.clip(r - half, 0, n_rows - NA_WIN_ROWS)
        t = jnp.where(r < half, r, jnp.where(r > n_rows - half, r - (n_rows - NA_WIN_ROWS), half))
        koff = pl.multiple_of(row_start * GRID_W, GRID_W)
        rows = slice(rr * GRID_W, (rr + 1) * GRID_W)
        scores = []
        for p, cols in enumerate(pair_cols):
            qp = q_ref[rows, cols]
            qs = jnp.concatenate([qp * head_mask[0], qp * head_mask[1]], axis=0)
            kp = k_ref[pl.ds(koff, NA_KEYS), cols]
            scores.append(_dot_nt(qs, kp) + bias_ref[t, p])
        return rows, koff, scores

    def stage_out(rows, koff, scores):
        probs, norms = [], []
        for s in scores:
            e = jnp.exp(s - jnp.max(s, axis=-1, keepdims=True))
            norms.append(jnp.sum(e, axis=-1, keepdims=True))
            probs.append(e.astype(BF16))
        for p, cols in enumerate(pair_cols):
            vp = v_ref[pl.ds(koff, NA_KEYS), cols]
            o2 = _dot(probs[p], vp) / norms[p]
            o = jnp.where(lane < NA_HEAD_DIM, o2[:GRID_W], o2[GRID_W:])
            o_ref[rows, cols] = (o * _silu(g_ref[rows, cols])).astype(BF16)

    pending = None
    for rr in range(NA_ROWS_PER_STEP):
        cur = stage_scores(rr)
        if pending is not None:
            stage_out(*pending)
        pending = cur
    stage_out(*pending)


def _neighbourhood_attention(aq, ak, av, ag, bias):
    bsz, seq, w = aq.shape
    n_rows = seq // GRID_W
    tq = NA_ROWS_PER_STEP * GRID_W
    bias = bias.reshape(NA_BIAS_TYPES, NA_HEADS // 2, 2 * GRID_W, NA_KEYS)
    row = pl.BlockSpec((None, tq, w), lambda b, i: (b, i, 0))
    full = pl.BlockSpec((None, seq, w), lambda b, i: (b, 0, 0))
    return pl.pallas_call(
        functools.partial(_na_kernel, n_rows=n_rows),
        grid=(bsz, n_rows // NA_ROWS_PER_STEP),
        in_specs=[row, full, full, row,
                  pl.BlockSpec(bias.shape, lambda b, i: (0, 0, 0, 0))],
        out_specs=row,
        out_shape=jax.ShapeDtypeStruct((bsz, seq, w), BF16),
        compiler_params=_cparams("parallel", "arbitrary"),
        name="na_attn",
    )(aq, ak, av, ag, bias)


def _rec_consts(chunk, reverse):
    nlev = int(math.log2(chunk))
    assert 1 << nlev == chunk
    w = np.zeros(((nlev + 2), chunk, chunk), np.float32)
    lv = np.full((chunk, chunk), -1, np.int32)
    idx = np.arange(chunk)
    for l in range(nlev):
        s = 1 << l
        off = idx % (2 * s)
        mid = idx - off + s - 1
        for p in range(chunk):
            if off[p] >= s:
                w[l, p, mid[p] + 1:p + 1] = 1.0
            else:
                w[l, p, p + 1:mid[p] + 1] = 1.0
        same = (idx[:, None] // (2 * s)) == (idx[None, :] // (2 * s))
        lv[same & (off[:, None] >= s) & (off[None, :] < s)] = l
    lv[idx, idx] = nlev
    w[nlev] = (idx[None, :] <= idx[:, None])
    if reverse:
        w = w[:, ::-1, ::-1]
        lv = lv[::-1, ::-1]
    w = np.concatenate([w[nlev]] + [w[l] for l in REC_SEL_LEVELS], axis=0)
    return (jnp.asarray(w, BF16), jnp.asarray(np.ascontiguousarray(lv)), nlev)


def _rec_kernel(*refs, final, reverse, heads, dk, dv, nlev):
    chunk = REC_CHUNK
    q_ref, k_ref, lf_ref, v_ref, w_ref, lv_ref = refs[:6]
    if final:
        prev_ref, gate_ref, ng_ref, o_ref, st_ref = refs[6:]
    else:
        o_ref, st_ref = refs[6:]

    @pl.when(pl.program_id(1) == 0)
    def _():
        st_ref[...] = jnp.zeros_like(st_ref)

    lane = lax.broadcasted_iota(jnp.int32, (1, LANES), 1)
    row = lax.broadcasted_iota(jnp.int32, (chunk, LANES), 0)
    per_slab = LANES // dk
    n_slabs = heads * dk // LANES
    lv = lv_ref[...]
    level_masks = [lv == l for l in range(nlev + 1)]
    last = 0 if reverse else chunk - 1

    def is_query_side(l):
        off = row & (2 * (1 << l) - 1)
        return (off < (1 << l)) if reverse else (off >= (1 << l))

    query_side = [is_query_side(l) for l in range(nlev) if (1 << l) < SUBLANES]

    def boundary_exponent(b, l):
        s = 1 << l
        parts = []
        for blk in range(chunk // (2 * s)):
            mid = blk * 2 * s + (s if reverse else s - 1)
            b_mid = jnp.broadcast_to(b[mid:mid + 1, :], (s, LANES))
            lo, hi = b[blk * 2 * s:blk * 2 * s + s], b[blk * 2 * s + s:(blk + 1) * 2 * s]
            parts += [lo - b_mid, b_mid - hi] if reverse else [b_mid - lo, hi - b_mid]
        return jnp.concatenate(parts, axis=0)

    def query_or_key(q, k, l):
        if (1 << l) < SUBLANES:
            return jnp.where(query_side[l], q, k)
        s = 1 << l
        parts = []
        for blk in range(chunk // (2 * s)):
            lo, hi = slice(blk * 2 * s, blk * 2 * s + s), slice(blk * 2 * s + s, (blk + 1) * 2 * s)
            parts += [q[lo], k[hi]] if reverse else [k[lo], q[hi]]
        return jnp.concatenate(parts, axis=0)

    head_masks = [((lane >= hh * dk) & (lane < (hh + 1) * dk)) for hh in range(per_slab)]
    head_masks_bf = [hm.astype(BF16) for hm in head_masks]

    def stage_scale(ci):
        rows = slice(ci * chunk, (ci + 1) * chunk)
        lf_all = lf_ref[rows, :]
        sums = _sel_dot2(w_ref[...], lf_all)
        slabs = []
        for s in range(n_slabs):
            cols = slice(s * LANES, (s + 1) * LANES)
            q = q_ref[rows, cols].astype(F32)
            k = k_ref[rows, cols].astype(F32)
            lf = lf_all[:, cols]
            b = sums[:chunk, cols]
            ys = []
            for l in range(nlev):
                if l == 0:
                    e = jnp.where(query_side[0], lf, 0.0)
                elif l in REC_SEL_LEVELS:
                    i0 = (1 + REC_SEL_LEVELS.index(l)) * chunk
                    e = sums[i0:i0 + chunk, cols]
                else:
                    e = boundary_exponent(b, l)
                ys.append((jnp.exp2(e) * query_or_key(q, k, l)).astype(BF16))
            eb = jnp.exp2(b)
            b_end = jnp.broadcast_to(b[last:last + 1, :], (chunk, LANES))
            slabs.append(dict(
                lhs=ys + [q.astype(BF16)], rhs=ys + [k.astype(BF16)],
                q_in=(eb * q).astype(BF16),
                k_up=(jnp.exp2(b_end - b) * k).astype(BF16),
                decay=eb[last:last + 1]))
        return rows, slabs

    lane2 = lax.broadcasted_iota(jnp.int32, (1, 2 * dv), 1)
    v_first, v_second = (lane2 < dv).astype(BF16), (lane2 >= dv).astype(BF16)
    pair_masks = [jnp.concatenate([m, m], axis=1) for m in level_masks]

    def block_diag(a, b_):
        za, zb = jnp.zeros_like(a), jnp.zeros_like(b_)
        return jnp.concatenate([jnp.concatenate([a, zb], axis=1), jnp.concatenate([za, b_], axis=1)], axis=0)

    def pair_operands(slabs, pr, key, l=None):
        pick = (lambda d: d[key][l]) if l is not None else (lambda d: d[key])
        if per_slab == 2:
            t = pick(slabs[pr])
            return t, jnp.concatenate([t * head_masks_bf[0], t * head_masks_bf[1]], axis=0)
        a, b_ = pick(slabs[2 * pr]), pick(slabs[2 * pr + 1])
        return jnp.concatenate([a, b_], axis=1), block_diag(a, b_)

    def stage_intra(slabs):
        mats = []
        for pr in range(heads // 2):
            a_mat = jnp.zeros((chunk, 2 * chunk), F32)
            for l in range(nlev + 1):
                lhs = pair_operands(slabs, pr, "lhs", l)[0]
                rhs = pair_operands(slabs, pr, "rhs", l)[1]
                a_mat = jnp.where(pair_masks[l], _dot_nt(lhs, rhs), a_mat)
            mats.append(a_mat.astype(BF16))
        return mats

    def stage_out(rows, slabs, mats):
        for pr in range(heads // 2):
            pcols = slice(pr * 2 * dv, (pr + 1) * 2 * dv)
            vp = v_ref[rows, pcols]
            v_bd = jnp.concatenate([vp * v_first, vp * v_second], axis=0)
            q_in = pair_operands(slabs, pr, "q_in")[0]
            k_up = pair_operands(slabs, pr, "k_up")[0]
            if per_slab == 2:
                st = st_ref[pr]
                st_bf = st.astype(BF16)
                st_bd = jnp.concatenate([st_bf * head_masks_bf[0], st_bf * head_masks_bf[1]], axis=0)
            else:
                st_a, st_b = st_ref[2 * pr], st_ref[2 * pr + 1]
                st_bd = block_diag(st_a.astype(BF16), st_b.astype(BF16))
            o = _dot(mats[pr], v_bd) + _dot_nt(q_in, st_bd)
            u = _dot_tn(vp, k_up)
            if per_slab == 2:
                st_ref[pr] = st * slabs[pr]["decay"] + jnp.where(head_masks[0], u[:dv], u[dv:])
            else:
                st_ref[2 * pr] = st_a * slabs[2 * pr]["decay"] + u[:dv, :LANES]
                st_ref[2 * pr + 1] = st_b * slabs[2 * pr + 1]["decay"] + u[dv:, LANES:]
            if final:
                for hh in range(2):
                    ocols = slice((2 * pr + hh) * dv, (2 * pr + hh + 1) * dv)
                    tot = prev_ref[rows, ocols] + o[:, hh * dv:(hh + 1) * dv]
                    ms = jnp.mean(tot * tot, axis=-1, keepdims=True)
                    y = tot * lax.rsqrt(ms + RMS_EPS) * ng_ref[...]
                    o_ref[rows, ocols] = (y * _silu(gate_ref[rows, ocols])).astype(o_ref.dtype)
            else:
                o_ref[rows, pcols] = o

    n_sub = q_ref.shape[0] // chunk
    order = list(range(n_sub - 1, -1, -1) if reverse else range(n_sub))
    pending = None
    for ci in order:
        cur = stage_scale(ci)
        if pending is not None:
            stage_out(*pending, stage_intra(pending[1]))
        pending = cur
    stage_out(*pending, stage_intra(pending[1]))


def _gated_recurrence(name, q, k, lf, v, prev, gate, norm_g, *, reverse, heads, dk, dv):
    bsz, seq, _ = q.shape
    chunk = REC_CHUNK
    blk_rows = min(REC_CHUNKS_PER_STEP * chunk, seq)
    n = seq // blk_rows
    final = prev is not None
    w_sel, lv, nlev = _rec_consts(chunk, reverse)
    cidx = (lambda b, i: (b, n - 1 - i, 0)) if reverse else (lambda b, i: (b, i, 0))
    row = lambda width: pl.BlockSpec((None, blk_rows, width), cidx)
    const2 = lambda a: pl.BlockSpec(a.shape, lambda b, i: (0, 0))
    in_specs = [row(heads * dk), row(heads * dk), row(heads * dk), row(heads * dv), const2(w_sel), const2(lv)]
    args = [q, k, lf, v, w_sel, lv]
    if final:
        in_specs += [row(heads * dv), row(heads * dv), pl.BlockSpec((1, dv), lambda b, i: (0, 0))]
        args += [prev, gate, norm_g.reshape(1, dv)]
    return pl.pallas_call(
        functools.partial(_rec_kernel, final=final, reverse=reverse, heads=heads, dk=dk, dv=dv, nlev=nlev),
        grid=(bsz, n),
        in_specs=in_specs,
        out_specs=row(heads * dv),
        out_shape=jax.ShapeDtypeStruct((bsz, seq, heads * dv), BF16 if final else F32),
        scratch_shapes=[pltpu.VMEM((heads * dk // LANES, dv, LANES), F32)],
        compiler_params=_cparams("parallel", "arbitrary"),
        name=f"{name}_{'bwd' if reverse else 'fwd'}",
    )(*args)


def _ssd_consts(chunk, reverse):
    idx = np.arange(chunk)
    if reverse:
        tri = idx[None, :] >= idx[:, None]
        rest = idx[None, :] < idx[:, None]
    else:
        tri = idx[None, :] <= idx[:, None]
        rest = idx[None, :] > idx[:, None]
    sel = np.concatenate([tri, rest], axis=0).astype(np.float32)
    expand = np.zeros((LANES, WIDTH_D), np.float32)
    for h in range(SSD_HEADS):
        expand[h + (SSD_HEADS if reverse else 0), h * SSD_HEAD_DIM:(h + 1) * SSD_HEAD_DIM] = 1.0
    return jnp.asarray(sel, BF16), jnp.asarray(expand, BF16)


def _ssd_kernel(*refs, final, reverse, n_blocks):
    chunk = REC_CHUNK
    if final:
        (xs_ref, bc_ref, dt_ref, dtb_ref, alog_ref, sel_ref, exp_ref,
         prev_ref, dz_ref, dskip_ref, ng_ref, o_ref, st_ref) = refs
    else:
        (x_ref, xprev_ref, xnext_ref, dt_ref, cw_ref, cb_ref, dtb_ref, alog_ref, sel_ref, exp_ref,
         o_ref, xs_ref, bc_ref, ext_ref, st_ref) = refs
    step = pl.program_id(1)
    blk = (n_blocks - 1 - step) if reverse else step
    blk_rows = dt_ref.shape[0]

    @pl.when(step == 0)
    def _():
        st_ref[...] = jnp.zeros_like(st_ref)

    if not final:
        halo = SUBLANES
        ext_ref[0:halo] = jnp.where(blk == 0, 0.0, xprev_ref[...])
        ext_ref[halo:halo + blk_rows] = x_ref[...]
        ext_ref[halo + blk_rows:2 * halo + blk_rows] = jnp.where(blk == n_blocks - 1, 0.0, xnext_ref[...])
        acc = cb_ref[...]
        for tap in range(SSD_CONV):
            o0 = halo - SSD_CONV // 2 + tap
            acc = acc + ext_ref[o0:o0 + blk_rows] * cw_ref[tap:tap + 1]
        xbc = _silu(acc)
        xs_ref[...] = xbc[:, :WIDTH_D]
        bc_ref[...] = xbc[:, WIDTH_D:].astype(BF16)

    gs = SSD_GROUPS * SSD_STATE
    last = 0 if reverse else chunk - 1
    ii = lax.broadcasted_iota(jnp.int32, (chunk, chunk), 0)
    jj = lax.broadcasted_iota(jnp.int32, (chunk, chunk), 1)
    causal = (ii <= jj) if reverse else (ii >= jj)
    lane = lax.broadcasted_iota(jnp.int32, (1, LANES), 1)
    lo_half = lane < SSD_HEAD_DIM
    col0 = SSD_HEADS if reverse else 0
    rep = SSD_HEADS // SSD_GROUPS
    n_pairs = SSD_HEADS // 2
    neg_a = -jnp.exp(alog_ref[...])

    def stage_decay(ci):
        rows = slice(ci * chunk, (ci + 1) * chunk)
        xs = xs_ref[rows, :]
        bm, cm = bc_ref[rows, :gs], bc_ref[rows, gs:]
        dt = _softplus(dt_ref[rows, :] + dtb_ref[...])
        sums = _sel_dot2(sel_ref[...], dt * neg_a)
        a_cum, a_rem = sums[:chunk], sums[chunk:]
        a_cum_t = a_cum.T
        expand = exp_ref[...]
        x_dt = xs * _dot_sel2(dt, expand)
        e_cum = jnp.exp(_dot_sel2(a_cum, expand))
        x_rem = (x_dt * jnp.exp(_dot_sel2(a_rem, expand))).astype(BF16)
        cbs = [_dot_nt(cm[:, g * SSD_STATE:(g + 1) * SSD_STATE], bm[:, g * SSD_STATE:(g + 1) * SSD_STATE])
               for g in range(SSD_GROUPS)]
        mats, stacks = [], []
        for p in range(n_pairs):
            pair = []
            for hh in range(2):
                ch = col0 + 2 * p + hh
                diff = a_cum[:, ch:ch + 1] - a_cum_t[ch:ch + 1, :]
                seg = jnp.exp(jnp.where(causal, diff, NEG_INF))
                pair.append((cbs[(2 * p) // rep] * seg).astype(BF16))
            mats.append(jnp.concatenate(pair, axis=1))
            xp = x_dt[:, p * LANES:(p + 1) * LANES]
            stacks.append(jnp.concatenate([jnp.where(lo_half, xp, 0.0), jnp.where(lo_half, 0.0, xp)],
                                          axis=0).astype(BF16))
        return dict(rows=rows, xs=xs, bm=bm, cm=cm, e_cum=e_cum, x_rem=x_rem, mats=mats, stacks=stacks)

    def stage_out(d):
        rows = d["rows"]
        ys = []
        for p in range(n_pairs):
            g = (2 * p) // rep
            gcols = slice(g * SSD_STATE, (g + 1) * SSD_STATE)
            pcols = slice(p * LANES, (p + 1) * LANES)
            st = st_ref[p]
            y = _dot(d["mats"][p], d["stacks"][p]) + d["e_cum"][:, pcols] * _dot(d["cm"][:, gcols], st.astype(BF16))
            upd = _dot_tn(d["bm"][:, gcols], d["x_rem"][:, pcols])
            st_ref[p] = st * d["e_cum"][last:last + 1, pcols] + upd
            if final:
                y = y + prev_ref[rows, pcols] + dskip_ref[:, pcols] * d["xs"][:, pcols]
                ys.append(y * _silu(dz_ref[rows, pcols]))
            else:
                o_ref[rows, pcols] = y
        if final:
            t = jnp.concatenate(ys, axis=1)
            ms = jnp.mean(t * t, axis=-1, keepdims=True)
            o_ref[rows, :] = (t * lax.rsqrt(ms + RMS_EPS) * ng_ref[...]).astype(o_ref.dtype)

    n_sub = blk_rows // chunk
    order = list(range(n_sub - 1, -1, -1) if reverse else range(n_sub))
    pending = None
    for ci in order:
        cur = stage_decay(ci)
        if pending is not None:
            stage_out(pending)
        pending = cur
    stage_out(pending)


def _ssd(dxbc, dt_raw, conv_w, conv_b, dt_bias, a_log, dz, d_skip, norm_g):
    bsz, seq, nch = dxbc.shape
    chunk = REC_CHUNK
    blk_rows = min(REC_CHUNKS_PER_STEP * chunk, seq)
    n = seq // blk_rows
    hb = blk_rows // SUBLANES
    nb = seq // SUBLANES
    const2 = lambda a: pl.BlockSpec(a.shape, lambda b, i: (0, 0))
    dtb = jnp.zeros((1, LANES), F32).at[0, :2 * SSD_HEADS].set(dt_bias.reshape(-1))
    alog = jnp.full((1, LANES), NEG_INF, F32).at[0, :2 * SSD_HEADS].set(a_log.reshape(-1))
    cb2 = conv_b.reshape(1, nch)
    state = pltpu.VMEM((SSD_HEADS // 2, SSD_STATE, LANES), F32)
    shp = lambda width, dt: jax.ShapeDtypeStruct((bsz, seq, width), dt)

    sel, expand = _ssd_consts(chunk, False)
    row = lambda width: pl.BlockSpec((None, blk_rows, width), lambda b, i: (b, i, 0))
    y_fwd, xs, bc = pl.pallas_call(
        functools.partial(_ssd_kernel, final=False, reverse=False, n_blocks=n),
        grid=(bsz, n),
        in_specs=[row(nch),
                  pl.BlockSpec((None, SUBLANES, nch), lambda b, i: (b, jnp.maximum(i * hb - 1, 0), 0)),
                  pl.BlockSpec((None, SUBLANES, nch), lambda b, i: (b, jnp.minimum((i + 1) * hb, nb - 1), 0)),
                  row(LANES), const2(conv_w), const2(cb2), const2(dtb), const2(alog), const2(sel), const2(expand)],
        out_specs=[row(WIDTH_D), row(WIDTH_D), row(nch - WIDTH_D)],
        out_shape=[shp(WIDTH_D, F32), shp(WIDTH_D, F32), shp(nch - WIDTH_D, BF16)],
        scratch_shapes=[pltpu.VMEM((blk_rows + 2 * SUBLANES, nch), F32), state],
        compiler_params=_cparams("parallel", "arbitrary"),
        name="ssd_fwd",
    )(dxbc, dxbc, dxbc, dt_raw, conv_w, cb2, dtb, alog, sel, expand)

    sel, expand = _ssd_consts(chunk, True)
    row = lambda width: pl.BlockSpec((None, blk_rows, width), lambda b, i: (b, n - 1 - i, 0))
    dsk = jnp.repeat(d_skip, SSD_HEAD_DIM).reshape(1, WIDTH_D)
    ng = norm_g.reshape(1, WIDTH_D)
    return pl.pallas_call(
        functools.partial(_ssd_kernel, final=True, reverse=True, n_blocks=n),
        grid=(bsz, n),
        in_specs=[row(WIDTH_D), row(nch - WIDTH_D), row(LANES), const2(dtb), const2(alog), const2(sel), const2(expand),
                  row(WIDTH_D), row(WIDTH_D), const2(dsk), const2(ng)],
        out_specs=row(WIDTH_D),
        out_shape=shp(WIDTH_D, BF16),
        scratch_shapes=[state],
        compiler_params=_cparams("parallel", "arbitrary"),
        name="ssd_bwd",
    )(xs, bc, dt_raw, dtb, alog, sel, expand, y_fwd, dz, dsk, ng)


def kernel(x, c, ada_w, ada_b, ln_g, ln_b, e_w_in, e_rpb, e_gla_w_up, e_gla_b, e_gla_norm_g, e_w_out,
           o_w_in, hgrn_lb, o_hgrn_norm_g, o_conv_w, o_conv_b, o_dt_bias, o_a_log, o_d_skip,
           o_ssm_norm_g, o_w_out):
    mod = _ada_mod(c, ada_w, ada_b)

    aq, ak, av, ag, bq, bk, bv, bg, lf_f, lf_b = _proj_even(x, mod[0], e_w_in[0], e_gla_w_up[0], e_gla_b[0])
    ya = _neighbourhood_attention(aq, ak, av, ag, _na_bias(e_rpb[0]))
    gla = functools.partial(_gated_recurrence, "gla", heads=GLA_HEADS, dk=GLA_DK, dv=GLA_DV)
    o_fwd = gla(bq, bk, lf_f, bv, None, None, None, reverse=False)
    yb = gla(bq, bk, lf_b, bv, o_fwd, bg, e_gla_norm_g[0], reverse=True)
    x = _out_ln(ya, yb, e_w_out[0], x, mod[0], ln_g[0], ln_b[0])

    cq, ck_f, lf_f, ck_b, lf_b, ci, cg, dz, dxbc, dt_raw = _proj_odd(x, mod[1], o_w_in[0], hgrn_lb)
    hgrn = functools.partial(_gated_recurrence, "hgrn", heads=HGRN_HEADS, dk=HGRN_DIM, dv=HGRN_DIM)
    o_fwd = hgrn(cq, ck_f, lf_f, ci, None, None, None, reverse=False)
    yc = hgrn(cq, ck_b, lf_b, ci, o_fwd, cg, o_hgrn_norm_g[0], reverse=True)
    yd = _ssd(dxbc, dt_raw, o_conv_w[0], o_conv_b[0], o_dt_bias[0], o_a_log[0], dz, o_d_skip[0], o_ssm_norm_g[0])
    return _out_ln(yc, yd, o_w_out[0], x, mod[1], ln_g[1], ln_b[1])
```

```python
import functools
import math

import numpy as np
import jax
import jax.numpy as jnp
from jax import lax
from jax.experimental import pallas as pl
from jax.experimental.pallas import tpu as pltpu

F32 = jnp.float32
BF16 = jnp.bfloat16

GRID_W = 64
NA_HEADS, NA_HEAD_DIM = 8, 64
NA_WIN_ROWS, NA_WIN_COLS = 8, 16
WIDTH_A = NA_HEADS * NA_HEAD_DIM
GLA_HEADS, GLA_DK, GLA_DV, GLA_RANK = 4, 64, 128, 16
GLA_GATE_NORM = 16.0
WIDTH_B = GLA_HEADS * GLA_DV
HGRN_HEADS, HGRN_DIM = 4, 128
WIDTH_C = HGRN_HEADS * HGRN_DIM
SSD_HEADS, SSD_HEAD_DIM, SSD_GROUPS, SSD_STATE, SSD_CONV = 8, 64, 2, 128, 4
WIDTH_D = SSD_HEADS * SSD_HEAD_DIM
SSD_CONV_CH = WIDTH_D + 2 * SSD_GROUPS * SSD_STATE
DEPTH = 2
DEEPNORM_ALPHA = (2 * DEPTH) ** 0.25
LN_EPS = 1e-5
RMS_EPS = 1e-6

LANES = 128
SUBLANES = 8
VMEM_LIMIT = 56 * 1024 * 1024

ROW_TILE = 512
ADA_COL_TILE = 1024
OUT_ROW_TILE = 1024
OUT_X_SLOTS = 3
OUT_ROW_SUBTILE = 256
REC_CHUNK = 128
REC_CHUNKS_PER_STEP = 16
REC_SEL_LEVELS = (1, 2)
NA_ROWS_PER_STEP = 16
NEG_INF = float("-inf")
LOG2_E = math.log2(math.e)


def _cparams(*sem):
    return pltpu.CompilerParams(dimension_semantics=sem, vmem_limit_bytes=VMEM_LIMIT)


def _dot(a, b):
    return jnp.dot(a, b, preferred_element_type=F32)


def _dot_nt(a, b):
    return lax.dot_general(a, b, (((1,), (1,)), ((), ())), preferred_element_type=F32)


def _dot_tn(a, b):
    return lax.dot_general(a, b, (((0,), (0,)), ((), ())), preferred_element_type=F32)


def _sel_dot2(sel, x):
    hi = x.astype(BF16)
    lo = (x - hi.astype(F32)).astype(BF16)
    return _dot(sel, hi) + _dot(sel, lo)


def _dot_sel2(x, sel):
    hi = x.astype(BF16)
    lo = (x - hi.astype(F32)).astype(BF16)
    return _dot(hi, sel) + _dot(lo, sel)


def _sigmoid(x):
    return 1.0 / (1.0 + jnp.exp(-x))


def _silu(x):
    return x * _sigmoid(x)


def _log1pexp_neg_abs(x):
    return jnp.log1p(jnp.exp(-jnp.abs(x)))


def _softplus(x):
    return jnp.maximum(x, 0.0) + _log1pexp_neg_abs(x)


def _ada_kernel(c_ref, w_ref, b_ref, o_ref):
    cond = _silu(c_ref[...]).astype(BF16)
    o_ref[...] = _dot(cond, w_ref[...].astype(BF16)) + b_ref[...]


def _ada_mod(c, ada_w, ada_b):
    bsz, d = c.shape
    bp = -(-bsz // SUBLANES) * SUBLANES
    n3 = ada_w.shape[-1]
    tn = ADA_COL_TILE
    c_pad = jnp.zeros((bp, d), F32).at[:bsz].set(c)
    out = pl.pallas_call(
        _ada_kernel,
        grid=(DEPTH, n3 // tn),
        in_specs=[pl.BlockSpec((bp, d), lambda l, j: (0, 0)),
                  pl.BlockSpec((None, d, tn), lambda l, j: (l, 0, j)),
                  pl.BlockSpec((None, 1, tn), lambda l, j: (l, 0, j))],
        out_specs=pl.BlockSpec((None, bp, tn), lambda l, j: (l, 0, j)),
        out_shape=jax.ShapeDtypeStruct((DEPTH, bp, n3), F32),
        compiler_params=_cparams("arbitrary", "arbitrary"),
        name="ada_mod",
    )(c_pad, ada_w, ada_b.reshape(DEPTH, 1, n3))
    return out[:, :bsz].reshape(DEPTH, bsz, 1, n3)


def _modulated(x_ref, mod_ref):
    d = x_ref.shape[-1]
    mod = mod_ref[...]
    return (x_ref[...] * (1.0 + mod[:, d:2 * d]) + mod[:, :d]).astype(BF16)


def _stage_weights(wt_ref, wbf_ref):
    n = wt_ref.shape[0]
    n_full = n // LANES * LANES

    @pl.when((pl.program_id(0) == 0) & (pl.program_id(1) == 0))
    def _():
        def body(r, carry):
            rows = pl.ds(pl.multiple_of(r * LANES, LANES), LANES)
            wbf_ref[rows, :] = wt_ref[rows, :].astype(BF16)
            return carry
        lax.fori_loop(0, n_full // LANES, body, 0)
        if n_full < wbf_ref.shape[0]:
            wbf_ref[n_full:, :] = jnp.zeros((wbf_ref.shape[0] - n_full, wbf_ref.shape[1]), BF16)
            wbf_ref[n_full:n, :] = wt_ref[n_full:n, :].astype(BF16)


def _proj_even_kernel(x_ref, mod_ref, wf_ref, wup_ref, gb_ref,
                      aq_ref, ak_ref, av_ref, ag_ref, bq_ref, bk_ref, bv_ref, bg_ref, lff_ref, lfb_ref, h_ref, w_ref):
    _stage_weights(wf_ref, w_ref)
    h_ref[...] = _modulated(x_ref, mod_ref)

    def mm(lo, hi):
        return _dot_nt(h_ref[...], w_ref[lo:hi, :])

    a, kb = WIDTH_A, GLA_HEADS * GLA_DK
    aq_ref[...] = (mm(0, a) * (NA_HEAD_DIM ** -0.5)).astype(BF16)
    o = 4 * a + 2 * kb + 2 * WIDTH_B
    lr = mm(o, o + LANES)
    lr_hi = lr.astype(BF16)
    lr_lo = (lr - lr_hi.astype(F32)).astype(BF16)
    ak_ref[...] = mm(a, 2 * a).astype(BF16)

    def gate_logits(d, out_ref):
        wu = wup_ref[d]
        wu_hi = wu.astype(BF16)
        wu_lo = (wu - wu_hi.astype(F32)).astype(BF16)
        out_ref[...] = _dot(lr_hi, wu_hi) + _dot(lr_lo, wu_hi) + _dot(lr_hi, wu_lo) + gb_ref[d]

    def finish_gate(out_ref):
        z = out_ref[...]
        log_sig = jnp.minimum(z, 0.0) - jnp.log(1.0 + jnp.exp2(jnp.abs(z) * (-LOG2_E)))
        out_ref[...] = log_sig * (LOG2_E / GLA_GATE_NORM)

    gate_logits(0, lff_ref)
    av_ref[...] = mm(2 * a, 3 * a).astype(BF16)
    gate_logits(1, lfb_ref)
    ag_ref[...] = mm(3 * a, 4 * a)
    finish_gate(lff_ref)
    o = 4 * a
    bqk = mm(o, o + 2 * kb)
    bq_ref[...] = (bqk[:, :kb] * (GLA_DK ** -0.5)).astype(BF16)
    bk_ref[...] = bqk[:, kb:].astype(BF16)
    finish_gate(lfb_ref)
    o += 2 * kb
    bv_ref[...] = mm(o, o + WIDTH_B).astype(BF16)
    bg_ref[...] = mm(o + WIDTH_B, o + 2 * WIDTH_B)


def _proj_even(x, mod, w_in, gla_w_up, gla_b):
    bsz, seq, d = x.shape
    kb = GLA_HEADS * GLA_DK
    n_main = 4 * WIDTH_A + 2 * kb + 2 * WIDTH_B
    wt = jnp.swapaxes(w_in, 0, 1)
    wup = jnp.zeros((2, LANES, kb), F32)
    wup = wup.at[0, :GLA_RANK].set(gla_w_up[0]).at[1, GLA_RANK:2 * GLA_RANK].set(gla_w_up[1])
    tm = min(ROW_TILE, seq)
    row = lambda n: pl.BlockSpec((None, tm, n), lambda b, i: (b, i, 0))
    shp = lambda n, dt: jax.ShapeDtypeStruct((bsz, seq, n), dt)
    widths = [(WIDTH_A, BF16), (WIDTH_A, BF16), (WIDTH_A, BF16), (WIDTH_A, F32),
              (kb, BF16), (kb, BF16), (WIDTH_B, BF16), (WIDTH_B, F32), (kb, F32), (kb, F32)]
    return pl.pallas_call(
        _proj_even_kernel,
        grid=(bsz, seq // tm),
        in_specs=[row(d),
                  pl.BlockSpec((None, 1, mod.shape[-1]), lambda b, i: (b, 0, 0)),
                  pl.BlockSpec(wt.shape, lambda b, i: (0, 0), pipeline_mode=pl.Buffered(1)),
                  pl.BlockSpec(wup.shape, lambda b, i: (0, 0, 0)),
                  pl.BlockSpec((2, 1, kb), lambda b, i: (0, 0, 0))],
        out_specs=[row(n) for n, _ in widths],
        out_shape=[shp(n, dt) for n, dt in widths],
        scratch_shapes=[pltpu.VMEM((tm, d), BF16), pltpu.VMEM((n_main + LANES, d), BF16)],
        compiler_params=_cparams("arbitrary", "arbitrary"),
        name="proj_even",
    )(x, mod, wt, wup, gla_b.reshape(2, 1, kb))


def _proj_odd_kernel(x_ref, mod_ref, wf_ref, lbraw_ref,
                     cq_ref, kf_ref, lff_ref, kb_ref, lfb_ref, ci_ref, cg_ref, dz_ref, dxbc_ref, dt_ref, h_ref, w_ref):
    _stage_weights(wf_ref, w_ref)
    h_ref[...] = _modulated(x_ref, mod_ref)

    def mm(lo, hi):
        return _dot_nt(h_ref[...], w_ref[lo:hi, :])

    c = WIDTH_C
    lbr = lbraw_ref[...]
    mx = jnp.maximum(lbr[0:1], lbr[1:2])
    e0, e1 = jnp.exp(lbr[0:1] - mx), jnp.exp(lbr[1:2] - mx)
    lb = e1 / (e0 + e1)
    log_lb, log_ub = jnp.log(lb), jnp.log1p(-lb)
    gate_refs = ((kf_ref, lff_ref), (kb_ref, lfb_ref))
    for d, (_, lf_ref) in enumerate(gate_refs):
        lf_ref[...] = mm((1 + d) * c, (2 + d) * c)

    half = x_ref.shape[0] // 2

    def finish_gate(d, part):
        k_ref, lf_ref = gate_refs[d]
        rows = slice(part * half, (part + 1) * half)
        z = lf_ref[rows, :]
        t = jnp.exp2(jnp.abs(z) * (-LOG2_E))
        u = 1.0 + t
        k_ref[rows, :] = (1.0 - lb) * (jnp.where(z > 0.0, t, 1.0) / u)
        lc = log_ub + (jnp.minimum(z, 0.0) - jnp.log(u))
        w = 1.0 + jnp.exp2(jnp.abs(log_lb - lc) * (-LOG2_E))
        lf_ref[rows, :] = (jnp.maximum(log_lb, lc) + jnp.log(w)) * LOG2_E

    cq_ref[...] = (mm(0, c) * (HGRN_DIM ** -0.5)).astype(BF16)
    finish_gate(0, 0)
    ci_ref[...] = mm(3 * c, 4 * c).astype(BF16)
    finish_gate(0, 1)
    cg_ref[...] = mm(4 * c, 5 * c)
    finish_gate(1, 0)
    o = 5 * c
    dz_ref[...] = mm(o, o + WIDTH_D)
    finish_gate(1, 1)
    o += WIDTH_D
    dxbc_ref[...] = mm(o, o + SSD_CONV_CH)
    o += SSD_CONV_CH
    dt_ref[...] = mm(o, o + LANES)


def _proj_odd(x, mod, w_in, hgrn_lb):
    bsz, seq, d = x.shape
    n_main = 5 * WIDTH_C + WIDTH_D + SSD_CONV_CH
    wt = jnp.swapaxes(w_in, 0, 1)
    tm = min(ROW_TILE, seq)
    row = lambda n: pl.BlockSpec((None, tm, n), lambda b, i: (b, i, 0))
    shp = lambda n, dt: jax.ShapeDtypeStruct((bsz, seq, n), dt)
    widths = [(WIDTH_C, BF16), (WIDTH_C, F32), (WIDTH_C, F32), (WIDTH_C, F32), (WIDTH_C, F32), (WIDTH_C, BF16),
              (WIDTH_C, F32), (WIDTH_D, F32), (SSD_CONV_CH, F32), (LANES, F32)]
    return pl.pallas_call(
        _proj_odd_kernel,
        grid=(bsz, seq // tm),
        in_specs=[row(d),
                  pl.BlockSpec((None, 1, mod.shape[-1]), lambda b, i: (b, 0, 0)),
                  pl.BlockSpec(wt.shape, lambda b, i: (0, 0), pipeline_mode=pl.Buffered(1)),
                  pl.BlockSpec(hgrn_lb.shape, lambda b, i: (0, 0))],
        out_specs=[row(n) for n, _ in widths],
        out_shape=[shp(n, dt) for n, dt in widths],
        scratch_shapes=[pltpu.VMEM((tm, d), BF16), pltpu.VMEM((n_main + LANES, d), BF16)],
        compiler_params=_cparams("arbitrary", "arbitrary"),
        name="proj_odd",
    )(x, mod, wt, hgrn_lb)


def _out_ln_kernel(ya_ref, yb_ref, w_ref, x_hbm, mod_ref, g_ref, b_ref, o_ref, xbuf, sems, *, n_inner):
    tm, d = xbuf.shape[1:]
    wa = ya_ref.shape[-1]
    gate = mod_ref[...][:, 2 * d:]
    sub = OUT_ROW_SUBTILE
    n_steps = pl.num_programs(0) * n_inner
    step = pl.program_id(0) * n_inner + pl.program_id(1)

    def x_copy(s):
        s = jnp.asarray(s, jnp.int32)
        slot = lax.rem(s, OUT_X_SLOTS)
        rows = pl.ds(pl.multiple_of(lax.rem(s, n_inner) * tm, tm), tm)
        return pltpu.make_async_copy(x_hbm.at[lax.div(s, n_inner), rows, :], xbuf.at[slot], sems.at[slot])

    @pl.when(step == 0)
    def _():
        for s in range(OUT_X_SLOTS - 1):
            x_copy(s).start()

    ahead = step + (OUT_X_SLOTS - 1)

    @pl.when(ahead < n_steps)
    def _():
        x_copy(ahead).start()

    x_copy(step).wait()
    x_ref = xbuf.at[lax.rem(step, OUT_X_SLOTS)]

    def project(r):
        rows = slice(r * sub, (r + 1) * sub)
        return rows, _dot(ya_ref[rows, :], w_ref[:wa, :]) + _dot(yb_ref[rows, :], w_ref[wa:, :])

    def normalise(rows, y):
        t = DEEPNORM_ALPHA * x_ref[rows, :] + gate * y
        mu = jnp.mean(t, axis=-1, keepdims=True)
        tc = t - mu
        var = jnp.mean(tc * tc, axis=-1, keepdims=True)
        o_ref[rows, :] = tc * lax.rsqrt(var + LN_EPS) * g_ref[...] + b_ref[...]

    pending = None
    for r in range(x_ref.shape[0] // sub):
        cur = project(r)
        if pending is not None:
            normalise(*pending)
        pending = cur
    normalise(*pending)


def _out_ln(ya, yb, w_out, x, mod, ln_g, ln_b):
    bsz, seq, d = x.shape
    tm = min(OUT_ROW_TILE, seq)
    n_inner = seq // tm
    assert bsz * n_inner >= OUT_X_SLOTS - 1
    w = w_out.astype(BF16)
    row = lambda n: pl.BlockSpec((None, tm, n), lambda b, i: (b, i, 0))
    vec = pl.BlockSpec((1, d), lambda b, i: (0, 0))
    return pl.pallas_call(
        functools.partial(_out_ln_kernel, n_inner=n_inner),
        grid=(bsz, n_inner),
        in_specs=[row(ya.shape[-1]), row(yb.shape[-1]),
                  pl.BlockSpec(w.shape, lambda b, i: (0, 0)),
                  pl.BlockSpec(memory_space=pl.ANY),
                  pl.BlockSpec((None, 1, mod.shape[-1]), lambda b, i: (b, 0, 0)),
                  vec, vec],
        out_specs=row(d),
        out_shape=jax.ShapeDtypeStruct((bsz, seq, d), F32),
        scratch_shapes=[pltpu.VMEM((OUT_X_SLOTS, tm, d), F32), pltpu.SemaphoreType.DMA((OUT_X_SLOTS,))],
        compiler_params=_cparams("arbitrary", "arbitrary"),
        name="out_ln",
    )(ya, yb, w, x, mod, ln_g.reshape(1, d), ln_b.reshape(1, d))


NA_BIAS_TYPES = NA_WIN_ROWS
NA_KEYS = NA_WIN_ROWS * GRID_W


def _na_bias_kernel(rpb_ref, o_ref):
    h = pl.program_id(0)
    shape = (GRID_W, LANES)
    q = lax.broadcasted_iota(jnp.int32, shape, 0)
    lane = lax.broadcasted_iota(jnp.int32, shape, 1)
    kc = lane & (GRID_W - 1)
    dc = kc - q + (NA_WIN_COLS - 1)
    cs = jnp.clip(q - NA_WIN_COLS // 2, 0, GRID_W - NA_WIN_COLS)
    valid = (kc >= cs) & (kc < cs + NA_WIN_COLS)
    upper = lane >= GRID_W
    n_dr, n_dc = 2 * NA_WIN_ROWS - 1, 2 * NA_WIN_COLS - 1
    tiles = []
    for dr in range(n_dr):
        acc = jnp.zeros(shape, F32)
        for d in range(n_dc):
            acc = jnp.where(dc == d, rpb_ref[(h * n_dr + dr) * n_dc + d], acc)
        tiles.append(jnp.where(valid, acc, NEG_INF))
    for t in range(NA_BIAS_TYPES):
        for c in range(NA_KEYS // LANES):
            dr0 = 2 * c + NA_WIN_ROWS - 1 - t
            o_ref[t, :, c * LANES:(c + 1) * LANES] = jnp.where(upper, tiles[dr0 + 1], tiles[dr0])


def _na_bias(rpb):
    return pl.pallas_call(
        _na_bias_kernel,
        grid=(NA_HEADS,),
        in_specs=[pl.BlockSpec(memory_space=pltpu.SMEM)],
        out_specs=pl.BlockSpec((NA_BIAS_TYPES, None, GRID_W, NA_KEYS), lambda h: (0, h, 0, 0)),
        out_shape=jax.ShapeDtypeStruct((NA_BIAS_TYPES, NA_HEADS, GRID_W, NA_KEYS), F32),
        compiler_params=_cparams("arbitrary"),
        name="na_bias",
    )(rpb.reshape(-1))


def _na_kernel(q_ref, k_ref, v_ref, g_ref, bias_ref, o_ref, *, n_rows):
    i = pl.program_id(1)
    lane = lax.broadcasted_iota(jnp.int32, (1, LANES), 1)
    head_mask = [(lane < NA_HEAD_DIM).astype(BF16), (lane >= NA_HEAD_DIM).astype(BF16)]
    half = NA_WIN_ROWS // 2
    pair_cols = [slice(p * LANES, (p + 1) * LANES) for p in range(WIDTH_A // LANES)]

    def stage_scores(rr):
        r = i * NA_ROWS_PER_STEP + rr
        row_start = jnp.clip(r - half, 0, n_rows - NA_WIN_ROWS)
        t = jnp.where(r < half, r, jnp.where(r > n_rows - half, r - (n_rows - NA_WIN_ROWS), half))
        koff = pl.multiple_of(row_start * GRID_W, GRID_W)
        rows = slice(rr * GRID_W, (rr + 1) * GRID_W)
        scores = []
        for p, cols in enumerate(pair_cols):
            qp = q_ref[rows, cols]
            qs = jnp.concatenate([qp * head_mask[0], qp * head_mask[1]], axis=0)
            kp = k_ref[pl.ds(koff, NA_KEYS), cols]
            scores.append(_dot_nt(qs, kp) + bias_ref[t, p])
        return rows, koff, scores

    def stage_out(rows, koff, scores):
        probs, norms = [], []
        for s in scores:
            e = jnp.exp(s - jnp.max(s, axis=-1, keepdims=True))
            norms.append(jnp.sum(e, axis=-1, keepdims=True))
            probs.append(e.astype(BF16))
        for p, cols in enumerate(pair_cols):
            vp = v_ref[pl.ds(koff, NA_KEYS), cols]
            o2 = _dot(probs[p], vp) / norms[p]
            o = jnp.where(lane < NA_HEAD_DIM, o2[:GRID_W], o2[GRID_W:])
            o_ref[rows, cols] = (o * _silu(g_ref[rows, cols])).astype(BF16)

    pending = None
    for rr in range(NA_ROWS_PER_STEP):
        cur = stage_scores(rr)
        if pending is not None:
            stage_out(*pending)
        pending = cur
    stage_out(*pending)


def _neighbourhood_attention(aq, ak, av, ag, bias):
    bsz, seq, w = aq.shape
    n_rows = seq // GRID_W
    tq = NA_ROWS_PER_STEP * GRID_W
    bias = bias.reshape(NA_BIAS_TYPES, NA_HEADS // 2, 2 * GRID_W, NA_KEYS)
    row = pl.BlockSpec((None, tq, w), lambda b, i: (b, i, 0))
    full = pl.BlockSpec((None, seq, w), lambda b, i: (b, 0, 0))
    return pl.pallas_call(
        functools.partial(_na_kernel, n_rows=n_rows),
        grid=(bsz, n_rows // NA_ROWS_PER_STEP),
        in_specs=[row, full, full, row,
                  pl.BlockSpec(bias.shape, lambda b, i: (0, 0, 0, 0))],
        out_specs=row,
        out_shape=jax.ShapeDtypeStruct((bsz, seq, w), BF16),
        compiler_params=_cparams("parallel", "arbitrary"),
        name="na_attn",
    )(aq, ak, av, ag, bias)


def _rec_consts(chunk, reverse):
    nlev = int(math.log2(chunk))
    assert 1 << nlev == chunk
    w = np.zeros(((nlev + 1), chunk, chunk), np.float32)
    lv = np.full((chunk, chunk), -1, np.int32)
    idx = np.arange(chunk)
    for l in range(nlev):
        s = 1 << l
        off = idx % (2 * s)
        mid = idx - off + s - 1
        for p in range(chunk):
            if off[p] >= s:
                w[l, p, mid[p] + 1:p + 1] = 1.0
            else:
                w[l, p, p + 1:mid[p] + 1] = 1.0
        same = (idx[:, None] // (2 * s)) == (idx[None, :] // (2 * s))
        lv[same & (off[:, None] >= s) & (off[None, :] < s)] = l
    lv[idx, idx] = nlev
    w[nlev] = (idx[None, :] <= idx[:, None])
    if reverse:
        w = w[:, ::-1, ::-1]
        lv = lv[::-1, ::-1]
    w = np.concatenate([w[nlev]] + [w[l] for l in REC_SEL_LEVELS], axis=0)
    return (jnp.asarray(w, BF16), jnp.asarray(np.ascontiguousarray(lv)), nlev)


def _rec_kernel(*refs, final, reverse, heads, dk, dv, nlev):
    chunk = REC_CHUNK
    q_ref, k_ref, lf_ref, v_ref, w_ref, lv_ref = refs[:6]
    if final:
        prev_ref, gate_ref, ng_ref, o_ref, st_ref = refs[6:]
    else:
        o_ref, st_ref = refs[6:]

    @pl.when(pl.program_id(1) == 0)
    def _():
        st_ref[...] = jnp.zeros_like(st_ref)

    lane = lax.broadcasted_iota(jnp.int32, (1, LANES), 1)
    row = lax.broadcasted_iota(jnp.int32, (chunk, LANES), 0)
    per_slab = LANES // dk
    n_slabs = heads * dk // LANES
    lv = lv_ref[...]
    level_masks = [lv == l for l in range(nlev + 1)]
    last = 0 if reverse else chunk - 1

    def is_query_side(l):
        off = row & (2 * (1 << l) - 1)
        return (off < (1 << l)) if reverse else (off >= (1 << l))

    query_side = [is_query_side(l) for l in range(nlev) if (1 << l) < SUBLANES]

    def boundary_exponent(b, l):
        s = 1 << l
        parts = []
        for blk in range(chunk // (2 * s)):
            mid = blk * 2 * s + (s if reverse else s - 1)
            b_mid = jnp.broadcast_to(b[mid:mid + 1, :], (s, LANES))
            lo, hi = b[blk * 2 * s:blk * 2 * s + s], b[blk * 2 * s + s:(blk + 1) * 2 * s]
            parts += [lo - b_mid, b_mid - hi] if reverse else [b_mid - lo, hi - b_mid]
        return jnp.concatenate(parts, axis=0)

    def query_or_key(q, k, l):
        if (1 << l) < SUBLANES:
            return jnp.where(query_side[l], q, k)
        s = 1 << l
        parts = []
        for blk in range(chunk // (2 * s)):
            lo, hi = slice(blk * 2 * s, blk * 2 * s + s), slice(blk * 2 * s + s, (blk + 1) * 2 * s)
            parts += [q[lo], k[hi]] if reverse else [k[lo], q[hi]]
        return jnp.concatenate(parts, axis=0)

    head_masks = [((lane >= hh * dk) & (lane < (hh + 1) * dk)) for hh in range(per_slab)]
    head_masks_bf = [hm.astype(BF16) for hm in head_masks]

    def stage_scale(ci):
        rows = slice(ci * chunk, (ci + 1) * chunk)
        lf_all = lf_ref[rows, :]
        sums = _sel_dot2(w_ref[...], lf_all)
        slabs = []
        for s in range(n_slabs):
            cols = slice(s * LANES, (s + 1) * LANES)
            q = q_ref[rows, cols].astype(F32)
            k = k_ref[rows, cols].astype(F32)
            lf = lf_all[:, cols]
            b = sums[:chunk, cols]
            ys = []
            for l in range(nlev):
                if l == 0:
                    e = jnp.where(query_side[0], lf, 0.0)
                elif l in REC_SEL_LEVELS:
                    i0 = (1 + REC_SEL_LEVELS.index(l)) * chunk
                    e = sums[i0:i0 + chunk, cols]
                else:
                    e = boundary_exponent(b, l)
                ys.append((jnp.exp2(e) * query_or_key(q, k, l)).astype(BF16))
            eb = jnp.exp2(b)
            b_end = jnp.broadcast_to(b[last:last + 1, :], (chunk, LANES))
            slabs.append(dict(
                lhs=ys + [q.astype(BF16)], rhs=ys + [k.astype(BF16)],
                q_in=(eb * q).astype(BF16),
                k_up=(jnp.exp2(b_end - b) * k).astype(BF16),
                decay=eb[last:last + 1]))
        return rows, slabs

    lane2 = lax.broadcasted_iota(jnp.int32, (1, 2 * dv), 1)
    v_first, v_second = (lane2 < dv).astype(BF16), (lane2 >= dv).astype(BF16)
    pair_masks = [jnp.concatenate([m, m], axis=1) for m in level_masks]

    def block_diag(a, b_):
        za, zb = jnp.zeros_like(a), jnp.zeros_like(b_)
        return jnp.concatenate([jnp.concatenate([a, zb], axis=1), jnp.concatenate([za, b_], axis=1)], axis=0)

    def pair_operands(slabs, pr, key, l=None):
        pick = (lambda d: d[key][l]) if l is not None else (lambda d: d[key])
        if per_slab == 2:
            t = pick(slabs[pr])
            return t, jnp.concatenate([t * head_masks_bf[0], t * head_masks_bf[1]], axis=0)
        a, b_ = pick(slabs[2 * pr]), pick(slabs[2 * pr + 1])
        return jnp.concatenate([a, b_], axis=1), block_diag(a, b_)

    def stage_intra(slabs):
        mats = []
        for pr in range(heads // 2):
            a_mat = jnp.zeros((chunk, 2 * chunk), F32)
            for l in range(nlev + 1):
                lhs = pair_operands(slabs, pr, "lhs", l)[0]
                rhs = pair_operands(slabs, pr, "rhs", l)[1]
                a_mat = jnp.where(pair_masks[l], _dot_nt(lhs, rhs), a_mat)
            mats.append(a_mat.astype(BF16))
        return mats

    def stage_out(rows, slabs, mats):
        for pr in range(heads // 2):
            pcols = slice(pr * 2 * dv, (pr + 1) * 2 * dv)
            vp = v_ref[rows, pcols]
            v_bd = jnp.concatenate([vp * v_first, vp * v_second], axis=0)
            q_in = pair_operands(slabs, pr, "q_in")[0]
            k_up = pair_operands(slabs, pr, "k_up")[0]
            if per_slab == 2:
                st = st_ref[pr]
                st_bf = st.astype(BF16)
                st_bd = jnp.concatenate([st_bf * head_masks_bf[0], st_bf * head_masks_bf[1]], axis=0)
            else:
                st_a, st_b = st_ref[2 * pr], st_ref[2 * pr + 1]
                st_bd = block_diag(st_a.astype(BF16), st_b.astype(BF16))
            o = _dot(mats[pr], v_bd) + _dot_nt(q_in, st_bd)
            u = _dot_tn(vp, k_up)
            if per_slab == 2:
                st_ref[pr] = st * slabs[pr]["decay"] + jnp.where(head_masks[0], u[:dv], u[dv:])
            else:
                st_ref[2 * pr] = st_a * slabs[2 * pr]["decay"] + u[:dv, :LANES]
                st_ref[2 * pr + 1] = st_b * slabs[2 * pr + 1]["decay"] + u[dv:, LANES:]
            if final:
                for hh in range(2):
                    ocols = slice((2 * pr + hh) * dv, (2 * pr + hh + 1) * dv)
                    tot = prev_ref[rows, ocols] + o[:, hh * dv:(hh + 1) * dv]
                    ms = jnp.mean(tot * tot, axis=-1, keepdims=True)
                    y = tot * lax.rsqrt(ms + RMS_EPS) * ng_ref[...]
                    o_ref[rows, ocols] = (y * _silu(gate_ref[rows, ocols])).astype(o_ref.dtype)
            else:
                o_ref[rows, pcols] = o

    n_sub = q_ref.shape[0] // chunk
    order = list(range(n_sub - 1, -1, -1) if reverse else range(n_sub))
    pending = None
    for ci in order:
        cur = stage_scale(ci)
        if pending is not None:
            stage_out(*pending, stage_intra(pending[1]))
        pending = cur
    stage_out(*pending, stage_intra(pending[1]))


def _gated_recurrence(name, q, k, lf, v, prev, gate, norm_g, *, reverse, heads, dk, dv):
    bsz, seq, _ = q.shape
    chunk = REC_CHUNK
    blk_rows = min(REC_CHUNKS_PER_STEP * chunk, seq)
    n = seq // blk_rows
    final = prev is not None
    w_sel, lv, nlev = _rec_consts(chunk, reverse)
    cidx = (lambda b, i: (b, n - 1 - i, 0)) if reverse else (lambda b, i: (b, i, 0))
    row = lambda width: pl.BlockSpec((None, blk_rows, width), cidx)
    const2 = lambda a: pl.BlockSpec(a.shape, lambda b, i: (0, 0))
    in_specs = [row(heads * dk), row(heads * dk), row(heads * dk), row(heads * dv), const2(w_sel), const2(lv)]
    args = [q, k, lf, v, w_sel, lv]
    if final:
        in_specs += [row(heads * dv), row(heads * dv), pl.BlockSpec((1, dv), lambda b, i: (0, 0))]
        args += [prev, gate, norm_g.reshape(1, dv)]
    return pl.pallas_call(
        functools.partial(_rec_kernel, final=final, reverse=reverse, heads=heads, dk=dk, dv=dv, nlev=nlev),
        grid=(bsz, n),
        in_specs=in_specs,
        out_specs=row(heads * dv),
        out_shape=jax.ShapeDtypeStruct((bsz, seq, heads * dv), BF16 if final else F32),
        scratch_shapes=[pltpu.VMEM((heads * dk // LANES, dv, LANES), F32)],
        compiler_params=_cparams("parallel", "arbitrary"),
        name=f"{name}_{'bwd' if reverse else 'fwd'}",
    )(*args)


def _ssd_consts(chunk, reverse):
    idx = np.arange(chunk)
    if reverse:
        tri = idx[None, :] >= idx[:, None]
        rest = idx[None, :] < idx[:, None]
    else:
        tri = idx[None, :] <= idx[:, None]
        rest = idx[None, :] > idx[:, None]
    sel = np.concatenate([tri, rest], axis=0).astype(np.float32)
    expand = np.zeros((LANES, WIDTH_D), np.float32)
    for h in range(SSD_HEADS):
        expand[h + (SSD_HEADS if reverse else 0), h * SSD_HEAD_DIM:(h + 1) * SSD_HEAD_DIM] = 1.0
    return jnp.asarray(sel, BF16), jnp.asarray(expand, BF16)


def _ssd_kernel(*refs, final, reverse, n_blocks):
    chunk = REC_CHUNK
    if final:
        (xs_ref, bc_ref, dt_ref, dtb_ref, alog_ref, sel_ref, exp_ref,
         prev_ref, dz_ref, dskip_ref, ng_ref, o_ref, st_ref) = refs
    else:
        (x_ref, xprev_ref, xnext_ref, dt_ref, cw_ref, cb_ref, dtb_ref, alog_ref, sel_ref, exp_ref,
         o_ref, xs_ref, bc_ref, ext_ref, st_ref) = refs
    step = pl.program_id(1)
    blk = (n_blocks - 1 - step) if reverse else step
    blk_rows = dt_ref.shape[0]

    @pl.when(step == 0)
    def _():
        st_ref[...] = jnp.zeros_like(st_ref)

    if not final:
        halo = SUBLANES
        ext_ref[0:halo] = jnp.where(blk == 0, 0.0, xprev_ref[...])
        ext_ref[halo:halo + blk_rows] = x_ref[...]
        ext_ref[halo + blk_rows:2 * halo + blk_rows] = jnp.where(blk == n_blocks - 1, 0.0, xnext_ref[...])
        acc = cb_ref[...]
        for tap in range(SSD_CONV):
            o0 = halo - SSD_CONV // 2 + tap
            acc = acc + ext_ref[o0:o0 + blk_rows] * cw_ref[tap:tap + 1]
        xbc = _silu(acc)
        xs_ref[...] = xbc[:, :WIDTH_D]
        bc_ref[...] = xbc[:, WIDTH_D:].astype(BF16)

    gs = SSD_GROUPS * SSD_STATE
    last = 0 if reverse else chunk - 1
    ii = lax.broadcasted_iota(jnp.int32, (chunk, chunk), 0)
    jj = lax.broadcasted_iota(jnp.int32, (chunk, chunk), 1)
    causal = (ii <= jj) if reverse else (ii >= jj)
    lane = lax.broadcasted_iota(jnp.int32, (1, LANES), 1)
    lo_half = lane < SSD_HEAD_DIM
    col0 = SSD_HEADS if reverse else 0
    rep = SSD_HEADS // SSD_GROUPS
    n_pairs = SSD_HEADS // 2
    neg_a = -jnp.exp(alog_ref[...])

    def stage_decay(ci):
        rows = slice(ci * chunk, (ci + 1) * chunk)
        xs = xs_ref[rows, :]
        bm, cm = bc_ref[rows, :gs], bc_ref[rows, gs:]
        dt = _softplus(dt_ref[rows, :] + dtb_ref[...])
        sums = _sel_dot2(sel_ref[...], dt * neg_a)
        a_cum, a_rem = sums[:chunk], sums[chunk:]
        a_cum_t = a_cum.T
        expand = exp_ref[...]
        x_dt = xs * _dot_sel2(dt, expand)
        e_cum = jnp.exp(_dot_sel2(a_cum, expand))
        x_rem = (x_dt * jnp.exp(_dot_sel2(a_rem, expand))).astype(BF16)
        cbs = [_dot_nt(cm[:, g * SSD_STATE:(g + 1) * SSD_STATE], bm[:, g * SSD_STATE:(g + 1) * SSD_STATE])
               for g in range(SSD_GROUPS)]
        mats, stacks = [], []
        for p in range(n_pairs):
            pair = []
            for hh in range(2):
                ch = col0 + 2 * p + hh
                diff = a_cum[:, ch:ch + 1] - a_cum_t[ch:ch + 1, :]
                seg = jnp.exp(jnp.where(causal, diff, NEG_INF))
                pair.append((cbs[(2 * p) // rep] * seg).astype(BF16))
            mats.append(jnp.concatenate(pair, axis=1))
            xp = x_dt[:, p * LANES:(p + 1) * LANES]
            stacks.append(jnp.concatenate([jnp.where(lo_half, xp, 0.0), jnp.where(lo_half, 0.0, xp)],
                                          axis=0).astype(BF16))
        return dict(rows=rows, xs=xs, bm=bm, cm=cm, e_cum=e_cum, x_rem=x_rem, mats=mats, stacks=stacks)

    def stage_out(d):
        rows = d["rows"]
        ys = []
        for p in range(n_pairs):
            g = (2 * p) // rep
            gcols = slice(g * SSD_STATE, (g + 1) * SSD_STATE)
            pcols = slice(p * LANES, (p + 1) * LANES)
            st = st_ref[p]
            y = _dot(d["mats"][p], d["stacks"][p]) + d["e_cum"][:, pcols] * _dot(d["cm"][:, gcols], st.astype(BF16))
            upd = _dot_tn(d["bm"][:, gcols], d["x_rem"][:, pcols])
            st_ref[p] = st * d["e_cum"][last:last + 1, pcols] + upd
            if final:
                y = y + prev_ref[rows, pcols] + dskip_ref[:, pcols] * d["xs"][:, pcols]
                ys.append(y * _silu(dz_ref[rows, pcols]))
            else:
                o_ref[rows, pcols] = y
        if final:
            t = jnp.concatenate(ys, axis=1)
            ms = jnp.mean(t * t, axis=-1, keepdims=True)
            o_ref[rows, :] = (t * lax.rsqrt(ms + RMS_EPS) * ng_ref[...]).astype(o_ref.dtype)

    n_sub = blk_rows // chunk
    order = list(range(n_sub - 1, -1, -1) if reverse else range(n_sub))
    pending = None
    for ci in order:
        cur = stage_decay(ci)
        if pending is not None:
            stage_out(pending)
        pending = cur
    stage_out(pending)


def _ssd(dxbc, dt_raw, conv_w, conv_b, dt_bias, a_log, dz, d_skip, norm_g):
    bsz, seq, nch = dxbc.shape
    chunk = REC_CHUNK
    blk_rows = min(REC_CHUNKS_PER_STEP * chunk, seq)
    n = seq // blk_rows
    hb = blk_rows // SUBLANES
    nb = seq // SUBLANES
    const2 = lambda a: pl.BlockSpec(a.shape, lambda b, i: (0, 0))
    dtb = jnp.zeros((1, LANES), F32).at[0, :2 * SSD_HEADS].set(dt_bias.reshape(-1))
    alog = jnp.full((1, LANES), NEG_INF, F32).at[0, :2 * SSD_HEADS].set(a_log.reshape(-1))
    cb2 = conv_b.reshape(1, nch)
    state = pltpu.VMEM((SSD_HEADS // 2, SSD_STATE, LANES), F32)
    shp = lambda width, dt: jax.ShapeDtypeStruct((bsz, seq, width), dt)

    sel, expand = _ssd_consts(chunk, False)
    row = lambda width: pl.BlockSpec((None, blk_rows, width), lambda b, i: (b, i, 0))
    y_fwd, xs, bc = pl.pallas_call(
        functools.partial(_ssd_kernel, final=False, reverse=False, n_blocks=n),
        grid=(bsz, n),
        in_specs=[row(nch),
                  pl.BlockSpec((None, SUBLANES, nch), lambda b, i: (b, jnp.maximum(i * hb - 1, 0), 0)),
                  pl.BlockSpec((None, SUBLANES, nch), lambda b, i: (b, jnp.minimum((i + 1) * hb, nb - 1), 0)),
                  row(LANES), const2(conv_w), const2(cb2), const2(dtb), const2(alog), const2(sel), const2(expand)],
        out_specs=[row(WIDTH_D), row(WIDTH_D), row(nch - WIDTH_D)],
        out_shape=[shp(WIDTH_D, F32), shp(WIDTH_D, F32), shp(nch - WIDTH_D, BF16)],
        scratch_shapes=[pltpu.VMEM((blk_rows + 2 * SUBLANES, nch), F32), state],
        compiler_params=_cparams("parallel", "arbitrary"),
        name="ssd_fwd",
    )(dxbc, dxbc, dxbc, dt_raw, conv_w, cb2, dtb, alog, sel, expand)

    sel, expand = _ssd_consts(chunk, True)
    row = lambda width: pl.BlockSpec((None, blk_rows, width), lambda b, i: (b, n - 1 - i, 0))
    dsk = jnp.repeat(d_skip, SSD_HEAD_DIM).reshape(1, WIDTH_D)
    ng = norm_g.reshape(1, WIDTH_D)
    return pl.pallas_call(
        functools.partial(_ssd_kernel, final=True, reverse=True, n_blocks=n),
        grid=(bsz, n),
        in_specs=[row(WIDTH_D), row(nch - WIDTH_D), row(LANES), const2(dtb), const2(alog), const2(sel), const2(expand),
                  row(WIDTH_D), row(WIDTH_D), const2(dsk), const2(ng)],
        out_specs=row(WIDTH_D),
        out_shape=shp(WIDTH_D, BF16),
        scratch_shapes=[state],
        compiler_params=_cparams("parallel", "arbitrary"),
        name="ssd_bwd",
    )(xs, bc, dt_raw, dtb, alog, sel, expand, y_fwd, dz, dsk, ng)


def kernel(x, c, ada_w, ada_b, ln_g, ln_b, e_w_in, e_rpb, e_gla_w_up, e_gla_b, e_gla_norm_g, e_w_out,
           o_w_in, hgrn_lb, o_hgrn_norm_g, o_conv_w, o_conv_b, o_dt_bias, o_a_log, o_d_skip,
           o_ssm_norm_g, o_w_out):
    mod = _ada_mod(c, ada_w, ada_b)

    aq, ak, av, ag, bq, bk, bv, bg, lf_f, lf_b = _proj_even(x, mod[0], e_w_in[0], e_gla_w_up[0], e_gla_b[0])
    ya = _neighbourhood_attention(aq, ak, av, ag, _na_bias(e_rpb[0]))
    gla = functools.partial(_gated_recurrence, "gla", heads=GLA_HEADS, dk=GLA_DK, dv=GLA_DV)
    o_fwd = gla(bq, bk, lf_f, bv, None, None, None, reverse=False)
    yb = gla(bq, bk, lf_b, bv, o_fwd, bg, e_gla_norm_g[0], reverse=True)
    x = _out_ln(ya, yb, e_w_out[0], x, mod[0], ln_g[0], ln_b[0])

    cq, ck_f, lf_f, ck_b, lf_b, ci, cg, dz, dxbc, dt_raw = _proj_odd(x, mod[1], o_w_in[0], hgrn_lb)
    hgrn = functools.partial(_gated_recurrence, "hgrn", heads=HGRN_HEADS, dk=HGRN_DIM, dv=HGRN_DIM)
    o_fwd = hgrn(cq, ck_f, lf_f, ci, None, None, None, reverse=False)
    yc = hgrn(cq, ck_b, lf_b, ci, o_fwd, cg, o_hgrn_norm_g[0], reverse=True)
    yd = _ssd(dxbc, dt_raw, o_conv_w[0], o_conv_b[0], o_dt_bias[0], o_a_log[0], dz, o_d_skip[0], o_ssm_norm_g[0])
    return _out_ln(yc, yd, o_w_out[0], x, mod[1], ln_g[1], ln_b[1])
```

```python
import functools
import math

import numpy as np
import jax
import jax.numpy as jnp
from jax import lax
from jax.experimental import pallas as pl
from jax.experimental.pallas import tpu as pltpu

F32 = jnp.float32
BF16 = jnp.bfloat16

GRID_W = 64
NA_HEADS, NA_HEAD_DIM = 8, 64
NA_WIN_ROWS, NA_WIN_COLS = 8, 16
WIDTH_A = NA_HEADS * NA_HEAD_DIM
GLA_HEADS, GLA_DK, GLA_DV, GLA_RANK = 4, 64, 128, 16
GLA_GATE_NORM = 16.0
WIDTH_B = GLA_HEADS * GLA_DV
HGRN_HEADS, HGRN_DIM = 4, 128
WIDTH_C = HGRN_HEADS * HGRN_DIM
SSD_HEADS, SSD_HEAD_DIM, SSD_GROUPS, SSD_STATE, SSD_CONV = 8, 64, 2, 128, 4
WIDTH_D = SSD_HEADS * SSD_HEAD_DIM
SSD_CONV_CH = WIDTH_D + 2 * SSD_GROUPS * SSD_STATE
DEPTH = 2
DEEPNORM_ALPHA = (2 * DEPTH) ** 0.25
LN_EPS = 1e-5
RMS_EPS = 1e-6

LANES = 128
SUBLANES = 8
VMEM_LIMIT = 56 * 1024 * 1024

ROW_TILE = 512
ADA_COL_TILE = 1024
OUT_ROW_TILE = 1024
OUT_X_SLOTS = 3
OUT_ROW_SUBTILE = 256
REC_CHUNK = 128
REC_CHUNKS_PER_STEP = 16
REC_SEL_LEVELS = (1, 2)
NA_ROWS_PER_STEP = 16
NEG_INF = float("-inf")
LOG2_E = math.log2(math.e)


def _cparams(*sem):
    return pltpu.CompilerParams(dimension_semantics=sem, vmem_limit_bytes=VMEM_LIMIT)


def _dot(a, b):
    return jnp.dot(a, b, preferred_element_type=F32)


def _dot_nt(a, b):
    return lax.dot_general(a, b, (((1,), (1,)), ((), ())), preferred_element_type=F32)


def _dot_tn(a, b):
    return lax.dot_general(a, b, (((0,), (0,)), ((), ())), preferred_element_type=F32)


def _sel_dot2(sel, x):
    hi = x.astype(BF16)
    lo = (x - hi.astype(F32)).astype(BF16)
    return _dot(sel, hi) + _dot(sel, lo)


def _dot_sel2(x, sel):
    hi = x.astype(BF16)
    lo = (x - hi.astype(F32)).astype(BF16)
    return _dot(hi, sel) + _dot(lo, sel)


def _sigmoid(x):
    return 1.0 / (1.0 + jnp.exp(-x))


def _silu(x):
    return x * _sigmoid(x)


def _log1pexp_neg_abs(x):
    return jnp.log1p(jnp.exp(-jnp.abs(x)))


def _softplus(x):
    return jnp.maximum(x, 0.0) + _log1pexp_neg_abs(x)


def _ada_kernel(c_ref, w_ref, b_ref, o_ref):
    cond = _silu(c_ref[...]).astype(BF16)
    o_ref[...] = _dot(cond, w_ref[...].astype(BF16)) + b_ref[...]


def _ada_mod(c, ada_w, ada_b):
    bsz, d = c.shape
    bp = -(-bsz // SUBLANES) * SUBLANES
    n3 = ada_w.shape[-1]
    tn = ADA_COL_TILE
    c_pad = jnp.zeros((bp, d), F32).at[:bsz].set(c)
    out = pl.pallas_call(
        _ada_kernel,
        grid=(DEPTH, n3 // tn),
        in_specs=[pl.BlockSpec((bp, d), lambda l, j: (0, 0)),
                  pl.BlockSpec((None, d, tn), lambda l, j: (l, 0, j)),
                  pl.BlockSpec((None, 1, tn), lambda l, j: (l, 0, j))],
        out_specs=pl.BlockSpec((None, bp, tn), lambda l, j: (l, 0, j)),
        out_shape=jax.ShapeDtypeStruct((DEPTH, bp, n3), F32),
        compiler_params=_cparams("arbitrary", "arbitrary"),
        name="ada_mod",
    )(c_pad, ada_w, ada_b.reshape(DEPTH, 1, n3))
    return out[:, :bsz].reshape(DEPTH, bsz, 1, n3)


def _modulated(x_ref, mod_ref):
    d = x_ref.shape[-1]
    mod = mod_ref[...]
    return (x_ref[...] * (1.0 + mod[:, d:2 * d]) + mod[:, :d]).astype(BF16)


def _stage_weights(wt_ref, wbf_ref):
    n = wt_ref.shape[0]
    n_full = n // LANES * LANES

    @pl.when((pl.program_id(0) == 0) & (pl.program_id(1) == 0))
    def _():
        def body(r, carry):
            rows = pl.ds(pl.multiple_of(r * LANES, LANES), LANES)
            wbf_ref[rows, :] = wt_ref[rows, :].astype(BF16)
            return carry
        lax.fori_loop(0, n_full // LANES, body, 0)
        if n_full < wbf_ref.shape[0]:
            wbf_ref[n_full:, :] = jnp.zeros((wbf_ref.shape[0] - n_full, wbf_ref.shape[1]), BF16)
            wbf_ref[n_full:n, :] = wt_ref[n_full:n, :].astype(BF16)


def _proj_even_kernel(x_ref, mod_ref, wf_ref, wup_ref, gb_ref,
                      aq_ref, ak_ref, av_ref, ag_ref, bq_ref, bk_ref, bv_ref, bg_ref, lff_ref, lfb_ref, h_ref, w_ref):
    _stage_weights(wf_ref, w_ref)
    h_ref[...] = _modulated(x_ref, mod_ref)

    def mm(lo, hi):
        return _dot_nt(h_ref[...], w_ref[lo:hi, :])

    a, kb = WIDTH_A, GLA_HEADS * GLA_DK
    aq_ref[...] = (mm(0, a) * (NA_HEAD_DIM ** -0.5)).astype(BF16)
    o = 4 * a + 2 * kb + 2 * WIDTH_B
    lr = mm(o, o + LANES)
    lr_hi = lr.astype(BF16)
    lr_lo = (lr - lr_hi.astype(F32)).astype(BF16)
    ak_ref[...] = mm(a, 2 * a).astype(BF16)

    def gate_logits(d, out_ref):
        wu = wup_ref[d]
        wu_hi = wu.astype(BF16)
        wu_lo = (wu - wu_hi.astype(F32)).astype(BF16)
        out_ref[...] = _dot(lr_hi, wu_hi) + _dot(lr_lo, wu_hi) + _dot(lr_hi, wu_lo) + gb_ref[d]

    def finish_gate(out_ref):
        z = out_ref[...]
        log_sig = jnp.minimum(z, 0.0) - jnp.log(1.0 + jnp.exp2(jnp.abs(z) * (-LOG2_E)))
        out_ref[...] = log_sig * (LOG2_E / GLA_GATE_NORM)

    gate_logits(0, lff_ref)
    av_ref[...] = mm(2 * a, 3 * a).astype(BF16)
    gate_logits(1, lfb_ref)
    ag_ref[...] = mm(3 * a, 4 * a)
    finish_gate(lff_ref)
    o = 4 * a
    bqk = mm(o, o + 2 * kb)
    bq_ref[...] = (bqk[:, :kb] * (GLA_DK ** -0.5)).astype(BF16)
    bk_ref[...] = bqk[:, kb:].astype(BF16)
    finish_gate(lfb_ref)
    o += 2 * kb
    bv_ref[...] = mm(o, o + WIDTH_B).astype(BF16)
    bg_ref[...] = mm(o + WIDTH_B, o + 2 * WIDTH_B)


def _proj_even(x, mod, w_in, gla_w_up, gla_b):
    bsz, seq, d = x.shape
    kb = GLA_HEADS * GLA_DK
    n_main = 4 * WIDTH_A + 2 * kb + 2 * WIDTH_B
    wt = jnp.swapaxes(w_in, 0, 1)
    wup = jnp.zeros((2, LANES, kb), F32)
    wup = wup.at[0, :GLA_RANK].set(gla_w_up[0]).at[1, GLA_RANK:2 * GLA_RANK].set(gla_w_up[1])
    tm = min(ROW_TILE, seq)
    row = lambda n: pl.BlockSpec((None, tm, n), lambda b, i: (b, i, 0))
    shp = lambda n, dt: jax.ShapeDtypeStruct((bsz, seq, n), dt)
    widths = [(WIDTH_A, BF16), (WIDTH_A, BF16), (WIDTH_A, BF16), (WIDTH_A, F32),
              (kb, BF16), (kb, BF16), (WIDTH_B, BF16), (WIDTH_B, F32), (kb, F32), (kb, F32)]
    return pl.pallas_call(
        _proj_even_kernel,
        grid=(bsz, seq // tm),
        in_specs=[row(d),
                  pl.BlockSpec((None, 1, mod.shape[-1]), lambda b, i: (b, 0, 0)),
                  pl.BlockSpec(wt.shape, lambda b, i: (0, 0), pipeline_mode=pl.Buffered(1)),
                  pl.BlockSpec(wup.shape, lambda b, i: (0, 0, 0)),
                  pl.BlockSpec((2, 1, kb), lambda b, i: (0, 0, 0))],
        out_specs=[row(n) for n, _ in widths],
        out_shape=[shp(n, dt) for n, dt in widths],
        scratch_shapes=[pltpu.VMEM((tm, d), BF16), pltpu.VMEM((n_main + LANES, d), BF16)],
        compiler_params=_cparams("arbitrary", "arbitrary"),
        name="proj_even",
    )(x, mod, wt, wup, gla_b.reshape(2, 1, kb))


def _proj_odd_kernel(x_ref, mod_ref, wf_ref, lbraw_ref,
                     cq_ref, kf_ref, lff_ref, kb_ref, lfb_ref, ci_ref, cg_ref, dz_ref, dxbc_ref, dt_ref, h_ref, w_ref):
    _stage_weights(wf_ref, w_ref)
    h_ref[...] = _modulated(x_ref, mod_ref)

    def mm(lo, hi):
        return _dot_nt(h_ref[...], w_ref[lo:hi, :])

    c = WIDTH_C
    lbr = lbraw_ref[...]
    mx = jnp.maximum(lbr[0:1], lbr[1:2])
    e0, e1 = jnp.exp(lbr[0:1] - mx), jnp.exp(lbr[1:2] - mx)
    lb = e1 / (e0 + e1)
    log_lb, log_ub = jnp.log(lb), jnp.log1p(-lb)
    gate_refs = ((kf_ref, lff_ref), (kb_ref, lfb_ref))
    for d, (_, lf_ref) in enumerate(gate_refs):
        lf_ref[...] = mm((1 + d) * c, (2 + d) * c)

    half = x_ref.shape[0] // 2

    def finish_gate(d, part):
        k_ref, lf_ref = gate_refs[d]
        rows = slice(part * half, (part + 1) * half)
        z = lf_ref[rows, :]
        t = jnp.exp2(jnp.abs(z) * (-LOG2_E))
        u = 1.0 + t
        k_ref[rows, :] = (1.0 - lb) * (jnp.where(z > 0.0, t, 1.0) / u)
        lc = log_ub + (jnp.minimum(z, 0.0) - jnp.log(u))
        w = 1.0 + jnp.exp2(jnp.abs(log_lb - lc) * (-LOG2_E))
        lf_ref[rows, :] = (jnp.maximum(log_lb, lc) + jnp.log(w)) * LOG2_E

    cq_ref[...] = (mm(0, c) * (HGRN_DIM ** -0.5)).astype(BF16)
    finish_gate(0, 0)
    ci_ref[...] = mm(3 * c, 4 * c).astype(BF16)
    finish_gate(0, 1)
    cg_ref[...] = mm(4 * c, 5 * c)
    finish_gate(1, 0)
    o = 5 * c
    dz_ref[...] = mm(o, o + WIDTH_D)
    finish_gate(1, 1)
    o += WIDTH_D
    dxbc_ref[...] = mm(o, o + SSD_CONV_CH)
    o += SSD_CONV_CH
    dt_ref[...] = mm(o, o + LANES)


def _proj_odd(x, mod, w_in, hgrn_lb):
    bsz, seq, d = x.shape
    n_main = 5 * WIDTH_C + WIDTH_D + SSD_CONV_CH
    wt = jnp.swapaxes(w_in, 0, 1)
    tm = min(ROW_TILE, seq)
    row = lambda n: pl.BlockSpec((None, tm, n), lambda b, i: (b, i, 0))
    shp = lambda n, dt: jax.ShapeDtypeStruct((bsz, seq, n), dt)
    widths = [(WIDTH_C, BF16), (WIDTH_C, F32), (WIDTH_C, F32), (WIDTH_C, F32), (WIDTH_C, F32), (WIDTH_C, BF16),
              (WIDTH_C, F32), (WIDTH_D, F32), (SSD_CONV_CH, F32), (LANES, F32)]
    return pl.pallas_call(
        _proj_odd_kernel,
        grid=(bsz, seq // tm),
        in_specs=[row(d),
                  pl.BlockSpec((None, 1, mod.shape[-1]), lambda b, i: (b, 0, 0)),
                  pl.BlockSpec(wt.shape, lambda b, i: (0, 0), pipeline_mode=pl.Buffered(1)),
                  pl.BlockSpec(hgrn_lb.shape, lambda b, i: (0, 0))],
        out_specs=[row(n) for n, _ in widths],
        out_shape=[shp(n, dt) for n, dt in widths],
        scratch_shapes=[pltpu.VMEM((tm, d), BF16), pltpu.VMEM((n_main + LANES, d), BF16)],
        compiler_params=_cparams("arbitrary", "arbitrary"),
        name="proj_odd",
    )(x, mod, wt, hgrn_lb)


def _out_ln_kernel(ya_ref, yb_ref, w_ref, x_hbm, mod_ref, g_ref, b_ref, o_ref, xbuf, sems, *, n_inner):
    tm, d = xbuf.shape[1:]
    wa = ya_ref.shape[-1]
    gate = mod_ref[...][:, 2 * d:]
    sub = OUT_ROW_SUBTILE
    n_steps = pl.num_programs(0) * n_inner
    step = pl.program_id(0) * n_inner + pl.program_id(1)

    def x_copy(s):
        s = jnp.asarray(s, jnp.int32)
        slot = lax.rem(s, OUT_X_SLOTS)
        rows = pl.ds(pl.multiple_of(lax.rem(s, n_inner) * tm, tm), tm)
        return pltpu.make_async_copy(x_hbm.at[lax.div(s, n_inner), rows, :], xbuf.at[slot], sems.at[slot])

    @pl.when(step == 0)
    def _():
        for s in range(OUT_X_SLOTS - 1):
            x_copy(s).start()

    ahead = step + (OUT_X_SLOTS - 1)

    @pl.when(ahead < n_steps)
    def _():
        x_copy(ahead).start()

    x_copy(step).wait()
    x_ref = xbuf.at[lax.rem(step, OUT_X_SLOTS)]

    def project(r):
        rows = slice(r * sub, (r + 1) * sub)
        return rows, _dot(ya_ref[rows, :], w_ref[:wa, :]) + _dot(yb_ref[rows, :], w_ref[wa:, :])

    def normalise(rows, y):
        t = DEEPNORM_ALPHA * x_ref[rows, :] + gate * y
        mu = jnp.mean(t, axis=-1, keepdims=True)
        tc = t - mu
        var = jnp.mean(tc * tc, axis=-1, keepdims=True)
        o_ref[rows, :] = tc * lax.rsqrt(var + LN_EPS) * g_ref[...] + b_ref[...]

    pending = None
    for r in range(x_ref.shape[0] // sub):
        cur = project(r)
        if pending is not None:
            normalise(*pending)
        pending = cur
    normalise(*pending)


def _out_ln(ya, yb, w_out, x, mod, ln_g, ln_b):
    bsz, seq, d = x.shape
    tm = min(OUT_ROW_TILE, seq)
    n_inner = seq // tm
    assert bsz * n_inner >= OUT_X_SLOTS - 1
    w = w_out.astype(BF16)
    row = lambda n: pl.BlockSpec((None, tm, n), lambda b, i: (b, i, 0))
    vec = pl.BlockSpec((1, d), lambda b, i: (0, 0))
    return pl.pallas_call(
        functools.partial(_out_ln_kernel, n_inner=n_inner),
        grid=(bsz, n_inner),
        in_specs=[row(ya.shape[-1]), row(yb.shape[-1]),
                  pl.BlockSpec(w.shape, lambda b, i: (0, 0)),
                  pl.BlockSpec(memory_space=pl.ANY),
                  pl.BlockSpec((None, 1, mod.shape[-1]), lambda b, i: (b, 0, 0)),
                  vec, vec],
        out_specs=row(d),
        out_shape=jax.ShapeDtypeStruct((bsz, seq, d), F32),
        scratch_shapes=[pltpu.VMEM((OUT_X_SLOTS, tm, d), F32), pltpu.SemaphoreType.DMA((OUT_X_SLOTS,))],
        compiler_params=_cparams("arbitrary", "arbitrary"),
        name="out_ln",
    )(ya, yb, w, x, mod, ln_g.reshape(1, d), ln_b.reshape(1, d))


NA_BIAS_TYPES = NA_WIN_ROWS
NA_KEYS = NA_WIN_ROWS * GRID_W


def _na_bias_kernel(rpb_ref, o_ref):
    h = pl.program_id(0)
    shape = (GRID_W, LANES)
    q = lax.broadcasted_iota(jnp.int32, shape, 0)
    lane = lax.broadcasted_iota(jnp.int32, shape, 1)
    kc = lane & (GRID_W - 1)
    dc = kc - q + (NA_WIN_COLS - 1)
    cs = jnp.clip(q - NA_WIN_COLS // 2, 0, GRID_W - NA_WIN_COLS)
    valid = (kc >= cs) & (kc < cs + NA_WIN_COLS)
    upper = lane >= GRID_W
    n_dr, n_dc = 2 * NA_WIN_ROWS - 1, 2 * NA_WIN_COLS - 1
    tiles = []
    for dr in range(n_dr):
        acc = jnp.zeros(shape, F32)
        for d in range(n_dc):
            acc = jnp.where(dc == d, rpb_ref[(h * n_dr + dr) * n_dc + d], acc)
        tiles.append(jnp.where(valid, acc, NEG_INF))
    for t in range(NA_BIAS_TYPES):
        for c in range(NA_KEYS // LANES):
            dr0 = 2 * c + NA_WIN_ROWS - 1 - t
            o_ref[t, :, c * LANES:(c + 1) * LANES] = jnp.where(upper, tiles[dr0 + 1], tiles[dr0])


def _na_bias(rpb):
    return pl.pallas_call(
        _na_bias_kernel,
        grid=(NA_HEADS,),
        in_specs=[pl.BlockSpec(memory_space=pltpu.SMEM)],
        out_specs=pl.BlockSpec((NA_BIAS_TYPES, None, GRID_W, NA_KEYS), lambda h: (0, h, 0, 0)),
        out_shape=jax.ShapeDtypeStruct((NA_BIAS_TYPES, NA_HEADS, GRID_W, NA_KEYS), F32),
        compiler_params=_cparams("arbitrary"),
        name="na_bias",
    )(rpb.reshape(-1))


def _na_kernel(q_ref, k_ref, v_ref, g_ref, bias_ref, o_ref, *, n_rows):
    i = pl.program_id(1)
    lane = lax.broadcasted_iota(jnp.int32, (1, LANES), 1)
    head_mask = [(lane < NA_HEAD_DIM).astype(BF16), (lane >= NA_HEAD_DIM).astype(BF16)]
    half = NA_WIN_ROWS // 2
    pair_cols = [slice(p * LANES, (p + 1) * LANES) for p in range(WIDTH_A // LANES)]

    def stage_scores(rr):
        r = i * NA_ROWS_PER_STEP + rr
        row_start = jnp.clip(r - half, 0, n_rows - NA_WIN_ROWS)
        t = jnp.where(r < half, r, jnp.where(r > n_rows - half, r - (n_rows - NA_WIN_ROWS), half))
        koff = pl.multiple_of(row_start * GRID_W, GRID_W)
        rows = slice(rr * GRID_W, (rr + 1) * GRID_W)
        scores = []
        for p, cols in enumerate(pair_cols):
            qp = q_ref[rows, cols]
            qs = jnp.concatenate([qp * head_mask[0], qp * head_mask[1]], axis=0)
            kp = k_ref[pl.ds(koff, NA_KEYS), cols]
            scores.append(_dot_nt(qs, kp) + bias_ref[t, p])
        return rows, koff, scores

    def stage_out(rows, koff, scores):
        probs, norms = [], []
        for s in scores:
            e = jnp.exp(s - jnp.max(s, axis=-1, keepdims=True))
            norms.append(jnp.sum(e, axis=-1, keepdims=True))
            probs.append(e.astype(BF16))
        for p, cols in enumerate(pair_cols):
            vp = v_ref[pl.ds(koff, NA_KEYS), cols]
            o2 = _dot(probs[p], vp) / norms[p]
            o = jnp.where(lane < NA_HEAD_DIM, o2[:GRID_W], o2[GRID_W:])
            o_ref[rows, cols] = (o * _silu(g_ref[rows, cols])).astype(BF16)

    pending = None
    for rr in range(NA_ROWS_PER_STEP):
        cur = stage_scores(rr)
        if pending is not None:
            stage_out(*pending)
        pending = cur
    stage_out(*pending)


def _neighbourhood_attention(aq, ak, av, ag, bias):
    bsz, seq, w = aq.shape
    n_rows = seq // GRID_W
    tq = NA_ROWS_PER_STEP * GRID_W
    bias = bias.reshape(NA_BIAS_TYPES, NA_HEADS // 2, 2 * GRID_W, NA_KEYS)
    row = pl.BlockSpec((None, tq, w), lambda b, i: (b, i, 0))
    full = pl.BlockSpec((None, seq, w), lambda b, i: (b, 0, 0))
    return pl.pallas_call(
        functools.partial(_na_kernel, n_rows=n_rows),
        grid=(bsz, n_rows // NA_ROWS_PER_STEP),
        in_specs=[row, full, full, row,
                  pl.BlockSpec(bias.shape, lambda b, i: (0, 0, 0, 0))],
        out_specs=row,
        out_shape=jax.ShapeDtypeStruct((bsz, seq, w), BF16),
        compiler_params=_cparams("parallel", "arbitrary"),
        name="na_attn",
    )(aq, ak, av, ag, bias)


def _rec_consts(chunk, reverse):
    nlev = int(math.log2(chunk))
    assert 1 << nlev == chunk
    w = np.zeros(((nlev + 1), chunk, chunk), np.float32)
    lv = np.full((chunk, chunk), -1, np.int32)
    idx = np.arange(chunk)
    for l in range(nlev):
        s = 1 << l
        off = idx % (2 * s)
        mid = idx - off + s - 1
        for p in range(chunk):
            if off[p] >= s:
                w[l, p, mid[p] + 1:p + 1] = 1.0
            else:
                w[l, p, p + 1:mid[p] + 1] = 1.0
        same = (idx[:, None] // (2 * s)) == (idx[None, :] // (2 * s))
        lv[same & (off[:, None] >= s) & (off[None, :] < s)] = l
    lv[idx, idx] = nlev
    w[nlev] = (idx[None, :] <= idx[:, None])
    if reverse:
        w = w[:, ::-1, ::-1]
        lv = lv[::-1, ::-1]
    w = np.concatenate([w[nlev]] + [w[l] for l in REC_SEL_LEVELS], axis=0)
    return (jnp.asarray(w, BF16), jnp.asarray(np.ascontiguousarray(lv)), nlev)


def _rec_kernel(*refs, final, reverse, heads, dk, dv, nlev):
    chunk = REC_CHUNK
    q_ref, k_ref, lf_ref, v_ref, w_ref, lv_ref = refs[:6]
    if final:
        prev_ref, gate_ref, ng_ref, o_ref, st_ref = refs[6:]
    else:
        o_ref, st_ref = refs[6:]

    @pl.when(pl.program_id(1) == 0)
    def _():
        st_ref[...] = jnp.zeros_like(st_ref)

    lane = lax.broadcasted_iota(jnp.int32, (1, LANES), 1)
    row = lax.broadcasted_iota(jnp.int32, (chunk, LANES), 0)
    per_slab = LANES // dk
    n_slabs = heads * dk // LANES
    lv = lv_ref[...]
    level_masks = [lv == l for l in range(nlev + 1)]
    last = 0 if reverse else chunk - 1

    def is_query_side(l):
        off = row & (2 * (1 << l) - 1)
        return (off < (1 << l)) if reverse else (off >= (1 << l))

    query_side = [is_query_side(l) for l in range(nlev) if (1 << l) < SUBLANES]

    def boundary_exponent(b, l):
        s = 1 << l
        parts = []
        for blk in range(chunk // (2 * s)):
            mid = blk * 2 * s + (s if reverse else s - 1)
            b_mid = jnp.broadcast_to(b[mid:mid + 1, :], (s, LANES))
            lo, hi = b[blk * 2 * s:blk * 2 * s + s], b[blk * 2 * s + s:(blk + 1) * 2 * s]
            parts += [lo - b_mid, b_mid - hi] if reverse else [b_mid - lo, hi - b_mid]
        return jnp.concatenate(parts, axis=0)

    def query_or_key(q, k, l):
        if (1 << l) < SUBLANES:
            return jnp.where(query_side[l], q, k)
        s = 1 << l
        parts = []
        for blk in range(chunk // (2 * s)):
            lo, hi = slice(blk * 2 * s, blk * 2 * s + s), slice(blk * 2 * s + s, (blk + 1) * 2 * s)
            parts += [q[lo], k[hi]] if reverse else [k[lo], q[hi]]
        return jnp.concatenate(parts, axis=0)

    head_masks = [((lane >= hh * dk) & (lane < (hh + 1) * dk)) for hh in range(per_slab)]
    head_masks_bf = [hm.astype(BF16) for hm in head_masks]

    def stage_scale(ci):
        rows = slice(ci * chunk, (ci + 1) * chunk)
        lf_all = lf_ref[rows, :]
        sums = _sel_dot2(w_ref[...], lf_all)
        slabs = []
        for s in range(n_slabs):
            cols = slice(s * LANES, (s + 1) * LANES)
            q = q_ref[rows, cols].astype(F32)
            k = k_ref[rows, cols].astype(F32)
            lf = lf_all[:, cols]
            b = sums[:chunk, cols]
            ys = []
            for l in range(nlev):
                if l == 0:
                    e = jnp.where(query_side[0], lf, 0.0)
                elif l in REC_SEL_LEVELS:
                    i0 = (1 + REC_SEL_LEVELS.index(l)) * chunk
                    e = sums[i0:i0 + chunk, cols]
                else:
                    e = boundary_exponent(b, l)
                ys.append((jnp.exp2(e) * query_or_key(q, k, l)).astype(BF16))
            eb = jnp.exp2(b)
            b_end = jnp.broadcast_to(b[last:last + 1, :], (chunk, LANES))
            slabs.append(dict(
                lhs=ys + [q.astype(BF16)], rhs=ys + [k.astype(BF16)],
                q_in=(eb * q).astype(BF16),
                k_up=(jnp.exp2(b_end - b) * k).astype(BF16),
                decay=eb[last:last + 1]))
        return rows, slabs

    lane2 = lax.broadcasted_iota(jnp.int32, (1, 2 * dv), 1)
    v_first, v_second = (lane2 < dv).astype(BF16), (lane2 >= dv).astype(BF16)
    pair_masks = [jnp.concatenate([m, m], axis=1) for m in level_masks]

    def block_diag(a, b_):
        za, zb = jnp.zeros_like(a), jnp.zeros_like(b_)
        return jnp.concatenate([jnp.concatenate([a, zb], axis=1), jnp.concatenate([za, b_], axis=1)], axis=0)

    def pair_operands(slabs, pr, key, l=None):
        pick = (lambda d: d[key][l]) if l is not None else (lambda d: d[key])
        if per_slab == 2:
            t = pick(slabs[pr])
            return t, jnp.concatenate([t * head_masks_bf[0], t * head_masks_bf[1]], axis=0)
        a, b_ = pick(slabs[2 * pr]), pick(slabs[2 * pr + 1])
        return jnp.concatenate([a, b_], axis=1), block_diag(a, b_)

    def stage_intra(slabs):
        mats = []
        for pr in range(heads // 2):
            a_mat = jnp.zeros((chunk, 2 * chunk), F32)
            for l in range(nlev + 1):
                lhs = pair_operands(slabs, pr, "lhs", l)[0]
                rhs = pair_operands(slabs, pr, "rhs", l)[1]
                a_mat = jnp.where(pair_masks[l], _dot_nt(lhs, rhs), a_mat)
            mats.append(a_mat.astype(BF16))
        return mats

    def stage_out(rows, slabs, mats):
        for pr in range(heads // 2):
            pcols = slice(pr * 2 * dv, (pr + 1) * 2 * dv)
            vp = v_ref[rows, pcols]
            v_bd = jnp.concatenate([vp * v_first, vp * v_second], axis=0)
            q_in = pair_operands(slabs, pr, "q_in")[0]
            k_up = pair_operands(slabs, pr, "k_up")[0]
            if per_slab == 2:
                st = st_ref[pr]
                st_bf = st.astype(BF16)
                st_bd = jnp.concatenate([st_bf * head_masks_bf[0], st_bf * head_masks_bf[1]], axis=0)
            else:
                st_a, st_b = st_ref[2 * pr], st_ref[2 * pr + 1]
                st_bd = block_diag(st_a.astype(BF16), st_b.astype(BF16))
            o = _dot(mats[pr], v_bd) + _dot_nt(q_in, st_bd)
            u = _dot_tn(vp, k_up)
            if per_slab == 2:
                st_ref[pr] = st * slabs[pr]["decay"] + jnp.where(head_masks[0], u[:dv], u[dv:])
            else:
                st_ref[2 * pr] = st_a * slabs[2 * pr]["decay"] + u[:dv, :LANES]
                st_ref[2 * pr + 1] = st_b * slabs[2 * pr + 1]["decay"] + u[dv:, LANES:]
            if final:
                for hh in range(2):
                    ocols = slice((2 * pr + hh) * dv, (2 * pr + hh + 1) * dv)
                    tot = prev_ref[rows, ocols] + o[:, hh * dv:(hh + 1) * dv]
                    ms = jnp.mean(tot * tot, axis=-1, keepdims=True)
                    y = tot * lax.rsqrt(ms + RMS_EPS) * ng_ref[...]
                    o_ref[rows, ocols] = (y * _silu(gate_ref[rows, ocols])).astype(o_ref.dtype)
            else:
                o_ref[rows, pcols] = o

    n_sub = q_ref.shape[0] // chunk
    order = list(range(n_sub - 1, -1, -1) if reverse else range(n_sub))
    pending = None
    for ci in order:
        cur = stage_scale(ci)
        if pending is not None:
            stage_out(*pending, stage_intra(pending[1]))
        pending = cur
    stage_out(*pending, stage_intra(pending[1]))


def _gated_recurrence(name, q, k, lf, v, prev, gate, norm_g, *, reverse, heads, dk, dv):
    bsz, seq, _ = q.shape
    chunk = REC_CHUNK
    blk_rows = min(REC_CHUNKS_PER_STEP * chunk, seq)
    n = seq // blk_rows
    final = prev is not None
    w_sel, lv, nlev = _rec_consts(chunk, reverse)
    cidx = (lambda b, i: (b, n - 1 - i, 0)) if reverse else (lambda b, i: (b, i, 0))
    row = lambda width: pl.BlockSpec((None, blk_rows, width), cidx)
    const2 = lambda a: pl.BlockSpec(a.shape, lambda b, i: (0, 0))
    in_specs = [row(heads * dk), row(heads * dk), row(heads * dk), row(heads * dv), const2(w_sel), const2(lv)]
    args = [q, k, lf, v, w_sel, lv]
    if final:
        in_specs += [row(heads * dv), row(heads * dv), pl.BlockSpec((1, dv), lambda b, i: (0, 0))]
        args += [prev, gate, norm_g.reshape(1, dv)]
    return pl.pallas_call(
        functools.partial(_rec_kernel, final=final, reverse=reverse, heads=heads, dk=dk, dv=dv, nlev=nlev),
        grid=(bsz, n),
        in_specs=in_specs,
        out_specs=row(heads * dv),
        out_shape=jax.ShapeDtypeStruct((bsz, seq, heads * dv), BF16 if final else F32),
        scratch_shapes=[pltpu.VMEM((heads * dk // LANES, dv, LANES), F32)],
        compiler_params=_cparams("parallel", "arbitrary"),
        name=f"{name}_{'bwd' if reverse else 'fwd'}",
    )(*args)


def _ssd_consts(chunk, reverse):
    idx = np.arange(chunk)
    if reverse:
        tri = idx[None, :] >= idx[:, None]
        rest = idx[None, :] < idx[:, None]
    else:
        tri = idx[None, :] <= idx[:, None]
        rest = idx[None, :] > idx[:, None]
    sel = np.concatenate([tri, rest], axis=0).astype(np.float32)
    expand = np.zeros((LANES, WIDTH_D), np.float32)
    for h in range(SSD_HEADS):
        expand[h + (SSD_HEADS if reverse else 0), h * SSD_HEAD_DIM:(h + 1) * SSD_HEAD_DIM] = 1.0
    return jnp.asarray(sel, BF16), jnp.asarray(expand, BF16)


def _ssd_kernel(*refs, final, reverse, n_blocks):
    chunk = REC_CHUNK
    if final:
        (xs_ref, bc_ref, dt_ref, dtb_ref, alog_ref, sel_ref, exp_ref,
         prev_ref, dz_ref, dskip_ref, ng_ref, o_ref, st_ref) = refs
    else:
        (x_ref, xprev_ref, xnext_ref, dt_ref, cw_ref, cb_ref, dtb_ref, alog_ref, sel_ref, exp_ref,
         o_ref, xs_ref, bc_ref, ext_ref, st_ref) = refs
    step = pl.program_id(1)
    blk = (n_blocks - 1 - step) if reverse else step
    blk_rows = dt_ref.shape[0]

    @pl.when(step == 0)
    def _():
        st_ref[...] = jnp.zeros_like(st_ref)

    if not final:
        halo = SUBLANES
        ext_ref[0:halo] = jnp.where(blk == 0, 0.0, xprev_ref[...])
        ext_ref[halo:halo + blk_rows] = x_ref[...]
        ext_ref[halo + blk_rows:2 * halo + blk_rows] = jnp.where(blk == n_blocks - 1, 0.0, xnext_ref[...])
        acc = cb_ref[...]
        for tap in range(SSD_CONV):
            o0 = halo - SSD_CONV // 2 + tap
            acc = acc + ext_ref[o0:o0 + blk_rows] * cw_ref[tap:tap + 1]
        xbc = _silu(acc)
        xs_ref[...] = xbc[:, :WIDTH_D]
        bc_ref[...] = xbc[:, WIDTH_D:].astype(BF16)

    gs = SSD_GROUPS * SSD_STATE
    last = 0 if reverse else chunk - 1
    ii = lax.broadcasted_iota(jnp.int32, (chunk, chunk), 0)
    jj = lax.broadcasted_iota(jnp.int32, (chunk, chunk), 1)
    causal = (ii <= jj) if reverse else (ii >= jj)
    lane = lax.broadcasted_iota(jnp.int32, (1, LANES), 1)
    lo_half = lane < SSD_HEAD_DIM
    col0 = SSD_HEADS if reverse else 0
    rep = SSD_HEADS // SSD_GROUPS
    n_pairs = SSD_HEADS // 2
    neg_a = -jnp.exp(alog_ref[...])

    def stage_decay(ci):
        rows = slice(ci * chunk, (ci + 1) * chunk)
        xs = xs_ref[rows, :]
        bm, cm = bc_ref[rows, :gs], bc_ref[rows, gs:]
        dt = _softplus(dt_ref[rows, :] + dtb_ref[...])
        sums = _sel_dot2(sel_ref[...], dt * neg_a)
        a_cum, a_rem = sums[:chunk], sums[chunk:]
        a_cum_t = a_cum.T
        expand = exp_ref[...]
        x_dt = xs * _dot_sel2(dt, expand)
        e_cum = jnp.exp(_dot_sel2(a_cum, expand))
        x_rem = (x_dt * jnp.exp(_dot_sel2(a_rem, expand))).astype(BF16)
        cbs = [_dot_nt(cm[:, g * SSD_STATE:(g + 1) * SSD_STATE], bm[:, g * SSD_STATE:(g + 1) * SSD_STATE])
               for g in range(SSD_GROUPS)]
        mats, stacks = [], []
        for p in range(n_pairs):
            pair = []
            for hh in range(2):
                ch = col0 + 2 * p + hh
                diff = a_cum[:, ch:ch + 1] - a_cum_t[ch:ch + 1, :]
                seg = jnp.exp(jnp.where(causal, diff, NEG_INF))
                pair.append((cbs[(2 * p) // rep] * seg).astype(BF16))
            mats.append(jnp.concatenate(pair, axis=1))
            xp = x_dt[:, p * LANES:(p + 1) * LANES]
            stacks.append(jnp.concatenate([jnp.where(lo_half, xp, 0.0), jnp.where(lo_half, 0.0, xp)],
                                          axis=0).astype(BF16))
        return dict(rows=rows, xs=xs, bm=bm, cm=cm, e_cum=e_cum, x_rem=x_rem, mats=mats, stacks=stacks)

    def stage_out(d):
        rows = d["rows"]
        ys = []
        for p in range(n_pairs):
            g = (2 * p) // rep
            gcols = slice(g * SSD_STATE, (g + 1) * SSD_STATE)
            pcols = slice(p * LANES, (p + 1) * LANES)
            st = st_ref[p]
            y = _dot(d["mats"][p], d["stacks"][p]) + d["e_cum"][:, pcols] * _dot(d["cm"][:, gcols], st.astype(BF16))
            upd = _dot_tn(d["bm"][:, gcols], d["x_rem"][:, pcols])
            st_ref[p] = st * d["e_cum"][last:last + 1, pcols] + upd
            if final:
                y = y + prev_ref[rows, pcols] + dskip_ref[:, pcols] * d["xs"][:, pcols]
                ys.append(y * _silu(dz_ref[rows, pcols]))
            else:
                o_ref[rows, pcols] = y
        if final:
            t = jnp.concatenate(ys, axis=1)
            ms = jnp.mean(t * t, axis=-1, keepdims=True)
            o_ref[rows, :] = (t * lax.rsqrt(ms + RMS_EPS) * ng_ref[...]).astype(o_ref.dtype)

    n_sub = blk_rows // chunk
    order = list(range(n_sub - 1, -1, -1) if reverse else range(n_sub))
    pending = None
    for ci in order:
        cur = stage_decay(ci)
        if pending is not None:
            stage_out(pending)
        pending = cur
    stage_out(pending)


def _ssd(dxbc, dt_raw, conv_w, conv_b, dt_bias, a_log, dz, d_skip, norm_g):
    bsz, seq, nch = dxbc.shape
    chunk = REC_CHUNK
    blk_rows = min(REC_CHUNKS_PER_STEP * chunk, seq)
    n = seq // blk_rows
    hb = blk_rows // SUBLANES
    nb = seq // SUBLANES
    const2 = lambda a: pl.BlockSpec(a.shape, lambda b, i: (0, 0))
    dtb = jnp.zeros((1, LANES), F32).at[0, :2 * SSD_HEADS].set(dt_bias.reshape(-1))
    alog = jnp.full((1, LANES), NEG_INF, F32).at[0, :2 * SSD_HEADS].set(a_log.reshape(-1))
    cb2 = conv_b.reshape(1, nch)
    state = pltpu.VMEM((SSD_HEADS // 2, SSD_STATE, LANES), F32)
    shp = lambda width, dt: jax.ShapeDtypeStruct((bsz, seq, width), dt)

    sel, expand = _ssd_consts(chunk, False)
    row = lambda width: pl.BlockSpec((None, blk_rows, width), lambda b, i: (b, i, 0))
    y_fwd, xs, bc = pl.pallas_call(
        functools.partial(_ssd_kernel, final=False, reverse=False, n_blocks=n),
        grid=(bsz, n),
        in_specs=[row(nch),
                  pl.BlockSpec((None, SUBLANES, nch), lambda b, i: (b, jnp.maximum(i * hb - 1, 0), 0)),
                  pl.BlockSpec((None, SUBLANES, nch), lambda b, i: (b, jnp.minimum((i + 1) * hb, nb - 1), 0)),
                  row(LANES), const2(conv_w), const2(cb2), const2(dtb), const2(alog), const2(sel), const2(expand)],
        out_specs=[row(WIDTH_D), row(WIDTH_D), row(nch - WIDTH_D)],
        out_shape=[shp(WIDTH_D, F32), shp(WIDTH_D, F32), shp(nch - WIDTH_D, BF16)],
        scratch_shapes=[pltpu.VMEM((blk_rows + 2 * SUBLANES, nch), F32), state],
        compiler_params=_cparams("parallel", "arbitrary"),
        name="ssd_fwd",
    )(dxbc, dxbc, dxbc, dt_raw, conv_w, cb2, dtb, alog, sel, expand)

    sel, expand = _ssd_consts(chunk, True)
    row = lambda width: pl.BlockSpec((None, blk_rows, width), lambda b, i: (b, n - 1 - i, 0))
    dsk = jnp.repeat(d_skip, SSD_HEAD_DIM).reshape(1, WIDTH_D)
    ng = norm_g.reshape(1, WIDTH_D)
    return pl.pallas_call(
        functools.partial(_ssd_kernel, final=True, reverse=True, n_blocks=n),
        grid=(bsz, n),
        in_specs=[row(WIDTH_D), row(nch - WIDTH_D), row(LANES), const2(dtb), const2(alog), const2(sel), const2(expand),
                  row(WIDTH_D), row(WIDTH_D), const2(dsk), const2(ng)],
        out_specs=row(WIDTH_D),
        out_shape=shp(WIDTH_D, BF16),
        scratch_shapes=[state],
        compiler_params=_cparams("parallel", "arbitrary"),
        name="ssd_bwd",
    )(xs, bc, dt_raw, dtb, alog, sel, expand, y_fwd, dz, dsk, ng)


def kernel(x, c, ada_w, ada_b, ln_g, ln_b, e_w_in, e_rpb, e_gla_w_up, e_gla_b, e_gla_norm_g, e_w_out,
           o_w_in, hgrn_lb, o_hgrn_norm_g, o_conv_w, o_conv_b, o_dt_bias, o_a_log, o_d_skip,
           o_ssm_norm_g, o_w_out):
    bsz, seq, d = x.shape
    assert x.dtype == F32 and d % LANES == 0 and ada_w.shape == (DEPTH, d, 3 * d)
    assert seq % (GRID_W * NA_ROWS_PER_STEP) == 0 and seq // GRID_W >= NA_WIN_ROWS
    for tile in (ROW_TILE, OUT_ROW_TILE, REC_CHUNK * REC_CHUNKS_PER_STEP):
        assert seq % min(tile, seq) == 0 and min(tile, seq) % REC_CHUNK == 0
    mod = _ada_mod(c, ada_w, ada_b)

    aq, ak, av, ag, bq, bk, bv, bg, lf_f, lf_b = _proj_even(x, mod[0], e_w_in[0], e_gla_w_up[0], e_gla_b[0])
    ya = _neighbourhood_attention(aq, ak, av, ag, _na_bias(e_rpb[0]))
    gla = functools.partial(_gated_recurrence, "gla", heads=GLA_HEADS, dk=GLA_DK, dv=GLA_DV)
    o_fwd = gla(bq, bk, lf_f, bv, None, None, None, reverse=False)
    yb = gla(bq, bk, lf_b, bv, o_fwd, bg, e_gla_norm_g[0], reverse=True)
    x = _out_ln(ya, yb, e_w_out[0], x, mod[0], ln_g[0], ln_b[0])

    cq, ck_f, lf_f, ck_b, lf_b, ci, cg, dz, dxbc, dt_raw = _proj_odd(x, mod[1], o_w_in[0], hgrn_lb)
    hgrn = functools.partial(_gated_recurrence, "hgrn", heads=HGRN_HEADS, dk=HGRN_DIM, dv=HGRN_DIM)
    o_fwd = hgrn(cq, ck_f, lf_f, ci, None, None, None, reverse=False)
    yc = hgrn(cq, ck_b, lf_b, ci, o_fwd, cg, o_hgrn_norm_g[0], reverse=True)
    yd = _ssd(dxbc, dt_raw, o_conv_w[0], o_conv_b[0], o_dt_bias[0], o_a_log[0], dz, o_d_skip[0], o_ssm_norm_g[0])
    return _out_ln(yc, yd, o_w_out[0], x, mod[1], ln_g[1], ln_b[1])
```
